```python
import jax, jax.numpy as jnp
from jax import lax
import numpy as np

D_MODEL = 2048
BATCH = 2
SEQ = 16384
DEPTH = 2

HEAD_DIM = 128
N_MIX_HEADS = D_MODEL // HEAD_DIM
N_SB_HEADS = N_MIX_HEADS // 4
N_GLA_HEADS = N_MIX_HEADS - N_SB_HEADS
GLA_DK = HEAD_DIM // 2
GLA_GATE_RANK = 16
GLA_GATE_TAU = 16.0
GLA_CHUNK = 16
SB_BLOCK = 128
CONV_WIDTH = 3
N_MEM = 256
N_XA_HEADS = 4
XA_WIDTH = N_XA_HEADS * HEAD_DIM
D_FF = 2 * D_MODEL
N_EVEN = (DEPTH + 1) // 2
N_ODD = DEPTH // 2
EPS = 1e-6

SB_W = N_SB_HEADS * HEAD_DIM
GLA_K_W = N_GLA_HEADS * GLA_DK
GLA_V_W = N_GLA_HEADS * HEAD_DIM
MIX_WIDTH = SB_W + GLA_V_W
AB_SPLIT_SIZES = (SB_W, SB_W, SB_W, GLA_K_W, GLA_K_W, GLA_V_W, GLA_V_W, GLA_GATE_RANK)
AB_IN_WIDTH = sum(AB_SPLIT_SIZES)
AB_SPLIT_POINTS = [int(v) for v in np.cumsum(AB_SPLIT_SIZES)[:-1]]

kernel_name = "hybrid_stickbreak_gla_shortconv_macaron"


def rmsnorm(x, g):
    xf = x.astype(jnp.float32)
    y = xf * lax.rsqrt(jnp.mean(xf * xf, axis=-1, keepdims=True) + EPS)
    return (y * g.astype(jnp.float32)).astype(x.dtype)


def swiglu_ffn(h, w_gu, w_down):
    gate, up = jnp.split(h @ w_gu, 2, axis=-1)
    return (jax.nn.silu(gate) * up) @ w_down


def to_heads(t, n_heads):
    b, s, w = t.shape
    return t.reshape(b, s, n_heads, w // n_heads).transpose(0, 2, 1, 3)


def from_heads(t):
    b, n, s, d = t.shape
    return t.transpose(0, 2, 1, 3).reshape(b, s, n * d)


def stick_breaking_attention(q, k, v):
    b, h, s, d = q.shape
    nb = s // SB_BLOCK
    qf = q.astype(jnp.float32) * (d ** -0.5)
    kf = k.astype(jnp.float32)
    vf = v.astype(jnp.float32)
    incl_lower = jnp.tril(jnp.ones((SB_BLOCK, SB_BLOCK), jnp.float32))
    outs = []
    for i in range(nb):
        nk = i + 1
        length = nk * SB_BLOCK
        qb = qf[:, :, i * SB_BLOCK:(i + 1) * SB_BLOCK]
        z = jnp.einsum('bhqd,bhkd->bhqk', qb, kf[:, :, :length])
        q_pos = i * SB_BLOCK + jnp.arange(SB_BLOCK)
        causal = jnp.arange(length)[None, :] < q_pos[:, None]
        sp = jnp.where(causal, jax.nn.softplus(z), 0.0).reshape(b, h, SB_BLOCK, nk, SB_BLOCK)
        within = jnp.einsum('bhqnj,js->bhqns', sp, incl_lower)
        later_mat = jnp.tril(jnp.ones((nk, nk), jnp.float32), -1)
        later = jnp.einsum('bhqm,mn->bhqn', sp.sum(axis=-1), later_mat)
        log_w = z.reshape(b, h, SB_BLOCK, nk, SB_BLOCK) - within - later[..., None]
        w = jnp.where(causal.reshape(SB_BLOCK, nk, SB_BLOCK), jnp.exp(log_w), 0.0)
        outs.append(jnp.einsum('bhqk,bhkd->bhqd', w.reshape(b, h, SB_BLOCK, length),
                               vf[:, :, :length]))
    return jnp.concatenate(outs, axis=2)


def gla_chunked(q, k, v, log_a):
    b, h, s, dk = q.shape
    dv = v.shape[-1]
    n = s // GLA_CHUNK

    def to_chunks(t):
        return t.astype(jnp.float32).reshape(b, h, n, GLA_CHUNK, t.shape[-1]).transpose(2, 0, 1, 3, 4)

    qc = to_chunks(q.astype(jnp.float32) * (dk ** -0.5))
    kc, vc, gc = to_chunks(k), to_chunks(v), to_chunks(log_a)
    incl = jnp.tril(jnp.ones((GLA_CHUNK, GLA_CHUNK), dtype=bool))

    def step(state, inp):
        qi, ki, vi, gi = inp
        bcum = jnp.cumsum(gi, axis=2)
        inter = jnp.einsum('bhcd,bhde->bhce', qi * jnp.exp(bcum), state)
        diff = bcum[:, :, :, None, :] - bcum[:, :, None, :, :]
        decay = jnp.exp(jnp.where(incl[:, :, None], diff, -jnp.inf))
        scores = jnp.einsum('bhtd,bhtsd,bhsd->bhts', qi, decay, ki)
        intra = jnp.einsum('bhts,bhse->bhte', scores, vi)
        b_last = bcum[:, :, -1:, :]
        new_state = (jnp.exp(b_last[:, :, 0, :])[..., None] * state
                     + jnp.einsum('bhcd,bhce->bhde', ki * jnp.exp(b_last - bcum), vi))
        return new_state, inter + intra

    state0 = jnp.zeros((b, h, dk, dv), jnp.float32)
    _, out = lax.scan(step, state0, (qc, kc, vc, gc))
    return out.transpose(1, 2, 0, 3, 4).reshape(b, h, s, dv)


def sb_gla_mixer(h, w_in, sb_qg, sb_kg, gla_w_gate, gla_b_gate, gla_og, w_out):
    proj = h @ w_in
    q_sb, k_sb, v_sb, q_g, k_g, v_g, r_g, g_lr = jnp.split(proj, AB_SPLIT_POINTS, axis=-1)
    o_sb = stick_breaking_attention(rmsnorm(to_heads(q_sb, N_SB_HEADS), sb_qg),
                                    rmsnorm(to_heads(k_sb, N_SB_HEADS), sb_kg),
                                    to_heads(v_sb, N_SB_HEADS))
    gate_logit = (g_lr @ gla_w_gate + gla_b_gate).astype(jnp.float32)
    log_a = jax.nn.log_sigmoid(gate_logit) / GLA_GATE_TAU
    o_g = gla_chunked(to_heads(q_g, N_GLA_HEADS), to_heads(k_g, N_GLA_HEADS),
                      to_heads(v_g, N_GLA_HEADS), to_heads(log_a, N_GLA_HEADS))
    o_g = from_heads(rmsnorm(o_g, gla_og)).astype(h.dtype) * jax.nn.silu(r_g)
    mixed = jnp.concatenate([from_heads(o_sb).astype(h.dtype), o_g], axis=-1)
    return mixed @ w_out


def short_conv_mixer(h, w_in, conv_w, w_out):
    bg, cg, u = jnp.split(h @ w_in, 3, axis=-1)
    y = lax.conv_general_dilated(cg * u, conv_w[:, None, :], window_strides=(1,),
                                 padding=[(CONV_WIDTH - 1, 0)],
                                 dimension_numbers=('NWC', 'WIO', 'NWC'),
                                 feature_group_count=D_MODEL)
    return (bg * y) @ w_out


def memory_cross_attention(h, m, w_q, w_kv, q_g, k_g, w_o):
    q = rmsnorm(to_heads(h @ w_q, N_XA_HEADS), q_g)
    k_m, v_m = jnp.split(m @ w_kv, 2, axis=-1)
    k = rmsnorm(to_heads(k_m, N_XA_HEADS), k_g)
    v = to_heads(v_m, N_XA_HEADS)
    scores = jnp.einsum('bhsd,bhmd->bhsm', q.astype(jnp.float32), k.astype(jnp.float32)) * (HEAD_DIM ** -0.5)
    p = jax.nn.softmax(scores, axis=-1)
    o = jnp.einsum('bhsm,bhmd->bhsd', p, v.astype(jnp.float32))
    return from_heads(o).astype(h.dtype) @ w_o


def setup_inputs(seed: int = 0) -> dict:
    key = jax.random.key(seed)
    ks = jax.random.split(key, 26)
    f32 = jnp.float32
    L, NE, NO = DEPTH, N_EVEN, N_ODD

    def dense(k, shape, fan_in):
        return jax.random.normal(k, shape, f32) * (fan_in ** -0.5)

    def gain(k, shape):
        return 1.0 + 0.02 * jax.random.normal(k, shape, f32)

    return {
        "x": jax.random.normal(ks[0], (BATCH, SEQ, D_MODEL), f32),
        "mem": jax.random.normal(ks[1], (BATCH, N_MEM, D_MODEL), f32),
        "ffn1_norm": gain(ks[2], (L, D_MODEL)),
        "ffn1_w_gu": dense(ks[3], (L, D_MODEL, 2 * D_FF), D_MODEL),
        "ffn1_w_down": dense(ks[4], (L, D_FF, D_MODEL), D_FF),
        "mix_norm": gain(ks[5], (L, D_MODEL)),
        "ab_w_in": dense(ks[6], (NE, D_MODEL, AB_IN_WIDTH), D_MODEL),
        "sb_q_norm": gain(ks[7], (NE, HEAD_DIM)),
        "sb_k_norm": gain(ks[8], (NE, HEAD_DIM)),
        "gla_w_gate": dense(ks[9], (NE, GLA_GATE_RANK, GLA_K_W), GLA_GATE_RANK),
        "gla_b_gate": 0.1 * jax.random.normal(ks[10], (NE, GLA_K_W), f32),
        "gla_o_norm": gain(ks[11], (NE, HEAD_DIM)),
        "ab_w_out": dense(ks[12], (NE, MIX_WIDTH, D_MODEL), MIX_WIDTH),
        "conv_w_in": dense(ks[13], (NO, D_MODEL, 3 * D_MODEL), D_MODEL),
        "conv_w": dense(ks[14], (NO, CONV_WIDTH, D_MODEL), CONV_WIDTH),
        "conv_w_out": dense(ks[15], (NO, D_MODEL, D_MODEL), D_MODEL),
        "xa_norm": gain(ks[16], (L, D_MODEL)),
        "mem_norm": gain(ks[17], (L, D_MODEL)),
        "xa_w_q": dense(ks[18], (L, D_MODEL, XA_WIDTH), D_MODEL),
        "xa_w_kv": dense(ks[19], (L, D_MODEL, 2 * XA_WIDTH), D_MODEL),
        "xa_q_norm": gain(ks[20], (L, HEAD_DIM)),
        "xa_k_norm": gain(ks[21], (L, HEAD_DIM)),
        "xa_w_o": dense(ks[22], (L, XA_WIDTH, D_MODEL), XA_WIDTH),
        "ffn2_norm": gain(ks[23], (L, D_MODEL)),
        "ffn2_w_gu": dense(ks[24], (L, D_MODEL, 2 * D_FF), D_MODEL),
        "ffn2_w_down": dense(ks[25], (L, D_FF, D_MODEL), D_FF),
    }


def reference(x, mem, ffn1_norm, ffn1_w_gu, ffn1_w_down, mix_norm, ab_w_in, sb_q_norm,
              sb_k_norm, gla_w_gate, gla_b_gate, gla_o_norm, ab_w_out, conv_w_in, conv_w,
              conv_w_out, xa_norm, mem_norm, xa_w_q, xa_w_kv, xa_q_norm, xa_k_norm, xa_w_o,
              ffn2_norm, ffn2_w_gu, ffn2_w_down):
    for layer in range(DEPTH):
        x = x + 0.5 * swiglu_ffn(rmsnorm(x, ffn1_norm[layer]), ffn1_w_gu[layer], ffn1_w_down[layer])
        h = rmsnorm(x, mix_norm[layer])
        i = layer // 2
        if layer % 2 == 0:
            x = x + sb_gla_mixer(h, ab_w_in[i], sb_q_norm[i], sb_k_norm[i], gla_w_gate[i],
                                 gla_b_gate[i], gla_o_norm[i], ab_w_out[i])
        else:
            x = x + short_conv_mixer(h, conv_w_in[i], conv_w[i], conv_w_out[i])
        x = x + memory_cross_attention(rmsnorm(x, xa_norm[layer]), rmsnorm(mem, mem_norm[layer]),
                                       xa_w_q[layer], xa_w_kv[layer], xa_q_norm[layer],
                                       xa_k_norm[layer], xa_w_o[layer])
        x = x + 0.5 * swiglu_ffn(rmsnorm(x, ffn2_norm[layer]), ffn2_w_gu[layer], ffn2_w_down[layer])
    return x
```

```python
import functools

import jax
import jax.numpy as jnp
from jax import lax
from jax.experimental import pallas as pl
from jax.experimental.pallas import tpu as pltpu

F32 = jnp.float32
BF16 = jnp.bfloat16

EPS = 1e-6
HEAD_DIM = 128
GLA_DK = 64
GLA_GATE_RANK = 16
GLA_GATE_TAU = 16.0
N_XA_HEADS = 4
LANES = 128
VMEM_LIMIT = 56 * 1024 * 1024

_NT = (((1,), (1,)), ((), ()))
_TN = (((0,), (0,)), ((), ()))


def _params(sem):
    return pltpu.CompilerParams(dimension_semantics=sem, vmem_limit_bytes=VMEM_LIMIT)


def _dot(a, b):
    return jnp.dot(a, b, preferred_element_type=F32)


def _dot_nt(a, b):
    return lax.dot_general(a, b, _NT, preferred_element_type=F32)


def _dot_tn(a, b):
    return lax.dot_general(a, b, _TN, preferred_element_type=F32)


def _rms(xf, g):
    ms = jnp.mean(xf * xf, axis=-1, keepdims=True)
    return xf * lax.rsqrt(ms + EPS) * g


def _sigmoid(x):
    return 1.0 / (1.0 + jnp.exp(-x))


def _softplus(x):
    return jnp.maximum(x, 0.0) + jnp.log(1.0 + jnp.exp(-jnp.abs(x)))


def _tile(n, pref):
    t = min(pref, n)
    while n % t:
        t //= 2
    return t


def _glu_kernel(x_ref, g_ref, wa_ref, wb_ref, wd_ref, o_ref, h_ref, *, scale, nf):
    f = pl.program_id(1)

    @pl.when(f == 0)
    def _():
        h_ref[...] = _rms(x_ref[...], g_ref[...]).astype(BF16)
        o_ref[...] = jnp.zeros_like(o_ref)

    h = h_ref[...]
    a = _dot(h, wa_ref[...])
    b = _dot(h, wb_ref[...])
    act = (a * _sigmoid(a) * b).astype(BF16)
    o_ref[...] += _dot(act, wd_ref[...])

    @pl.when(f == nf - 1)
    def _():
        o_ref[...] = x_ref[...] + scale * o_ref[...]


def _swiglu_ffn(x2, g, w_gu, w_d, *, scale):
    T, D = x2.shape
    F = w_d.shape[0]
    tm = _tile(T, 512)
    tf = _tile(F, 512)
    nf = F // tf
    return pl.pallas_call(
        functools.partial(_glu_kernel, scale=scale, nf=nf),
        grid=(T // tm, nf),
        in_specs=[
            pl.BlockSpec((tm, D), lambda i, f: (i, 0)),
            pl.BlockSpec((1, D), lambda i, f: (0, 0)),
            pl.BlockSpec((D, tf), lambda i, f: (0, f)),
            pl.BlockSpec((D, tf), lambda i, f: (0, f + nf)),
            pl.BlockSpec((tf, D), lambda i, f: (f, 0)),
        ],
        out_specs=pl.BlockSpec((tm, D), lambda i, f: (i, 0)),
        out_shape=jax.ShapeDtypeStruct((T, D), F32),
        scratch_shapes=[pltpu.VMEM((tm, D), BF16)],
        compiler_params=_params(("parallel", "arbitrary")),
        name="swiglu_ffn",
    )(x2, g, w_gu, w_gu, w_d)


def _conv_kernel(x_ref, g_ref, wb_ref, wc_ref, wu_ref, cw_ref, wo_ref, o_ref, h_ref, halo_ref,
                 *, nc):
    i = pl.program_id(1)
    c = pl.program_id(2)

    @pl.when(c == 0)
    def _():
        h_ref[...] = _rms(x_ref[...], g_ref[...]).astype(BF16)
        o_ref[...] = jnp.zeros_like(o_ref)

    @pl.when(i == 0)
    def _():
        halo_ref[c] = jnp.zeros(halo_ref.shape[1:], F32)

    h = h_ref[...]
    bg = _dot(h, wb_ref[...])
    cu = _dot(h, wc_ref[...]) * _dot(h, wu_ref[...])
    tm = cu.shape[0]
    prev = halo_ref[c]
    p1 = prev[7:8, :]
    p2 = prev[6:7, :]
    row = lax.broadcasted_iota(jnp.int32, cu.shape, 0)
    r1 = jnp.where(row == 0, p1, pltpu.roll(cu, 1, 0))
    r2 = jnp.where(row == 0, p2, jnp.where(row == 1, p1, pltpu.roll(cu, 2, 0)))
    cw = cw_ref[...]
    y = cw[0:1, :] * r2 + cw[1:2, :] * r1 + cw[2:3, :] * cu
    halo_ref[c] = cu[tm - 8:, :]
    o_ref[...] += _dot((bg * y).astype(BF16), wo_ref[...])

    @pl.when(c == nc - 1)
    def _():
        o_ref[...] = x_ref[...] + o_ref[...]


def _conv_mixer(x, g, w_in, conv_w, w_out):
    B, S, D = x.shape
    tm = _tile(S, 512)
    tc = _tile(D, 512)
    nc = D // tc
    return pl.pallas_call(
        functools.partial(_conv_kernel, nc=nc),
        grid=(B, S // tm, nc),
        in_specs=[
            pl.BlockSpec((None, tm, D), lambda b, i, c: (b, i, 0)),
            pl.BlockSpec((1, D), lambda b, i, c: (0, 0)),
            pl.BlockSpec((D, tc), lambda b, i, c: (0, c)),
            pl.BlockSpec((D, tc), lambda b, i, c: (0, c + nc)),
            pl.BlockSpec((D, tc), lambda b, i, c: (0, c + 2 * nc)),
            pl.BlockSpec((3, tc), lambda b, i, c: (0, c)),
            pl.BlockSpec((tc, D), lambda b, i, c: (c, 0)),
        ],
        out_specs=pl.BlockSpec((None, tm, D), lambda b, i, c: (b, i, 0)),
        out_shape=jax.ShapeDtypeStruct((B, S, D), F32),
        scratch_shapes=[pltpu.VMEM((tm, D), BF16), pltpu.VMEM((nc, 8, tc), F32)],
        compiler_params=_params(("arbitrary", "arbitrary", "arbitrary")),
        name="conv_mixer",
    )(x, g, w_in, w_in, w_in, conv_w, w_out)


def _proj_sb_kernel(x_ref, g_ref, w_ref, qg_ref, kg_ref, q_ref, k_ref, v_ref, *, nh):
    h = _rms(x_ref[...], g_ref[...]).astype(BF16)
    acc = _dot(h, w_ref[...])
    w = nh * HEAD_DIM
    qscale = HEAD_DIM ** -0.5
    for hd in range(nh):
        lo = hd * HEAD_DIM
        q = acc[:, lo:lo + HEAD_DIM]
        k = acc[:, w + lo:w + lo + HEAD_DIM]
        q_ref[:, lo:lo + HEAD_DIM] = (_rms(q, qg_ref[...]) * qscale).astype(BF16)
        k_ref[:, lo:lo + HEAD_DIM] = _rms(k, kg_ref[...]).astype(BF16)
    v_ref[...] = acc[:, 2 * w:].astype(BF16)


def _proj_gla_kernel(x_ref, g_ref, w_ref, wg_ref, bg_ref, q_ref, k_ref, la_ref, *, kw):
    h = _rms(x_ref[...], g_ref[...]).astype(BF16)
    acc = _dot(h, w_ref[...])
    q_ref[...] = acc[:, :kw] * (GLA_DK ** -0.5)
    k_ref[...] = acc[:, kw:2 * kw]
    logit = _dot(acc[:, 2 * kw:].astype(BF16), wg_ref[...]) + bg_ref[...]
    la_ref[...] = -_softplus(-logit) * (1.0 / GLA_GATE_TAU)


def _proj_plain_kernel(x_ref, g_ref, w_ref, o_ref, *, silu):
    h = _rms(x_ref[...], g_ref[...]).astype(BF16)
    acc = _dot(h, w_ref[...])
    if silu:
        acc = acc * _sigmoid(acc)
    o_ref[...] = acc.astype(BF16)


def _row_spec(tm, n):
    return pl.BlockSpec((tm, n), lambda i: (i, 0))


def _full_spec(shape):
    return pl.BlockSpec(shape, lambda i: (0,) * len(shape))


def _proj_call(kernel, x2, extra_in, out_widths, out_dtypes, name):
    T, D = x2.shape
    tm = _tile(T, 512)
    in_specs = [_row_spec(tm, D)] + [_full_spec(a.shape) for a in extra_in]
    return pl.pallas_call(
        kernel,
        grid=(T // tm,),
        in_specs=in_specs,
        out_specs=[_row_spec(tm, n) for n in out_widths],
        out_shape=[jax.ShapeDtypeStruct((T, n), dt) for n, dt in zip(out_widths, out_dtypes)],
        compiler_params=_params(("parallel",)),
        name=name,
    )(x2, *extra_in)


def _sb_kernel(q_ref, k_ref, v_ref, o_ref, acc_ref, run_ref, *, tq):
    i = pl.program_id(2)
    acc_ref[...] = jnp.zeros_like(acc_ref)
    run_ref[...] = jnp.zeros_like(run_ref)
    q = q_ref[...]
    row = lax.broadcasted_iota(jnp.int32, (tq, tq), 0)
    col = lax.broadcasted_iota(jnp.int32, (tq, tq), 1)
    suffix = (row >= col).astype(BF16)

    def step(n, carry):
        j = i - n
        start = pl.multiple_of(j * tq, tq)
        k = k_ref[pl.ds(start, tq), :]
        v = v_ref[pl.ds(start, tq), :]
        z = _dot_nt(q, k)
        causal = (col + j * tq) < (row + i * tq)
        sp = jnp.where(causal, _softplus(z), 0.0)
        within = _dot(sp.astype(BF16), suffix)
        run = run_ref[...]
        w = jnp.where(causal, jnp.exp(z - within - run), 0.0)
        acc_ref[...] += _dot(w.astype(BF16), v)
        run_ref[...] = run + jnp.sum(sp, axis=1, keepdims=True)
        return carry

    lax.fori_loop(0, i + 1, step, 0)
    o_ref[...] = acc_ref[...].astype(BF16)


def _stick_breaking(q, k, v, nh):
    B, S, _ = q.shape
    tq = _tile(S, 256)
    return pl.pallas_call(
        functools.partial(_sb_kernel, tq=tq),
        grid=(B, nh, S // tq),
        in_specs=[
            pl.BlockSpec((None, tq, HEAD_DIM), lambda b, h, i: (b, i, h)),
            pl.BlockSpec((None, S, HEAD_DIM), lambda b, h, i: (b, 0, h)),
            pl.BlockSpec((None, S, HEAD_DIM), lambda b, h, i: (b, 0, h)),
        ],
        out_specs=pl.BlockSpec((None, tq, HEAD_DIM), lambda b, h, i: (b, i, h)),
        out_shape=jax.ShapeDtypeStruct(q.shape, BF16),
        scratch_shapes=[pltpu.VMEM((tq, HEAD_DIM), F32), pltpu.VMEM((tq, 1), F32)],
        compiler_params=_params(("parallel", "parallel", "arbitrary")),
        name="stick_breaking",
    )(q, k, v)


def _split3(x):
    a = x.astype(BF16)
    r = x - a.astype(F32)
    b = r.astype(BF16)
    c = (r - b.astype(F32)).astype(BF16)
    return a, b, c


def _gla_kernel(q_ref, k_ref, la_ref, v_ref, r_ref, gn_ref, o_ref, s_ref, *, chunk, npair):
    n = pl.program_id(1)

    @pl.when(n == 0)
    def _():
        s_ref[...] = jnp.zeros_like(s_ref)

    C = chunk
    q = q_ref[...]
    k = k_ref[...]
    g3 = _split3(la_ref[...])
    row = lax.broadcasted_iota(jnp.int32, (C, C), 0)
    col = lax.broadcasted_iota(jnp.int32, (C, C), 1)
    rcol = lax.broadcasted_iota(jnp.int32, (C, 1), 0)

    def rowsum(mask):
        m = mask.astype(BF16)
        return _dot(m, g3[0]) + _dot(m, g3[1]) + _dot(m, g3[2])

    b = rowsum(col <= row)
    b_last = b[C - 1:C, :]

    levels = [(q.astype(BF16), k.astype(BF16), row == col)]
    m = 1
    while m < C:
        boundary = (row // (2 * m)) * (2 * m) + (m - 1)
        ref = rowsum(col <= boundary)
        odd = (rcol // m) % 2 == 1
        qs = jnp.where(odd, q * jnp.exp(jnp.where(odd, b - ref, 0.0)), 0.0)
        ks = jnp.where(odd, 0.0, k * jnp.exp(jnp.where(odd, 0.0, ref - b)))
        levels.append((qs.astype(BF16), ks.astype(BF16), (row // (2 * m)) == (col // (2 * m))))
        m *= 2

    q_inter = (q * jnp.exp(b)).astype(BF16)
    k_state = (k * jnp.exp(b_last - b)).astype(BF16)
    lane = lax.broadcasted_iota(jnp.int32, (C, LANES), 1)
    first = lane < GLA_DK
    sq_first = lax.broadcasted_iota(jnp.int32, (HEAD_DIM, LANES), 1) < GLA_DK
    zero = jnp.zeros((), BF16)

    for p in range(npair):
        ksl = slice(p * LANES, (p + 1) * LANES)
        state = s_ref[p]
        state_b = state.astype(BF16)
        kv = []
        for e in range(2):
            hd = 2 * p + e
            vsl = slice(hd * HEAD_DIM, (hd + 1) * HEAD_DIM)
            mine = first if e == 0 else jnp.logical_not(first)
            scores = jnp.zeros((C, C), F32)
            for qs, ks, mask in levels:
                qe = jnp.where(mine, qs[:, ksl], zero)
                scores += jnp.where(mask, _dot_nt(qe, ks[:, ksl]), 0.0)
            v = v_ref[:, vsl]
            o = _dot(scores.astype(BF16), v)
            o += _dot_nt(jnp.where(mine, q_inter[:, ksl], zero), state_b)
            o = _rms(o, gn_ref[...]) * r_ref[:, vsl].astype(F32)
            o_ref[:, vsl] = o.astype(BF16)
            kv.append(_dot_tn(v, k_state[:, ksl]))
        s_ref[p] = jnp.exp(b_last[:, ksl]) * state + jnp.where(sq_first, kv[0], kv[1])


def _gla(q, k, la, v, r, gn):
    B, S, kw = q.shape
    vw = v.shape[-1]
    chunk = _tile(S, 128)
    npair = kw // LANES
    return pl.pallas_call(
        functools.partial(_gla_kernel, chunk=chunk, npair=npair),
        grid=(B, S // chunk),
        in_specs=[
            pl.BlockSpec((None, chunk, kw), lambda b, n: (b, n, 0)),
            pl.BlockSpec((None, chunk, kw), lambda b, n: (b, n, 0)),
            pl.BlockSpec((None, chunk, kw), lambda b, n: (b, n, 0)),
            pl.BlockSpec((None, chunk, vw), lambda b, n: (b, n, 0)),
            pl.BlockSpec((None, chunk, vw), lambda b, n: (b, n, 0)),
            pl.BlockSpec((1, HEAD_DIM), lambda b, n: (0, 0)),
        ],
        out_specs=pl.BlockSpec((None, chunk, vw), lambda b, n: (b, n, 0)),
        out_shape=jax.ShapeDtypeStruct(v.shape, BF16),
        scratch_shapes=[pltpu.VMEM((npair, HEAD_DIM, LANES), F32)],
        compiler_params=_params(("arbitrary", "arbitrary")),
        name="gla",
    )(q, k, la, v, r, gn)


def _outproj_kernel(x_ref, a_ref, b_ref, wa_ref, wb_ref, o_ref):
    o_ref[...] = x_ref[...] + _dot(a_ref[...], wa_ref[...]) + _dot(b_ref[...], wb_ref[...])


def _out_proj(x2, a, b, w_out):
    T, D = x2.shape
    wa, wb = a.shape[1], b.shape[1]
    tm = _tile(T, 512)
    return pl.pallas_call(
        _outproj_kernel,
        grid=(T // tm,),
        in_specs=[
            _row_spec(tm, D), _row_spec(tm, wa), _row_spec(tm, wb),
            pl.BlockSpec((wa, D), lambda i: (0, 0)),
            pl.BlockSpec((wb, D), lambda i: (0, 0)),
        ],
        out_specs=_row_spec(tm, D),
        out_shape=jax.ShapeDtypeStruct((T, D), F32),
        compiler_params=_params(("parallel",)),
        name="mix_out_proj",
    )(x2, a, b, w_out[:wa], w_out[wa:])


def _memkv_kernel(m_ref, g_ref, w_ref, kg_ref, k_ref, v_ref, *, nh):
    h = _rms(m_ref[...], g_ref[...]).astype(BF16)
    acc = _dot(h, w_ref[...])
    w = nh * HEAD_DIM
    for hd in range(nh):
        sl = slice(hd * HEAD_DIM, (hd + 1) * HEAD_DIM)
        k_ref[:, sl] = _rms(acc[:, sl], kg_ref[...]).astype(BF16)
    v_ref[...] = acc[:, w:].astype(BF16)


def _mem_kv(mem2, g, w_kv, kg):
    R, D = mem2.shape
    w = w_kv.shape[1] // 2
    return pl.pallas_call(
        functools.partial(_memkv_kernel, nh=w // HEAD_DIM),
        grid=(1,),
        in_specs=[_full_spec(mem2.shape), _full_spec(g.shape), _full_spec(w_kv.shape),
                  _full_spec(kg.shape)],
        out_specs=[_full_spec((R, w)), _full_spec((R, w))],
        out_shape=[jax.ShapeDtypeStruct((R, w), BF16)] * 2,
        compiler_params=_params(("arbitrary",)),
        name="mem_kv",
    )(mem2, g, w_kv, kg)


def _xa_kernel(x_ref, g_ref, wq_ref, qg_ref, k_ref, v_ref, wo_ref, o_ref, *, nh):
    x = x_ref[...]
    h = _rms(x, g_ref[...]).astype(BF16)
    qf = _dot(h, wq_ref[...])
    scale = HEAD_DIM ** -0.5
    outs = []
    for hd in range(nh):
        sl = slice(hd * HEAD_DIM, (hd + 1) * HEAD_DIM)
        q = (_rms(qf[:, sl], qg_ref[...]) * scale).astype(BF16)
        s = _dot_nt(q, k_ref[:, sl])
        s = s - jnp.max(s, axis=-1, keepdims=True)
        e = jnp.exp(s)
        p = e / jnp.sum(e, axis=-1, keepdims=True)
        outs.append(_dot(p.astype(BF16), v_ref[:, sl]).astype(BF16))
    o = jnp.concatenate(outs, axis=-1)
    o_ref[...] = x + _dot(o, wo_ref[...])


def _cross_attention(x, g, w_q, qg, k, v, w_o):
    B, S, D = x.shape
    M, W = k.shape[1], k.shape[2]
    tm = _tile(S, 512)
    return pl.pallas_call(
        functools.partial(_xa_kernel, nh=W // HEAD_DIM),
        grid=(B, S // tm),
        in_specs=[
            pl.BlockSpec((None, tm, D), lambda b, i: (b, i, 0)),
            pl.BlockSpec((1, D), lambda b, i: (0, 0)),
            pl.BlockSpec((D, W), lambda b, i: (0, 0)),
            pl.BlockSpec((1, HEAD_DIM), lambda b, i: (0, 0)),
            pl.BlockSpec((None, M, W), lambda b, i: (b, 0, 0)),
            pl.BlockSpec((None, M, W), lambda b, i: (b, 0, 0)),
            pl.BlockSpec((W, D), lambda b, i: (0, 0)),
        ],
        out_specs=pl.BlockSpec((None, tm, D), lambda b, i: (b, i, 0)),
        out_shape=jax.ShapeDtypeStruct((B, S, D), F32),
        compiler_params=_params(("parallel", "parallel")),
        name="mem_cross_attention",
    )(x, g, w_q, qg, k, v, w_o)


def _row(v):
    return v.reshape(1, -1).astype(F32)


def _sb_gla_mixer(x, g, w_in, sb_qg, sb_kg, w_gate, b_gate, gla_og, w_out):
    B, S, D = x.shape
    T = B * S
    x2 = x.reshape(T, D)
    n_heads = D // HEAD_DIM
    n_sb = n_heads // 4
    n_gla = n_heads - n_sb
    sbw = n_sb * HEAD_DIM
    kw = n_gla * GLA_DK
    vw = n_gla * HEAD_DIM
    o = 0
    w_sb = w_in[:, o:o + 3 * sbw]; o += 3 * sbw
    w_qk = w_in[:, o:o + 2 * kw]; o += 2 * kw
    w_v = w_in[:, o:o + vw]; o += vw
    w_r = w_in[:, o:o + vw]; o += vw
    w_g = w_in[:, o:o + GLA_GATE_RANK]
    w_qkg = jnp.concatenate([w_qk, w_g, jnp.zeros((D, LANES - GLA_GATE_RANK), w_in.dtype)], axis=1)
    w_gate_p = jnp.concatenate(
        [w_gate, jnp.zeros((LANES - GLA_GATE_RANK, kw), w_gate.dtype)], axis=0).astype(BF16)

    q_sb, k_sb, v_sb = _proj_call(
        functools.partial(_proj_sb_kernel, nh=n_sb), x2,
        [g, w_sb.astype(BF16), _row(sb_qg), _row(sb_kg)],
        [sbw, sbw, sbw], [BF16] * 3, "proj_sb")
    q_g, k_g, log_a = _proj_call(
        functools.partial(_proj_gla_kernel, kw=kw), x2,
        [g, w_qkg.astype(BF16), w_gate_p, _row(b_gate)],
        [kw, kw, kw], [F32] * 3, "proj_gla_qk")
    (v_g,) = _proj_call(functools.partial(_proj_plain_kernel, silu=False), x2,
                        [g, w_v.astype(BF16)], [vw], [BF16], "proj_gla_v")
    (r_g,) = _proj_call(functools.partial(_proj_plain_kernel, silu=True), x2,
                        [g, w_r.astype(BF16)], [vw], [BF16], "proj_gla_r")

    o_sb = _stick_breaking(q_sb.reshape(B, S, sbw), k_sb.reshape(B, S, sbw),
                           v_sb.reshape(B, S, sbw), n_sb)
    o_g = _gla(q_g.reshape(B, S, kw), k_g.reshape(B, S, kw), log_a.reshape(B, S, kw),
               v_g.reshape(B, S, vw), r_g.reshape(B, S, vw), _row(gla_og))
    out = _out_proj(x2, o_sb.reshape(T, sbw), o_g.reshape(T, vw), w_out.astype(BF16))
    return out.reshape(B, S, D)


def kernel(x, mem, ffn1_norm, ffn1_w_gu, ffn1_w_down, mix_norm, ab_w_in, sb_q_norm, sb_k_norm,
           gla_w_gate, gla_b_gate, gla_o_norm, ab_w_out, conv_w_in, conv_w, conv_w_out, xa_norm,
           mem_norm, xa_w_q, xa_w_kv, xa_q_norm, xa_k_norm, xa_w_o, ffn2_norm, ffn2_w_gu,
           ffn2_w_down):
    B, S, D = x.shape
    T = B * S
    depth = ffn1_norm.shape[0]
    mem2 = mem.reshape(-1, D)
    for layer in range(depth):
        x = _swiglu_ffn(x.reshape(T, D), _row(ffn1_norm[layer]), ffn1_w_gu[layer].astype(BF16),
                        ffn1_w_down[layer].astype(BF16), scale=0.5).reshape(B, S, D)
        i = layer // 2
        if layer % 2 == 0:
            x = _sb_gla_mixer(x, _row(mix_norm[layer]), ab_w_in[i], sb_q_norm[i], sb_k_norm[i],
                              gla_w_gate[i], gla_b_gate[i], gla_o_norm[i], ab_w_out[i])
        else:
            x = _conv_mixer(x, _row(mix_norm[layer]), conv_w_in[i].astype(BF16),
                            conv_w[i].astype(F32), conv_w_out[i].astype(BF16))
        k_m, v_m = _mem_kv(mem2, _row(mem_norm[layer]), xa_w_kv[layer].astype(BF16),
                           _row(xa_k_norm[layer]))
        xw = k_m.shape[1]
        x = _cross_attention(x, _row(xa_norm[layer]), xa_w_q[layer].astype(BF16),
                             _row(xa_q_norm[layer]), k_m.reshape(B, -1, xw),
                             v_m.reshape(B, -1, xw), xa_w_o[layer].astype(BF16))
        x = _swiglu_ffn(x.reshape(T, D), _row(ffn2_norm[layer]), ffn2_w_gu[layer].astype(BF16),
                        ffn2_w_down[layer].astype(BF16), scale=0.5).reshape(B, S, D)
    return x
```

```python
import functools

import jax
import jax.numpy as jnp
from jax import lax
from jax.experimental import pallas as pl
from jax.experimental.pallas import tpu as pltpu

F32 = jnp.float32
BF16 = jnp.bfloat16

EPS = 1e-6
HEAD_DIM = 128
GLA_DK = 64
GLA_GATE_RANK = 16
GLA_GATE_TAU = 16.0
N_XA_HEADS = 4
LANES = 128
LOG2E = 1.4426950408889634
VMEM_LIMIT = 56 * 1024 * 1024

_NT = (((1,), (1,)), ((), ()))
_TN = (((0,), (0,)), ((), ()))


def _params(sem):
    return pltpu.CompilerParams(dimension_semantics=sem, vmem_limit_bytes=VMEM_LIMIT)


def _dot(a, b):
    return jnp.dot(a, b, preferred_element_type=F32)


def _dot_nt(a, b):
    return lax.dot_general(a, b, _NT, preferred_element_type=F32)


def _dot_tn(a, b):
    return lax.dot_general(a, b, _TN, preferred_element_type=F32)


def _rms(xf, g):
    ms = jnp.mean(xf * xf, axis=-1, keepdims=True)
    return xf * lax.rsqrt(ms + EPS) * g


def _sigmoid(x):
    return 1.0 / (1.0 + jnp.exp(-x))


def _softplus(x):
    return jnp.maximum(x, 0.0) + jnp.log(1.0 + jnp.exp(-jnp.abs(x)))


def _tile(n, pref):
    t = min(pref, n)
    while n % t:
        t //= 2
    return t


def _glu_kernel(x_ref, g_ref, wa_ref, wb_ref, wd_ref, o_ref, h_ref, *, scale, nf):
    f = pl.program_id(1)

    @pl.when(f == 0)
    def _():
        h_ref[...] = _rms(x_ref[...], g_ref[...]).astype(BF16)
        o_ref[...] = jnp.zeros_like(o_ref)

    h = h_ref[...]
    a = _dot(h, wa_ref[...])
    b = _dot(h, wb_ref[...])
    act = (a * _sigmoid(a) * b).astype(BF16)
    o_ref[...] += _dot(act, wd_ref[...])

    @pl.when(f == nf - 1)
    def _():
        o_ref[...] = x_ref[...] + scale * o_ref[...]


def _swiglu_ffn(x2, g, w_gu, w_d, *, scale):
    T, D = x2.shape
    F = w_d.shape[0]
    tm = _tile(T, 512)
    tf = _tile(F, 512)
    nf = F // tf
    return pl.pallas_call(
        functools.partial(_glu_kernel, scale=scale, nf=nf),
        grid=(T // tm, nf),
        in_specs=[
            pl.BlockSpec((tm, D), lambda i, f: (i, 0)),
            pl.BlockSpec((1, D), lambda i, f: (0, 0)),
            pl.BlockSpec((D, tf), lambda i, f: (0, f)),
            pl.BlockSpec((D, tf), lambda i, f: (0, f + nf)),
            pl.BlockSpec((tf, D), lambda i, f: (f, 0)),
        ],
        out_specs=pl.BlockSpec((tm, D), lambda i, f: (i, 0)),
        out_shape=jax.ShapeDtypeStruct((T, D), F32),
        scratch_shapes=[pltpu.VMEM((tm, D), BF16)],
        compiler_params=_params(("parallel", "arbitrary")),
        name="swiglu_ffn",
    )(x2, g, w_gu, w_gu, w_d)


def _conv_kernel(x_ref, g_ref, wb_ref, wc_ref, wu_ref, cw_ref, wo_ref, o_ref, h_ref, halo_ref,
                 *, nc):
    i = pl.program_id(1)
    c = pl.program_id(2)

    @pl.when(c == 0)
    def _():
        h_ref[...] = _rms(x_ref[...], g_ref[...]).astype(BF16)
        o_ref[...] = jnp.zeros_like(o_ref)

    @pl.when(i == 0)
    def _():
        halo_ref[c] = jnp.zeros(halo_ref.shape[1:], F32)

    h = h_ref[...]
    bg = _dot(h, wb_ref[...])
    cu = _dot(h, wc_ref[...]) * _dot(h, wu_ref[...])
    tm = cu.shape[0]
    prev = halo_ref[c]
    p1 = prev[7:8, :]
    p2 = prev[6:7, :]
    row = lax.broadcasted_iota(jnp.int32, cu.shape, 0)
    r1 = jnp.where(row == 0, p1, pltpu.roll(cu, 1, 0))
    r2 = jnp.where(row == 0, p2, jnp.where(row == 1, p1, pltpu.roll(cu, 2, 0)))
    cw = cw_ref[...]
    y = cw[0:1, :] * r2 + cw[1:2, :] * r1 + cw[2:3, :] * cu
    halo_ref[c] = cu[tm - 8:, :]
    o_ref[...] += _dot((bg * y).astype(BF16), wo_ref[...])

    @pl.when(c == nc - 1)
    def _():
        o_ref[...] = x_ref[...] + o_ref[...]


def _conv_mixer(x, g, w_in, conv_w, w_out):
    B, S, D = x.shape
    tm = _tile(S, 512)
    tc = _tile(D, 512)
    nc = D // tc
    return pl.pallas_call(
        functools.partial(_conv_kernel, nc=nc),
        grid=(B, S // tm, nc),
        in_specs=[
            pl.BlockSpec((None, tm, D), lambda b, i, c: (b, i, 0)),
            pl.BlockSpec((1, D), lambda b, i, c: (0, 0)),
            pl.BlockSpec((D, tc), lambda b, i, c: (0, c)),
            pl.BlockSpec((D, tc), lambda b, i, c: (0, c + nc)),
            pl.BlockSpec((D, tc), lambda b, i, c: (0, c + 2 * nc)),
            pl.BlockSpec((3, tc), lambda b, i, c: (0, c)),
            pl.BlockSpec((tc, D), lambda b, i, c: (c, 0)),
        ],
        out_specs=pl.BlockSpec((None, tm, D), lambda b, i, c: (b, i, 0)),
        out_shape=jax.ShapeDtypeStruct((B, S, D), F32),
        scratch_shapes=[pltpu.VMEM((tm, D), BF16), pltpu.VMEM((nc, 8, tc), F32)],
        compiler_params=_params(("arbitrary", "arbitrary", "arbitrary")),
        name="conv_mixer",
    )(x, g, w_in, w_in, w_in, conv_w, w_out)


def _proj_sb_kernel(x_ref, g_ref, w_ref, qg_ref, kg_ref, q_ref, k_ref, v_ref, *, nh):
    h = _rms(x_ref[...], g_ref[...]).astype(BF16)
    acc = _dot(h, w_ref[...])
    w = nh * HEAD_DIM
    qscale = LOG2E * HEAD_DIM ** -0.5
    for hd in range(nh):
        lo = hd * HEAD_DIM
        q = acc[:, lo:lo + HEAD_DIM]
        k = acc[:, w + lo:w + lo + HEAD_DIM]
        q_ref[:, lo:lo + HEAD_DIM] = (_rms(q, qg_ref[...]) * qscale).astype(BF16)
        k_ref[:, lo:lo + HEAD_DIM] = _rms(k, kg_ref[...]).astype(BF16)
    v_ref[...] = acc[:, 2 * w:].astype(BF16)


def _proj_gla_kernel(x_ref, g_ref, w_ref, wg_ref, bg_ref, q_ref, k_ref, la_ref, *, kw):
    h = _rms(x_ref[...], g_ref[...]).astype(BF16)
    acc = _dot(h, w_ref[...])
    q_ref[...] = acc[:, :kw] * (GLA_DK ** -0.5)
    k_ref[...] = acc[:, kw:2 * kw]
    logit = _dot(acc[:, 2 * kw:].astype(BF16), wg_ref[...]) + bg_ref[...]
    la_ref[...] = -_softplus(-logit) * (1.0 / GLA_GATE_TAU)


def _proj_plain_kernel(x_ref, g_ref, w_ref, o_ref, *, silu):
    h = _rms(x_ref[...], g_ref[...]).astype(BF16)
    acc = _dot(h, w_ref[...])
    if silu:
        acc = acc * _sigmoid(acc)
    o_ref[...] = acc.astype(BF16)


def _row_spec(tm, n):
    return pl.BlockSpec((tm, n), lambda i: (i, 0))


def _full_spec(shape):
    return pl.BlockSpec(shape, lambda i: (0,) * len(shape))


def _proj_call(kernel, x2, extra_in, out_widths, out_dtypes, name):
    T, D = x2.shape
    tm = _tile(T, 512)
    in_specs = [_row_spec(tm, D)] + [_full_spec(a.shape) for a in extra_in]
    return pl.pallas_call(
        kernel,
        grid=(T // tm,),
        in_specs=in_specs,
        out_specs=[_row_spec(tm, n) for n in out_widths],
        out_shape=[jax.ShapeDtypeStruct((T, n), dt) for n, dt in zip(out_widths, out_dtypes)],
        compiler_params=_params(("parallel",)),
        name=name,
    )(x2, *extra_in)


def _sb_kernel(q_ref, k_ref, v_ref, o_ref, acc_ref, run_ref, *, tq, tk, group):
    i = pl.program_id(2)
    span = tk * group
    acc_ref[...] = jnp.zeros_like(acc_ref)
    run_ref[...] = jnp.zeros_like(run_ref)
    q = q_ref[...]
    row = lax.broadcasted_iota(jnp.int32, (tq, tk), 0)
    col = lax.broadcasted_iota(jnp.int32, (tq, tk), 1)
    krow = lax.broadcasted_iota(jnp.int32, (tk, tk), 0)
    kcol = lax.broadcasted_iota(jnp.int32, (tk, tk), 1)
    suffix = (krow >= kcol).astype(BF16)

    def key_span(gi, masked):
        base = pl.multiple_of(gi * span, span)
        run = run_ref[...]
        ws = [None] * group
        for c in reversed(range(group)):
            start = pl.multiple_of(base + c * tk, tk)
            z = _dot_nt(q, k_ref[pl.ds(start, tk), :])
            sp = jnp.maximum(z, 0.0) + jnp.log(1.0 + jnp.exp2(-jnp.abs(z))) * LOG2E
            if masked:
                causal = (col + start) < (row + i * tq)
                sp = jnp.where(causal, sp, 0.0)
            within = _dot(sp.astype(BF16), suffix)
            w = jnp.exp2(z - within - run)
            if masked:
                w = jnp.where(causal, w, 0.0)
            ws[c] = w.astype(BF16)
            run = run + jnp.sum(sp, axis=1, keepdims=True)
        acc_ref[...] += _dot(jnp.concatenate(ws, axis=1), v_ref[pl.ds(base, span), :])
        run_ref[...] = run

    diag = (i * tq) // span
    key_span(diag, True)

    def step(n, carry):
        key_span(diag - 1 - n, False)
        return carry

    lax.fori_loop(0, diag, step, 0)
    o_ref[...] = acc_ref[...].astype(BF16)


def _stick_breaking(q, k, v, nh):
    B, S, _ = q.shape
    tq = _tile(S, 256)
    tk = tq
    group = _tile(S // tk, 4)
    return pl.pallas_call(
        functools.partial(_sb_kernel, tq=tq, tk=tk, group=group),
        grid=(B, nh, S // tq),
        in_specs=[
            pl.BlockSpec((None, tq, HEAD_DIM), lambda b, h, i: (b, i, h)),
            pl.BlockSpec((None, S, HEAD_DIM), lambda b, h, i: (b, 0, h)),
            pl.BlockSpec((None, S, HEAD_DIM), lambda b, h, i: (b, 0, h)),
        ],
        out_specs=pl.BlockSpec((None, tq, HEAD_DIM), lambda b, h, i: (b, i, h)),
        out_shape=jax.ShapeDtypeStruct(q.shape, BF16),
        scratch_shapes=[pltpu.VMEM((tq, HEAD_DIM), F32), pltpu.VMEM((tq, 1), F32)],
        compiler_params=_params(("parallel", "parallel", "arbitrary")),
        name="stick_breaking",
    )(q, k, v)


def _split3(x):
    a = x.astype(BF16)
    r = x - a.astype(F32)
    b = r.astype(BF16)
    c = (r - b.astype(F32)).astype(BF16)
    return a, b, c


def _gla_kernel(q_ref, k_ref, la_ref, v_ref, r_ref, gn_ref, o_ref, s_ref, *, chunk, npair):
    n = pl.program_id(1)

    @pl.when(n == 0)
    def _():
        s_ref[...] = jnp.zeros_like(s_ref)

    C = chunk
    q = q_ref[...]
    k = k_ref[...]
    g3 = _split3(la_ref[...])
    row = lax.broadcasted_iota(jnp.int32, (C, C), 0)
    col = lax.broadcasted_iota(jnp.int32, (C, C), 1)
    rcol = lax.broadcasted_iota(jnp.int32, (C, 1), 0)

    def rowsum(mask):
        m = mask.astype(BF16)
        return _dot(m, g3[0]) + _dot(m, g3[1]) + _dot(m, g3[2])

    b = rowsum(col <= row)
    b_last = b[C - 1:C, :]

    levels = [(q.astype(BF16), k.astype(BF16), row == col)]
    m = 1
    while m < C:
        boundary = (row // (2 * m)) * (2 * m) + (m - 1)
        ref = rowsum(col <= boundary)
        odd = (rcol // m) % 2 == 1
        qs = jnp.where(odd, q * jnp.exp(jnp.where(odd, b - ref, 0.0)), 0.0)
        ks = jnp.where(odd, 0.0, k * jnp.exp(jnp.where(odd, 0.0, ref - b)))
        levels.append((qs.astype(BF16), ks.astype(BF16), (row // (2 * m)) == (col // (2 * m))))
        m *= 2

    q_inter = (q * jnp.exp(b)).astype(BF16)
    k_state = (k * jnp.exp(b_last - b)).astype(BF16)
    lane = lax.broadcasted_iota(jnp.int32, (C, LANES), 1)
    first = lane < GLA_DK
    sq_first = lax.broadcasted_iota(jnp.int32, (HEAD_DIM, LANES), 1) < GLA_DK
    zero = jnp.zeros((), BF16)

    for p in range(npair):
        ksl = slice(p * LANES, (p + 1) * LANES)
        state = s_ref[p]
        state_b = state.astype(BF16)
        kv = []
        for e in range(2):
            hd = 2 * p + e
            vsl = slice(hd * HEAD_DIM, (hd + 1) * HEAD_DIM)
            mine = first if e == 0 else jnp.logical_not(first)
            scores = jnp.zeros((C, C), F32)
            for qs, ks, mask in levels:
                qe = jnp.where(mine, qs[:, ksl], zero)
                scores += jnp.where(mask, _dot_nt(qe, ks[:, ksl]), 0.0)
            v = v_ref[:, vsl]
            o = _dot(scores.astype(BF16), v)
            o += _dot_nt(jnp.where(mine, q_inter[:, ksl], zero), state_b)
            o = _rms(o, gn_ref[...]) * r_ref[:, vsl].astype(F32)
            o_ref[:, vsl] = o.astype(BF16)
            kv.append(_dot_tn(v, k_state[:, ksl]))
        s_ref[p] = jnp.exp(b_last[:, ksl]) * state + jnp.where(sq_first, kv[0], kv[1])


def _gla(q, k, la, v, r, gn):
    B, S, kw = q.shape
    vw = v.shape[-1]
    chunk = _tile(S, 128)
    npair = kw // LANES
    return pl.pallas_call(
        functools.partial(_gla_kernel, chunk=chunk, npair=npair),
        grid=(B, S // chunk),
        in_specs=[
            pl.BlockSpec((None, chunk, kw), lambda b, n: (b, n, 0)),
            pl.BlockSpec((None, chunk, kw), lambda b, n: (b, n, 0)),
            pl.BlockSpec((None, chunk, kw), lambda b, n: (b, n, 0)),
            pl.BlockSpec((None, chunk, vw), lambda b, n: (b, n, 0)),
            pl.BlockSpec((None, chunk, vw), lambda b, n: (b, n, 0)),
            pl.BlockSpec((1, HEAD_DIM), lambda b, n: (0, 0)),
        ],
        out_specs=pl.BlockSpec((None, chunk, vw), lambda b, n: (b, n, 0)),
        out_shape=jax.ShapeDtypeStruct(v.shape, BF16),
        scratch_shapes=[pltpu.VMEM((npair, HEAD_DIM, LANES), F32)],
        compiler_params=_params(("arbitrary", "arbitrary")),
        name="gla",
    )(q, k, la, v, r, gn)


def _outproj_kernel(x_ref, a_ref, b_ref, wa_ref, wb_ref, o_ref):
    o_ref[...] = x_ref[...] + _dot(a_ref[...], wa_ref[...]) + _dot(b_ref[...], wb_ref[...])


def _out_proj(x2, a, b, w_out):
    T, D = x2.shape
    wa, wb = a.shape[1], b.shape[1]
    tm = _tile(T, 512)
    return pl.pallas_call(
        _outproj_kernel,
        grid=(T // tm,),
        in_specs=[
            _row_spec(tm, D), _row_spec(tm, wa), _row_spec(tm, wb),
            pl.BlockSpec((wa, D), lambda i: (0, 0)),
            pl.BlockSpec((wb, D), lambda i: (0, 0)),
        ],
        out_specs=_row_spec(tm, D),
        out_shape=jax.ShapeDtypeStruct((T, D), F32),
        compiler_params=_params(("parallel",)),
        name="mix_out_proj",
    )(x2, a, b, w_out[:wa], w_out[wa:])


def _memkv_kernel(m_ref, g_ref, w_ref, kg_ref, k_ref, v_ref, *, nh):
    h = _rms(m_ref[...], g_ref[...]).astype(BF16)
    acc = _dot(h, w_ref[...])
    w = nh * HEAD_DIM
    for hd in range(nh):
        sl = slice(hd * HEAD_DIM, (hd + 1) * HEAD_DIM)
        k_ref[:, sl] = _rms(acc[:, sl], kg_ref[...]).astype(BF16)
    v_ref[...] = acc[:, w:].astype(BF16)


def _mem_kv(mem2, g, w_kv, kg):
    R, D = mem2.shape
    w = w_kv.shape[1] // 2
    return pl.pallas_call(
        functools.partial(_memkv_kernel, nh=w // HEAD_DIM),
        grid=(1,),
        in_specs=[_full_spec(mem2.shape), _full_spec(g.shape), _full_spec(w_kv.shape),
                  _full_spec(kg.shape)],
        out_specs=[_full_spec((R, w)), _full_spec((R, w))],
        out_shape=[jax.ShapeDtypeStruct((R, w), BF16)] * 2,
        compiler_params=_params(("arbitrary",)),
        name="mem_kv",
    )(mem2, g, w_kv, kg)


def _xa_kernel(x_ref, g_ref, wq_ref, qg_ref, k_ref, v_ref, wo_ref, o_ref, *, nh):
    x = x_ref[...]
    h = _rms(x, g_ref[...]).astype(BF16)
    qf = _dot(h, wq_ref[...])
    scale = HEAD_DIM ** -0.5
    outs = []
    for hd in range(nh):
        sl = slice(hd * HEAD_DIM, (hd + 1) * HEAD_DIM)
        q = (_rms(qf[:, sl], qg_ref[...]) * scale).astype(BF16)
        s = _dot_nt(q, k_ref[:, sl])
        s = s - jnp.max(s, axis=-1, keepdims=True)
        e = jnp.exp(s)
        p = e / jnp.sum(e, axis=-1, keepdims=True)
        outs.append(_dot(p.astype(BF16), v_ref[:, sl]).astype(BF16))
    o = jnp.concatenate(outs, axis=-1)
    o_ref[...] = x + _dot(o, wo_ref[...])


def _cross_attention(x, g, w_q, qg, k, v, w_o):
    B, S, D = x.shape
    M, W = k.shape[1], k.shape[2]
    tm = _tile(S, 512)
    return pl.pallas_call(
        functools.partial(_xa_kernel, nh=W // HEAD_DIM),
        grid=(B, S // tm),
        in_specs=[
            pl.BlockSpec((None, tm, D), lambda b, i: (b, i, 0)),
            pl.BlockSpec((1, D), lambda b, i: (0, 0)),
            pl.BlockSpec((D, W), lambda b, i: (0, 0)),
            pl.BlockSpec((1, HEAD_DIM), lambda b, i: (0, 0)),
            pl.BlockSpec((None, M, W), lambda b, i: (b, 0, 0)),
            pl.BlockSpec((None, M, W), lambda b, i: (b, 0, 0)),
            pl.BlockSpec((W, D), lambda b, i: (0, 0)),
        ],
        out_specs=pl.BlockSpec((None, tm, D), lambda b, i: (b, i, 0)),
        out_shape=jax.ShapeDtypeStruct((B, S, D), F32),
        compiler_params=_params(("parallel", "parallel")),
        name="mem_cross_attention",
    )(x, g, w_q, qg, k, v, w_o)


def _row(v):
    return v.reshape(1, -1).astype(F32)


def _sb_gla_mixer(x, g, w_in, sb_qg, sb_kg, w_gate, b_gate, gla_og, w_out):
    B, S, D = x.shape
    T = B * S
    x2 = x.reshape(T, D)
    n_heads = D // HEAD_DIM
    n_sb = n_heads // 4
    n_gla = n_heads - n_sb
    sbw = n_sb * HEAD_DIM
    kw = n_gla * GLA_DK
    vw = n_gla * HEAD_DIM
    o = 0
    w_sb = w_in[:, o:o + 3 * sbw]; o += 3 * sbw
    w_qk = w_in[:, o:o + 2 * kw]; o += 2 * kw
    w_v = w_in[:, o:o + vw]; o += vw
    w_r = w_in[:, o:o + vw]; o += vw
    w_g = w_in[:, o:o + GLA_GATE_RANK]
    w_qkg = jnp.concatenate([w_qk, w_g, jnp.zeros((D, LANES - GLA_GATE_RANK), w_in.dtype)], axis=1)
    w_gate_p = jnp.concatenate(
        [w_gate, jnp.zeros((LANES - GLA_GATE_RANK, kw), w_gate.dtype)], axis=0).astype(BF16)

    q_sb, k_sb, v_sb = _proj_call(
        functools.partial(_proj_sb_kernel, nh=n_sb), x2,
        [g, w_sb.astype(BF16), _row(sb_qg), _row(sb_kg)],
        [sbw, sbw, sbw], [BF16] * 3, "proj_sb")
    q_g, k_g, log_a = _proj_call(
        functools.partial(_proj_gla_kernel, kw=kw), x2,
        [g, w_qkg.astype(BF16), w_gate_p, _row(b_gate)],
        [kw, kw, kw], [F32] * 3, "proj_gla_qk")
    (v_g,) = _proj_call(functools.partial(_proj_plain_kernel, silu=False), x2,
                        [g, w_v.astype(BF16)], [vw], [BF16], "proj_gla_v")
    (r_g,) = _proj_call(functools.partial(_proj_plain_kernel, silu=True), x2,
                        [g, w_r.astype(BF16)], [vw], [BF16], "proj_gla_r")

    o_sb = _stick_breaking(q_sb.reshape(B, S, sbw), k_sb.reshape(B, S, sbw),
                           v_sb.reshape(B, S, sbw), n_sb)
    o_g = _gla(q_g.reshape(B, S, kw), k_g.reshape(B, S, kw), log_a.reshape(B, S, kw),
               v_g.reshape(B, S, vw), r_g.reshape(B, S, vw), _row(gla_og))
    out = _out_proj(x2, o_sb.reshape(T, sbw), o_g.reshape(T, vw), w_out.astype(BF16))
    return out.reshape(B, S, D)


def kernel(x, mem, ffn1_norm, ffn1_w_gu, ffn1_w_down, mix_norm, ab_w_in, sb_q_norm, sb_k_norm,
           gla_w_gate, gla_b_gate, gla_o_norm, ab_w_out, conv_w_in, conv_w, conv_w_out, xa_norm,
           mem_norm, xa_w_q, xa_w_kv, xa_q_norm, xa_k_norm, xa_w_o, ffn2_norm, ffn2_w_gu,
           ffn2_w_down):
    B, S, D = x.shape
    T = B * S
    depth = ffn1_norm.shape[0]
    mem2 = mem.reshape(-1, D)
    for layer in range(depth):
        x = _swiglu_ffn(x.reshape(T, D), _row(ffn1_norm[layer]), ffn1_w_gu[layer].astype(BF16),
                        ffn1_w_down[layer].astype(BF16), scale=0.5).reshape(B, S, D)
        i = layer // 2
        if layer % 2 == 0:
            x = _sb_gla_mixer(x, _row(mix_norm[layer]), ab_w_in[i], sb_q_norm[i], sb_k_norm[i],
                              gla_w_gate[i], gla_b_gate[i], gla_o_norm[i], ab_w_out[i])
        else:
            x = _conv_mixer(x, _row(mix_norm[layer]), conv_w_in[i].astype(BF16),
                            conv_w[i].astype(F32), conv_w_out[i].astype(BF16))
        k_m, v_m = _mem_kv(mem2, _row(mem_norm[layer]), xa_w_kv[layer].astype(BF16),
                           _row(xa_k_norm[layer]))
        xw = k_m.shape[1]
        x = _cross_attention(x, _row(xa_norm[layer]), xa_w_q[layer].astype(BF16),
                             _row(xa_q_norm[layer]), k_m.reshape(B, -1, xw),
                             v_m.reshape(B, -1, xw), xa_w_o[layer].astype(BF16))
        x = _swiglu_ffn(x.reshape(T, D), _row(ffn2_norm[layer]), ffn2_w_gu[layer].astype(BF16),
                        ffn2_w_down[layer].astype(BF16), scale=0.5).reshape(B, S, D)
    return x
```

```python
import functools

import jax
import jax.numpy as jnp
from jax import lax
from jax.experimental import pallas as pl
from jax.experimental.pallas import tpu as pltpu

F32 = jnp.float32
BF16 = jnp.bfloat16

EPS = 1e-6
HEAD_DIM = 128
GLA_DK = 64
GLA_GATE_RANK = 16
GLA_GATE_TAU = 16.0
N_XA_HEADS = 4
LANES = 128
LOG2E = 1.4426950408889634
UNDERFLOW_LOG2 = 160.0
SCORE_BOUND_SLACK = 1.001
VMEM_LIMIT = 56 * 1024 * 1024

_NT = (((1,), (1,)), ((), ()))
_TN = (((0,), (0,)), ((), ()))


def _params(sem):
    return pltpu.CompilerParams(dimension_semantics=sem, vmem_limit_bytes=VMEM_LIMIT)


def _dot(a, b):
    return jnp.dot(a, b, preferred_element_type=F32)


def _dot_nt(a, b):
    return lax.dot_general(a, b, _NT, preferred_element_type=F32)


def _dot_tn(a, b):
    return lax.dot_general(a, b, _TN, preferred_element_type=F32)


def _rms(xf, g):
    ms = jnp.mean(xf * xf, axis=-1, keepdims=True)
    return xf * lax.rsqrt(ms + EPS) * g


def _sigmoid(x):
    return 1.0 / (1.0 + jnp.exp(-x))


def _softplus(x):
    return jnp.maximum(x, 0.0) + jnp.log(1.0 + jnp.exp(-jnp.abs(x)))


def _tile(n, pref):
    t = min(pref, n)
    while n % t:
        t //= 2
    return t


def _glu_kernel(x_ref, g_ref, wa_ref, wb_ref, wd_ref, o_ref, h_ref, *, scale, nf):
    f = pl.program_id(1)

    @pl.when(f == 0)
    def _():
        h_ref[...] = _rms(x_ref[...], g_ref[...]).astype(BF16)
        o_ref[...] = jnp.zeros_like(o_ref)

    h = h_ref[...]
    a = _dot(h, wa_ref[...])
    b = _dot(h, wb_ref[...])
    act = (a * _sigmoid(a) * b).astype(BF16)
    o_ref[...] += _dot(act, wd_ref[...])

    @pl.when(f == nf - 1)
    def _():
        o_ref[...] = x_ref[...] + scale * o_ref[...]


def _swiglu_ffn(x2, g, w_gu, w_d, *, scale):
    T, D = x2.shape
    F = w_d.shape[0]
    tm = _tile(T, 512)
    tf = _tile(F, 1024)
    nf = F // tf
    return pl.pallas_call(
        functools.partial(_glu_kernel, scale=scale, nf=nf),
        grid=(T // tm, nf),
        in_specs=[
            pl.BlockSpec((tm, D), lambda i, f: (i, 0)),
            pl.BlockSpec((1, D), lambda i, f: (0, 0)),
            pl.BlockSpec((D, tf), lambda i, f: (0, f)),
            pl.BlockSpec((D, tf), lambda i, f: (0, f + nf)),
            pl.BlockSpec((tf, D), lambda i, f: (f, 0)),
        ],
        out_specs=pl.BlockSpec((tm, D), lambda i, f: (i, 0)),
        out_shape=jax.ShapeDtypeStruct((T, D), F32),
        scratch_shapes=[pltpu.VMEM((tm, D), BF16)],
        compiler_params=_params(("parallel", "arbitrary")),
        name="swiglu_ffn",
    )(x2, g, w_gu, w_gu, w_d)


def _conv_kernel(x_ref, g_ref, wb_ref, wc_ref, wu_ref, cw_ref, wo_ref, o_ref, h_ref, halo_ref,
                 *, nc):
    i = pl.program_id(1)
    c = pl.program_id(2)

    @pl.when(c == 0)
    def _():
        h_ref[...] = _rms(x_ref[...], g_ref[...]).astype(BF16)
        o_ref[...] = jnp.zeros_like(o_ref)

    @pl.when(i == 0)
    def _():
        halo_ref[c] = jnp.zeros(halo_ref.shape[1:], F32)

    h = h_ref[...]
    bg = _dot(h, wb_ref[...])
    cu = _dot(h, wc_ref[...]) * _dot(h, wu_ref[...])
    tm = cu.shape[0]
    prev = halo_ref[c]
    p1 = prev[7:8, :]
    p2 = prev[6:7, :]
    row = lax.broadcasted_iota(jnp.int32, cu.shape, 0)
    r1 = jnp.where(row == 0, p1, pltpu.roll(cu, 1, 0))
    r2 = jnp.where(row == 0, p2, jnp.where(row == 1, p1, pltpu.roll(cu, 2, 0)))
    cw = cw_ref[...]
    y = cw[0:1, :] * r2 + cw[1:2, :] * r1 + cw[2:3, :] * cu
    halo_ref[c] = cu[tm - 8:, :]
    o_ref[...] += _dot((bg * y).astype(BF16), wo_ref[...])

    @pl.when(c == nc - 1)
    def _():
        o_ref[...] = x_ref[...] + o_ref[...]


def _conv_mixer(x, g, w_in, conv_w, w_out):
    B, S, D = x.shape
    tm = _tile(S, 512)
    tc = _tile(D, 512)
    nc = D // tc
    return pl.pallas_call(
        functools.partial(_conv_kernel, nc=nc),
        grid=(B, S // tm, nc),
        in_specs=[
            pl.BlockSpec((None, tm, D), lambda b, i, c: (b, i, 0)),
            pl.BlockSpec((1, D), lambda b, i, c: (0, 0)),
            pl.BlockSpec((D, tc), lambda b, i, c: (0, c)),
            pl.BlockSpec((D, tc), lambda b, i, c: (0, c + nc)),
            pl.BlockSpec((D, tc), lambda b, i, c: (0, c + 2 * nc)),
            pl.BlockSpec((3, tc), lambda b, i, c: (0, c)),
            pl.BlockSpec((tc, D), lambda b, i, c: (c, 0)),
        ],
        out_specs=pl.BlockSpec((None, tm, D), lambda b, i, c: (b, i, 0)),
        out_shape=jax.ShapeDtypeStruct((B, S, D), F32),
        scratch_shapes=[pltpu.VMEM((tm, D), BF16), pltpu.VMEM((nc, 8, tc), F32)],
        compiler_params=_params(("arbitrary", "arbitrary", "arbitrary")),
        name="conv_mixer",
    )(x, g, w_in, w_in, w_in, conv_w, w_out)


def _proj_sb_kernel(x_ref, g_ref, w_ref, qg_ref, kg_ref, q_ref, k_ref, v_ref, *, nh):
    h = _rms(x_ref[...], g_ref[...]).astype(BF16)
    acc = _dot(h, w_ref[...])
    w = nh * HEAD_DIM
    qscale = LOG2E * HEAD_DIM ** -0.5
    for hd in range(nh):
        lo = hd * HEAD_DIM
        q = acc[:, lo:lo + HEAD_DIM]
        k = acc[:, w + lo:w + lo + HEAD_DIM]
        q_ref[:, lo:lo + HEAD_DIM] = (_rms(q, qg_ref[...]) * qscale).astype(BF16)
        k_ref[:, lo:lo + HEAD_DIM] = _rms(k, kg_ref[...]).astype(BF16)
    v_ref[...] = acc[:, 2 * w:].astype(BF16)


def _proj_gla_kernel(x_ref, g_ref, w_ref, wg_ref, bg_ref, q_ref, k_ref, la_ref, *, kw):
    h = _rms(x_ref[...], g_ref[...]).astype(BF16)
    acc = _dot(h, w_ref[...])
    q_ref[...] = acc[:, :kw] * (GLA_DK ** -0.5)
    k_ref[...] = acc[:, kw:2 * kw]
    logit = _dot(acc[:, 2 * kw:].astype(BF16), wg_ref[...]) + bg_ref[...]
    la_ref[...] = -_softplus(-logit) * (1.0 / GLA_GATE_TAU)


def _proj_plain_kernel(x_ref, g_ref, w_ref, o_ref, *, silu):
    h = _rms(x_ref[...], g_ref[...]).astype(BF16)
    acc = _dot(h, w_ref[...])
    if silu:
        acc = acc * _sigmoid(acc)
    o_ref[...] = acc.astype(BF16)


def _row_spec(tm, n):
    return pl.BlockSpec((tm, n), lambda i: (i, 0))


def _full_spec(shape):
    return pl.BlockSpec(shape, lambda i: (0,) * len(shape))


def _proj_call(kernel, x2, extra_in, out_widths, out_dtypes, name):
    T, D = x2.shape
    tm = _tile(T, 512)
    in_specs = [_row_spec(tm, D)] + [_full_spec(a.shape) for a in extra_in]
    return pl.pallas_call(
        kernel,
        grid=(T // tm,),
        in_specs=in_specs,
        out_specs=[_row_spec(tm, n) for n in out_widths],
        out_shape=[jax.ShapeDtypeStruct((T, n), dt) for n, dt in zip(out_widths, out_dtypes)],
        compiler_params=_params(("parallel",)),
        name=name,
    )(x2, *extra_in)


def _sb_kernel(q_ref, k_ref, v_ref, o_ref, acc_ref, run_ref, kmax_ref, *, tq, tk, group):
    i = pl.program_id(2)
    span = tk * group
    n_keys = k_ref.shape[0]

    @pl.when(i == 0)
    def _():
        def body(n, m):
            kk = k_ref[pl.ds(pl.multiple_of(n * span, span), span), :].astype(F32)
            return jnp.maximum(m, jnp.max(jnp.sum(kk * kk, axis=1, keepdims=True), axis=0, keepdims=True))
        m = lax.fori_loop(0, n_keys // span, body, jnp.zeros((1, 1), F32))
        kmax_ref[...] = jnp.sqrt(m)

    acc_ref[...] = jnp.zeros_like(acc_ref)
    run_ref[...] = jnp.zeros_like(run_ref)
    q = q_ref[...]
    qf = q.astype(F32)
    zmax = jnp.sqrt(jnp.sum(qf * qf, axis=1, keepdims=True)) * kmax_ref[...] * SCORE_BOUND_SLACK
    row = lax.broadcasted_iota(jnp.int32, (tq, tk), 0)
    col = lax.broadcasted_iota(jnp.int32, (tq, tk), 1)
    krow = lax.broadcasted_iota(jnp.int32, (tk, tk), 0)
    kcol = lax.broadcasted_iota(jnp.int32, (tk, tk), 1)
    suffix = (krow >= kcol).astype(BF16)

    def key_span(gi, masked):
        base = pl.multiple_of(gi * span, span)
        run = run_ref[...]
        ws = [None] * group
        for c in reversed(range(group)):
            start = pl.multiple_of(base + c * tk, tk)
            z = _dot_nt(q, k_ref[pl.ds(start, tk), :])
            sp = jnp.maximum(z, 0.0) + jnp.log(1.0 + jnp.exp2(-jnp.abs(z))) * LOG2E
            if masked:
                causal = (col + start) < (row + i * tq)
                sp = jnp.where(causal, sp, 0.0)
            within = _dot(sp.astype(BF16), suffix)
            w = jnp.exp2(z - within - run)
            if masked:
                w = jnp.where(causal, w, 0.0)
            ws[c] = w.astype(BF16)
            run = run + jnp.sum(sp, axis=1, keepdims=True)
        acc_ref[...] += _dot(jnp.concatenate(ws, axis=1), v_ref[pl.ds(base, span), :])
        run_ref[...] = run

    def any_weight_left():
        return jnp.max(zmax - run_ref[...]) > -UNDERFLOW_LOG2

    diag = (i * tq) // span
    key_span(diag, True)

    def cond(carry):
        n, alive = carry
        return jnp.logical_and(n < diag, alive)

    def step(carry):
        n, _ = carry
        key_span(diag - 1 - n, False)
        return n + 1, any_weight_left()

    lax.while_loop(cond, step, (jnp.int32(0), any_weight_left()))
    o_ref[...] = acc_ref[...].astype(BF16)


def _stick_breaking(q, k, v, nh):
    B, S, _ = q.shape
    tq = _tile(S, 256)
    tk = tq
    group = _tile(S // tk, 4)
    return pl.pallas_call(
        functools.partial(_sb_kernel, tq=tq, tk=tk, group=group),
        grid=(B, nh, S // tq),
        in_specs=[
            pl.BlockSpec((None, tq, HEAD_DIM), lambda b, h, i: (b, i, h)),
            pl.BlockSpec((None, S, HEAD_DIM), lambda b, h, i: (b, 0, h)),
            pl.BlockSpec((None, S, HEAD_DIM), lambda b, h, i: (b, 0, h)),
        ],
        out_specs=pl.BlockSpec((None, tq, HEAD_DIM), lambda b, h, i: (b, i, h)),
        out_shape=jax.ShapeDtypeStruct(q.shape, BF16),
        scratch_shapes=[pltpu.VMEM((tq, HEAD_DIM), F32), pltpu.VMEM((tq, 1), F32),
                        pltpu.VMEM((1, 1), F32)],
        compiler_params=_params(("arbitrary", "arbitrary", "arbitrary")),
        name="stick_breaking",
    )(q, k, v)


def _split2(x):
    a = x.astype(BF16)
    return a, (x - a.astype(F32)).astype(BF16)


def _gla_kernel(q_ref, k_ref, la_ref, v_ref, r_ref, gn_ref, o_ref, s_ref, *, chunk, npair):
    n = pl.program_id(1)

    @pl.when(n == 0)
    def _():
        s_ref[...] = jnp.zeros_like(s_ref)

    C = chunk
    q = q_ref[...]
    k = k_ref[...]
    g_hi, g_lo = _split2(la_ref[...])
    row = lax.broadcasted_iota(jnp.int32, (C, C), 0)
    col = lax.broadcasted_iota(jnp.int32, (C, C), 1)
    rcol = lax.broadcasted_iota(jnp.int32, (C, 1), 0)

    def rowsum(mask):
        m = mask.astype(BF16)
        return _dot(m, g_hi) + _dot(m, g_lo)

    b = rowsum(col <= row)
    b_last = b[C - 1:C, :]

    levels = [(q.astype(BF16), k.astype(BF16), row == col)]
    m = 1
    while m < C:
        boundary = (row // (2 * m)) * (2 * m) + (m - 1)
        ref = rowsum(col <= boundary)
        odd = (rcol // m) % 2 == 1
        qs = jnp.where(odd, q * jnp.exp(jnp.where(odd, b - ref, 0.0)), 0.0)
        ks = jnp.where(odd, 0.0, k * jnp.exp(jnp.where(odd, 0.0, ref - b)))
        levels.append((qs.astype(BF16), ks.astype(BF16), (row // (2 * m)) == (col // (2 * m))))
        m *= 2

    q_inter = (q * jnp.exp(b)).astype(BF16)
    k_state = (k * jnp.exp(b_last - b)).astype(BF16)
    lane = lax.broadcasted_iota(jnp.int32, (C, LANES), 1)
    first = lane < GLA_DK
    sq_first = lax.broadcasted_iota(jnp.int32, (HEAD_DIM, LANES), 1) < GLA_DK
    zero = jnp.zeros((), BF16)

    for p in range(npair):
        ksl = slice(p * LANES, (p + 1) * LANES)
        state = s_ref[p]
        state_b = state.astype(BF16)
        kv = []
        for e in range(2):
            hd = 2 * p + e
            vsl = slice(hd * HEAD_DIM, (hd + 1) * HEAD_DIM)
            mine = first if e == 0 else jnp.logical_not(first)
            scores = jnp.zeros((C, C), F32)
            for qs, ks, mask in levels:
                qe = jnp.where(mine, qs[:, ksl], zero)
                scores += jnp.where(mask, _dot_nt(qe, ks[:, ksl]), 0.0)
            v = v_ref[:, vsl]
            o = _dot(scores.astype(BF16), v)
            o += _dot_nt(jnp.where(mine, q_inter[:, ksl], zero), state_b)
            o = _rms(o, gn_ref[...]) * r_ref[:, vsl].astype(F32)
            o_ref[:, vsl] = o.astype(BF16)
            kv.append(_dot_tn(v, k_state[:, ksl]))
        s_ref[p] = jnp.exp(b_last[:, ksl]) * state + jnp.where(sq_first, kv[0], kv[1])


def _gla(q, k, la, v, r, gn):
    B, S, kw = q.shape
    vw = v.shape[-1]
    chunk = _tile(S, 128)
    npair = kw // LANES
    return pl.pallas_call(
        functools.partial(_gla_kernel, chunk=chunk, npair=npair),
        grid=(B, S // chunk),
        in_specs=[
            pl.BlockSpec((None, chunk, kw), lambda b, n: (b, n, 0)),
            pl.BlockSpec((None, chunk, kw), lambda b, n: (b, n, 0)),
            pl.BlockSpec((None, chunk, kw), lambda b, n: (b, n, 0)),
            pl.BlockSpec((None, chunk, vw), lambda b, n: (b, n, 0)),
            pl.BlockSpec((None, chunk, vw), lambda b, n: (b, n, 0)),
            pl.BlockSpec((1, HEAD_DIM), lambda b, n: (0, 0)),
        ],
        out_specs=pl.BlockSpec((None, chunk, vw), lambda b, n: (b, n, 0)),
        out_shape=jax.ShapeDtypeStruct(v.shape, BF16),
        scratch_shapes=[pltpu.VMEM((npair, HEAD_DIM, LANES), F32)],
        compiler_params=_params(("arbitrary", "arbitrary")),
        name="gla",
    )(q, k, la, v, r, gn)


def _outproj_kernel(x_ref, a_ref, b_ref, wa_ref, wb_ref, o_ref):
    o_ref[...] = x_ref[...] + _dot(a_ref[...], wa_ref[...]) + _dot(b_ref[...], wb_ref[...])


def _out_proj(x2, a, b, w_out):
    T, D = x2.shape
    wa, wb = a.shape[1], b.shape[1]
    tm = _tile(T, 512)
    return pl.pallas_call(
        _outproj_kernel,
        grid=(T // tm,),
        in_specs=[
            _row_spec(tm, D), _row_spec(tm, wa), _row_spec(tm, wb),
            pl.BlockSpec((wa, D), lambda i: (0, 0)),
            pl.BlockSpec((wb, D), lambda i: (0, 0)),
        ],
        out_specs=_row_spec(tm, D),
        out_shape=jax.ShapeDtypeStruct((T, D), F32),
        compiler_params=_params(("parallel",)),
        name="mix_out_proj",
    )(x2, a, b, w_out[:wa], w_out[wa:])


def _memkv_kernel(m_ref, g_ref, w_ref, kg_ref, k_ref, v_ref, *, nh):
    h = _rms(m_ref[...], g_ref[...]).astype(BF16)
    acc = _dot(h, w_ref[...])
    w = nh * HEAD_DIM
    for hd in range(nh):
        sl = slice(hd * HEAD_DIM, (hd + 1) * HEAD_DIM)
        k_ref[:, sl] = _rms(acc[:, sl], kg_ref[...]).astype(BF16)
    v_ref[...] = acc[:, w:].astype(BF16)


def _mem_kv(mem2, g, w_kv, kg):
    R, D = mem2.shape
    w = w_kv.shape[1] // 2
    return pl.pallas_call(
        functools.partial(_memkv_kernel, nh=w // HEAD_DIM),
        grid=(1,),
        in_specs=[_full_spec(mem2.shape), _full_spec(g.shape), _full_spec(w_kv.shape),
                  _full_spec(kg.shape)],
        out_specs=[_full_spec((R, w)), _full_spec((R, w))],
        out_shape=[jax.ShapeDtypeStruct((R, w), BF16)] * 2,
        compiler_params=_params(("arbitrary",)),
        name="mem_kv",
    )(mem2, g, w_kv, kg)


def _xa_kernel(x_ref, g_ref, wq_ref, qg_ref, k_ref, v_ref, wo_ref, o_ref, *, nh):
    x = x_ref[...]
    h = _rms(x, g_ref[...]).astype(BF16)
    qf = _dot(h, wq_ref[...])
    scale = HEAD_DIM ** -0.5
    outs = []
    for hd in range(nh):
        sl = slice(hd * HEAD_DIM, (hd + 1) * HEAD_DIM)
        q = (_rms(qf[:, sl], qg_ref[...]) * scale).astype(BF16)
        s = _dot_nt(q, k_ref[:, sl])
        s = s - jnp.max(s, axis=-1, keepdims=True)
        e = jnp.exp(s)
        p = e / jnp.sum(e, axis=-1, keepdims=True)
        outs.append(_dot(p.astype(BF16), v_ref[:, sl]).astype(BF16))
    o = jnp.concatenate(outs, axis=-1)
    o_ref[...] = x + _dot(o, wo_ref[...])


def _cross_attention(x, g, w_q, qg, k, v, w_o):
    B, S, D = x.shape
    M, W = k.shape[1], k.shape[2]
    tm = _tile(S, 512)
    return pl.pallas_call(
        functools.partial(_xa_kernel, nh=W // HEAD_DIM),
        grid=(B, S // tm),
        in_specs=[
            pl.BlockSpec((None, tm, D), lambda b, i: (b, i, 0)),
            pl.BlockSpec((1, D), lambda b, i: (0, 0)),
            pl.BlockSpec((D, W), lambda b, i: (0, 0)),
            pl.BlockSpec((1, HEAD_DIM), lambda b, i: (0, 0)),
            pl.BlockSpec((None, M, W), lambda b, i: (b, 0, 0)),
            pl.BlockSpec((None, M, W), lambda b, i: (b, 0, 0)),
            pl.BlockSpec((W, D), lambda b, i: (0, 0)),
        ],
        out_specs=pl.BlockSpec((None, tm, D), lambda b, i: (b, i, 0)),
        out_shape=jax.ShapeDtypeStruct((B, S, D), F32),
        compiler_params=_params(("parallel", "parallel")),
        name="mem_cross_attention",
    )(x, g, w_q, qg, k, v, w_o)


def _row(v):
    return v.reshape(1, -1).astype(F32)


def _sb_gla_mixer(x, g, w_in, sb_qg, sb_kg, w_gate, b_gate, gla_og, w_out):
    B, S, D = x.shape
    T = B * S
    x2 = x.reshape(T, D)
    n_heads = D // HEAD_DIM
    n_sb = n_heads // 4
    n_gla = n_heads - n_sb
    sbw = n_sb * HEAD_DIM
    kw = n_gla * GLA_DK
    vw = n_gla * HEAD_DIM
    o = 0
    w_sb = w_in[:, o:o + 3 * sbw]; o += 3 * sbw
    w_qk = w_in[:, o:o + 2 * kw]; o += 2 * kw
    w_v = w_in[:, o:o + vw]; o += vw
    w_r = w_in[:, o:o + vw]; o += vw
    w_g = w_in[:, o:o + GLA_GATE_RANK]
    w_qkg = jnp.concatenate([w_qk, w_g, jnp.zeros((D, LANES - GLA_GATE_RANK), w_in.dtype)], axis=1)
    w_gate_p = jnp.concatenate(
        [w_gate, jnp.zeros((LANES - GLA_GATE_RANK, kw), w_gate.dtype)], axis=0).astype(BF16)

    q_sb, k_sb, v_sb = _proj_call(
        functools.partial(_proj_sb_kernel, nh=n_sb), x2,
        [g, w_sb.astype(BF16), _row(sb_qg), _row(sb_kg)],
        [sbw, sbw, sbw], [BF16] * 3, "proj_sb")
    q_g, k_g, log_a = _proj_call(
        functools.partial(_proj_gla_kernel, kw=kw), x2,
        [g, w_qkg.astype(BF16), w_gate_p, _row(b_gate)],
        [kw, kw, kw], [F32] * 3, "proj_gla_qk")
    (v_g,) = _proj_call(functools.partial(_proj_plain_kernel, silu=False), x2,
                        [g, w_v.astype(BF16)], [vw], [BF16], "proj_gla_v")
    (r_g,) = _proj_call(functools.partial(_proj_plain_kernel, silu=True), x2,
                        [g, w_r.astype(BF16)], [vw], [BF16], "proj_gla_r")

    o_sb = _stick_breaking(q_sb.reshape(B, S, sbw), k_sb.reshape(B, S, sbw),
                           v_sb.reshape(B, S, sbw), n_sb)
    o_g = _gla(q_g.reshape(B, S, kw), k_g.reshape(B, S, kw), log_a.reshape(B, S, kw),
               v_g.reshape(B, S, vw), r_g.reshape(B, S, vw), _row(gla_og))
    out = _out_proj(x2, o_sb.reshape(T, sbw), o_g.reshape(T, vw), w_out.astype(BF16))
    return out.reshape(B, S, D)


def kernel(x, mem, ffn1_norm, ffn1_w_gu, ffn1_w_down, mix_norm, ab_w_in, sb_q_norm, sb_k_norm,
           gla_w_gate, gla_b_gate, gla_o_norm, ab_w_out, conv_w_in, conv_w, conv_w_out, xa_norm,
           mem_norm, xa_w_q, xa_w_kv, xa_q_norm, xa_k_norm, xa_w_o, ffn2_norm, ffn2_w_gu,
           ffn2_w_down):
    B, S, D = x.shape
    T = B * S
    depth = ffn1_norm.shape[0]
    mem2 = mem.reshape(-1, D)
    for layer in range(depth):
        x = _swiglu_ffn(x.reshape(T, D), _row(ffn1_norm[layer]), ffn1_w_gu[layer].astype(BF16),
                        ffn1_w_down[layer].astype(BF16), scale=0.5).reshape(B, S, D)
        i = layer // 2
        if layer % 2 == 0:
            x = _sb_gla_mixer(x, _row(mix_norm[layer]), ab_w_in[i], sb_q_norm[i], sb_k_norm[i],
                              gla_w_gate[i], gla_b_gate[i], gla_o_norm[i], ab_w_out[i])
        else:
            x = _conv_mixer(x, _row(mix_norm[layer]), conv_w_in[i].astype(BF16),
                            conv_w[i].astype(F32), conv_w_out[i].astype(BF16))
        k_m, v_m = _mem_kv(mem2, _row(mem_norm[layer]), xa_w_kv[layer].astype(BF16),
                           _row(xa_k_norm[layer]))
        xw = k_m.shape[1]
        x = _cross_attention(x, _row(xa_norm[layer]), xa_w_q[layer].astype(BF16),
                             _row(xa_q_norm[layer]), k_m.reshape(B, -1, xw),
                             v_m.reshape(B, -1, xw), xa_w_o[layer].astype(BF16))
        x = _swiglu_ffn(x.reshape(T, D), _row(ffn2_norm[layer]), ffn2_w_gu[layer].astype(BF16),
                        ffn2_w_down[layer].astype(BF16), scale=0.5).reshape(B, S, D)
    return x
```

```python
import functools

import jax
import jax.numpy as jnp
from jax import lax
from jax.experimental import pallas as pl
from jax.experimental.pallas import tpu as pltpu

F32 = jnp.float32
BF16 = jnp.bfloat16

EPS = 1e-6
HEAD_DIM = 128
GLA_DK = 64
GLA_GATE_RANK = 16
GLA_GATE_TAU = 16.0
LANES = 128
LOG2E = 1.4426950408889634
UNDERFLOW_LOG2 = 160.0
SCORE_BOUND_SLACK = 1.001
DIAG_BLOCK = 8
GLA_MODERATE_GATE = 10.0
VMEM_LIMIT = 56 * 1024 * 1024

_NT = (((1,), (1,)), ((), ()))
_TN = (((0,), (0,)), ((), ()))


def _params(sem):
    return pltpu.CompilerParams(dimension_semantics=sem, vmem_limit_bytes=VMEM_LIMIT)


def _dot(a, b):
    return jnp.dot(a, b, preferred_element_type=F32)


def _dot_nt(a, b):
    return lax.dot_general(a, b, _NT, preferred_element_type=F32)


def _dot_tn(a, b):
    return lax.dot_general(a, b, _TN, preferred_element_type=F32)


def _rms(xf, g):
    ms = jnp.mean(xf * xf, axis=-1, keepdims=True)
    return xf * lax.rsqrt(ms + EPS) * g


def _sigmoid(x):
    return 1.0 / (1.0 + jnp.exp(-x))


def _softplus(x):
    return jnp.maximum(x, 0.0) + jnp.log(1.0 + jnp.exp(-jnp.abs(x)))


def _tile(n, pref):
    t = min(pref, n)
    while n % t:
        t //= 2
    return t


def _glu_kernel(x_ref, g_ref, wa_ref, wb_ref, wd_ref, *rest, scale, nf, next_norm):
    if next_norm:
        g2_ref, o_ref, hn_ref, h_ref = rest
    else:
        o_ref, h_ref = rest
    i = pl.program_id(0)
    f = pl.program_id(1)
    slot = i % 2

    @pl.when(jnp.logical_and(i == 0, f == 0))
    def _():
        h_ref[0] = _rms(x_ref[...], g_ref[...]).astype(BF16)

    def down():
        h = h_ref[slot]
        a = _dot(h, wa_ref[...])
        b = _dot(h, wb_ref[...])
        return _dot((a * _sigmoid(a) * b).astype(BF16), wd_ref[...])

    @pl.when(f == 0)
    def _():
        o_ref[...] = x_ref[...] * (1.0 / scale) + down()

    @pl.when(jnp.logical_and(f > 0, f < nf - 1))
    def _():
        o_ref[...] += down()

    @pl.when(f == nf - 1)
    def _():
        out = scale * (o_ref[...] + down())
        o_ref[...] = out
        if next_norm:
            hn_ref[...] = _rms(out, g2_ref[...]).astype(BF16)
        h_ref[1 - slot] = _rms(x_ref[...], g_ref[...]).astype(BF16)


def _swiglu_ffn(x2, g, w_gu, w_d, *, scale, next_gain=None):
    T, D = x2.shape
    F = w_d.shape[0]
    tm = _tile(T, 512)
    tf = _tile(F, 1024)
    nf = F // tf
    nt = T // tm
    assert nf >= 2
    next_norm = next_gain is not None
    row = pl.BlockSpec((tm, D), lambda i, f: (i, 0))
    vec = pl.BlockSpec((1, D), lambda i, f: (0, 0))
    in_specs = [
        pl.BlockSpec((tm, D), lambda i, f: (jnp.minimum(i + jnp.minimum(f, 1), nt - 1), 0)),
        vec,
        pl.BlockSpec((D, tf), lambda i, f: (0, f)),
        pl.BlockSpec((D, tf), lambda i, f: (0, f + nf)),
        pl.BlockSpec((tf, D), lambda i, f: (f, 0)),
    ]
    args = [x2, g, w_gu, w_gu, w_d]
    out_specs = row
    out_shape = jax.ShapeDtypeStruct((T, D), F32)
    if next_norm:
        in_specs.append(vec)
        args.append(next_gain)
        out_specs = [row, row]
        out_shape = [out_shape, jax.ShapeDtypeStruct((T, D), BF16)]
    return pl.pallas_call(
        functools.partial(_glu_kernel, scale=scale, nf=nf, next_norm=next_norm),
        grid=(nt, nf),
        in_specs=in_specs,
        out_specs=out_specs,
        out_shape=out_shape,
        scratch_shapes=[pltpu.VMEM((2, tm, D), BF16)],
        compiler_params=_params(("arbitrary", "arbitrary")),
        name="swiglu_ffn",
    )(*args)


def _conv_kernel(x_ref, g_ref, wb_ref, wc_ref, wu_ref, cw_ref, wo_ref, o_ref, h_ref, halo_ref,
                 *, nc):
    i = pl.program_id(1)
    c = pl.program_id(2)

    @pl.when(c == 0)
    def _():
        h_ref[...] = _rms(x_ref[...], g_ref[...]).astype(BF16)
        o_ref[...] = jnp.zeros_like(o_ref)

    @pl.when(i == 0)
    def _():
        halo_ref[c] = jnp.zeros(halo_ref.shape[1:], F32)

    h = h_ref[...]
    bg = _dot(h, wb_ref[...])
    cu = _dot(h, wc_ref[...]) * _dot(h, wu_ref[...])
    tm = cu.shape[0]
    prev = halo_ref[c]
    p1 = prev[7:8, :]
    p2 = prev[6:7, :]
    row = lax.broadcasted_iota(jnp.int32, cu.shape, 0)
    r1 = jnp.where(row == 0, p1, pltpu.roll(cu, 1, 0))
    r2 = jnp.where(row == 0, p2, jnp.where(row == 1, p1, pltpu.roll(cu, 2, 0)))
    cw = cw_ref[...]
    y = cw[0:1, :] * r2 + cw[1:2, :] * r1 + cw[2:3, :] * cu
    halo_ref[c] = cu[tm - 8:, :]
    o_ref[...] += _dot((bg * y).astype(BF16), wo_ref[...])

    @pl.when(c == nc - 1)
    def _():
        o_ref[...] = x_ref[...] + o_ref[...]


def _conv_mixer(x, g, w_in, conv_w, w_out):
    B, S, D = x.shape
    tm = _tile(S, 512)
    tc = _tile(D, 512)
    nc = D // tc
    return pl.pallas_call(
        functools.partial(_conv_kernel, nc=nc),
        grid=(B, S // tm, nc),
        in_specs=[
            pl.BlockSpec((None, tm, D), lambda b, i, c: (b, i, 0)),
            pl.BlockSpec((1, D), lambda b, i, c: (0, 0)),
            pl.BlockSpec((D, tc), lambda b, i, c: (0, c)),
            pl.BlockSpec((D, tc), lambda b, i, c: (0, c + nc)),
            pl.BlockSpec((D, tc), lambda b, i, c: (0, c + 2 * nc)),
            pl.BlockSpec((3, tc), lambda b, i, c: (0, c)),
            pl.BlockSpec((tc, D), lambda b, i, c: (c, 0)),
        ],
        out_specs=pl.BlockSpec((None, tm, D), lambda b, i, c: (b, i, 0)),
        out_shape=jax.ShapeDtypeStruct((B, S, D), F32),
        scratch_shapes=[pltpu.VMEM((tm, D), BF16), pltpu.VMEM((nc, 8, tc), F32)],
        compiler_params=_params(("arbitrary", "arbitrary", "arbitrary")),
        name="conv_mixer",
    )(x, g, w_in, w_in, w_in, conv_w, w_out)


def _proj_sb_kernel(h_ref, w_ref, qg_ref, kg_ref, q_ref, k_ref, v_ref, *, nh):
    acc = _dot(h_ref[...], w_ref[...])
    w = nh * HEAD_DIM
    qscale = LOG2E * HEAD_DIM ** -0.5
    for hd in range(nh):
        lo = hd * HEAD_DIM
        q = acc[:, lo:lo + HEAD_DIM]
        k = acc[:, w + lo:w + lo + HEAD_DIM]
        q_ref[:, lo:lo + HEAD_DIM] = (_rms(q, qg_ref[...]) * qscale).astype(BF16)
        k_ref[:, lo:lo + HEAD_DIM] = _rms(k, kg_ref[...]).astype(BF16)
    v_ref[...] = acc[:, 2 * w:].astype(BF16)


def _proj_gla_kernel(h_ref, w_ref, wg_ref, bg_ref, q_ref, k_ref, la_ref, *, kw):
    acc = _dot(h_ref[...], w_ref[...])
    q_ref[...] = acc[:, :kw] * (GLA_DK ** -0.5)
    k_ref[...] = acc[:, kw:2 * kw]
    logit = _dot(acc[:, 2 * kw:].astype(BF16), wg_ref[...]) + bg_ref[...]
    la_ref[...] = -_softplus(-logit) * (1.0 / GLA_GATE_TAU)


def _proj_plain_kernel(h_ref, w_ref, o_ref, *, silu):
    acc = _dot(h_ref[...], w_ref[...])
    if silu:
        acc = acc * _sigmoid(acc)
    o_ref[...] = acc.astype(BF16)


def _row_spec(tm, n):
    return pl.BlockSpec((tm, n), lambda i: (i, 0))


def _full_spec(shape):
    return pl.BlockSpec(shape, lambda i: (0,) * len(shape))


def _proj_call(kernel, h2, extra_in, out_widths, out_dtypes, name):
    T, D = h2.shape
    tm = _tile(T, 512)
    in_specs = [_row_spec(tm, D)] + [_full_spec(a.shape) for a in extra_in]
    return pl.pallas_call(
        kernel,
        grid=(T // tm,),
        in_specs=in_specs,
        out_specs=[_row_spec(tm, n) for n in out_widths],
        out_shape=[jax.ShapeDtypeStruct((T, n), dt) for n, dt in zip(out_widths, out_dtypes)],
        compiler_params=_params(("parallel",)),
        name=name,
    )(h2, *extra_in)


def _sb_kernel(q_ref, k_ref, v_ref, o_ref, acc_ref, run_ref, kmax_ref, *, tq, tk, group):
    i = pl.program_id(2)
    span = tk * group
    n_keys = k_ref.shape[0]

    @pl.when(i == 0)
    def _():
        def body(n, m):
            kk = k_ref[pl.ds(pl.multiple_of(n * span, span), span), :].astype(F32)
            return jnp.maximum(m, jnp.max(jnp.sum(kk * kk, axis=1, keepdims=True), axis=0, keepdims=True))
        m = lax.fori_loop(0, n_keys // span, body, jnp.zeros((1, 1), F32))
        kmax_ref[...] = jnp.sqrt(m)

    acc_ref[...] = jnp.zeros_like(acc_ref)
    run_ref[...] = jnp.zeros_like(run_ref)
    q = q_ref[...]
    qf = q.astype(F32)
    zmax = jnp.sqrt(jnp.sum(qf * qf, axis=1, keepdims=True)) * kmax_ref[...] * SCORE_BOUND_SLACK
    row = lax.broadcasted_iota(jnp.int32, (tq, tk), 0)
    col = lax.broadcasted_iota(jnp.int32, (tq, tk), 1)
    krow = lax.broadcasted_iota(jnp.int32, (tk, tk), 0)
    kcol = lax.broadcasted_iota(jnp.int32, (tk, tk), 1)
    suffix = (krow >= kcol).astype(BF16)

    def key_span(gi, masked):
        base = pl.multiple_of(gi * span, span)
        run = run_ref[...]
        ws = [None] * group
        for c in reversed(range(group)):
            start = pl.multiple_of(base + c * tk, tk)
            z = _dot_nt(q, k_ref[pl.ds(start, tk), :])
            sp = jnp.maximum(z, 0.0) + jnp.log(1.0 + jnp.exp2(-jnp.abs(z))) * LOG2E
            if masked:
                causal = (col + start) < (row + i * tq)
                sp = jnp.where(causal, sp, 0.0)
            within = _dot(sp.astype(BF16), suffix)
            w = jnp.exp2(z - within - run)
            if masked:
                w = jnp.where(causal, w, 0.0)
            ws[c] = w.astype(BF16)
            run = run + jnp.sum(sp, axis=1, keepdims=True)
        acc_ref[...] += _dot(jnp.concatenate(ws, axis=1), v_ref[pl.ds(base, span), :])
        run_ref[...] = run

    def any_weight_left():
        return jnp.max(zmax - run_ref[...]) > -UNDERFLOW_LOG2

    diag = (i * tq) // span
    key_span(diag, True)

    def cond(carry):
        n, alive = carry
        return jnp.logical_and(n < diag, alive)

    def step(carry):
        n, _ = carry
        key_span(diag - 1 - n, False)
        return n + 1, any_weight_left()

    lax.while_loop(cond, step, (jnp.int32(0), any_weight_left()))
    o_ref[...] = acc_ref[...].astype(BF16)


def _stick_breaking(q, k, v, nh):
    B, S, _ = q.shape
    tq = _tile(S, 256)
    tk = tq
    group = _tile(S // tk, 2)
    return pl.pallas_call(
        functools.partial(_sb_kernel, tq=tq, tk=tk, group=group),
        grid=(B, nh, S // tq),
        in_specs=[
            pl.BlockSpec((None, tq, HEAD_DIM), lambda b, h, i: (b, i, h)),
            pl.BlockSpec((None, S, HEAD_DIM), lambda b, h, i: (b, 0, h)),
            pl.BlockSpec((None, S, HEAD_DIM), lambda b, h, i: (b, 0, h)),
        ],
        out_specs=pl.BlockSpec((None, tq, HEAD_DIM), lambda b, h, i: (b, i, h)),
        out_shape=jax.ShapeDtypeStruct(q.shape, BF16),
        scratch_shapes=[pltpu.VMEM((tq, HEAD_DIM), F32), pltpu.VMEM((tq, 1), F32),
                        pltpu.VMEM((1, 1), F32)],
        compiler_params=_params(("arbitrary", "arbitrary", "arbitrary")),
        name="stick_breaking",
    )(q, k, v)


def _split2(x):
    a = x.astype(BF16)
    return a, (x - a.astype(F32)).astype(BF16)


def _gla_kernel(q_ref, k_ref, la_ref, v_ref, r_ref, gn_ref, o_ref, s_ref, *, chunk, npair):
    n = pl.program_id(1)

    @pl.when(n == 0)
    def _():
        s_ref[...] = jnp.zeros_like(s_ref)

    C = chunk
    row = lax.broadcasted_iota(jnp.int32, (C, C), 0)
    col = lax.broadcasted_iota(jnp.int32, (C, C), 1)
    rcol = lax.broadcasted_iota(jnp.int32, (C, 1), 0)
    lane = lax.broadcasted_iota(jnp.int32, (C, LANES), 1)
    first = lane < GLA_DK
    sq_first = lax.broadcasted_iota(jnp.int32, (HEAD_DIM, LANES), 1) < GLA_DK
    zero = jnp.zeros((), BF16)

    def chunk_body(fine):
        q = q_ref[...]
        k = k_ref[...]
        la = la_ref[...]
        g_hi, g_lo = _split2(la)

        def rowsum(mask):
            m = mask.astype(BF16)
            return _dot(m, g_hi) + _dot(m, g_lo)

        def block_row(x, blk, at):
            x3 = x.reshape(C // blk, blk, x.shape[1])
            return jnp.broadcast_to(x3[:, at:at + 1, :], x3.shape).reshape(x.shape)

        b = rowsum(col <= row)
        b_last = b[C - 1:C, :]

        if fine:
            levels = [(q.astype(BF16), k.astype(BF16), row == col)]
            m = 1
        else:
            ref = block_row(b, DIAG_BLOCK, DIAG_BLOCK // 2 - 1)
            levels = [((q * jnp.exp(b - ref)).astype(BF16), (k * jnp.exp(ref - b)).astype(BF16),
                       jnp.logical_and(row // DIAG_BLOCK == col // DIAG_BLOCK, col <= row))]
            m = DIAG_BLOCK
        while m < C:
            if fine:
                ref = rowsum(col <= (row // (2 * m)) * (2 * m) + (m - 1))
            else:
                ref = block_row(b, 2 * m, m - 1)
            odd = (rcol // m) % 2 == 1
            qs = jnp.where(odd, q * jnp.exp(jnp.where(odd, b - ref, 0.0)), 0.0)
            ks = jnp.where(odd, 0.0, k * jnp.exp(jnp.where(odd, 0.0, ref - b)))
            levels.append((qs.astype(BF16), ks.astype(BF16), (row // (2 * m)) == (col // (2 * m))))
            m *= 2

        q_inter = (q * jnp.exp(b)).astype(BF16)
        k_state = (k * jnp.exp(b_last - b)).astype(BF16)

        for p in range(npair):
            ksl = slice(p * LANES, (p + 1) * LANES)
            state = s_ref[p]
            state_b = state.astype(BF16)
            scores = jnp.zeros((2 * C, C), F32)
            for qs, ks, mask in levels:
                qp = qs[:, ksl]
                q2 = jnp.concatenate([jnp.where(first, qp, zero), jnp.where(first, zero, qp)], axis=0)
                mask2 = jnp.concatenate([mask, mask], axis=0)
                scores += jnp.where(mask2, _dot_nt(q2, ks[:, ksl]), 0.0)
            qi = q_inter[:, ksl]
            inter = _dot_nt(jnp.concatenate([jnp.where(first, qi, zero), jnp.where(first, zero, qi)],
                                            axis=0), state_b)
            kv = []
            for e in range(2):
                hd = 2 * p + e
                vsl = slice(hd * HEAD_DIM, (hd + 1) * HEAD_DIM)
                rows = slice(e * C, (e + 1) * C)
                v = v_ref[:, vsl]
                o = _dot(scores[rows].astype(BF16), v) + inter[rows]
                o = _rms(o, gn_ref[...]) * r_ref[:, vsl].astype(F32)
                o_ref[:, vsl] = o.astype(BF16)
                kv.append(_dot_tn(v, k_state[:, ksl]))
            s_ref[p] = jnp.exp(b_last[:, ksl]) * state + jnp.where(sq_first, kv[0], kv[1])

    moderate = jnp.min(la_ref[...]) > -GLA_MODERATE_GATE

    @pl.when(moderate)
    def _():
        chunk_body(fine=False)

    @pl.when(jnp.logical_not(moderate))
    def _():
        chunk_body(fine=True)


def _gla(q, k, la, v, r, gn):
    B, S, kw = q.shape
    vw = v.shape[-1]
    chunk = _tile(S, 128)
    npair = kw // LANES
    return pl.pallas_call(
        functools.partial(_gla_kernel, chunk=chunk, npair=npair),
        grid=(B, S // chunk),
        in_specs=[
            pl.BlockSpec((None, chunk, kw), lambda b, n: (b, n, 0)),
            pl.BlockSpec((None, chunk, kw), lambda b, n: (b, n, 0)),
            pl.BlockSpec((None, chunk, kw), lambda b, n: (b, n, 0)),
            pl.BlockSpec((None, chunk, vw), lambda b, n: (b, n, 0)),
            pl.BlockSpec((None, chunk, vw), lambda b, n: (b, n, 0)),
            pl.BlockSpec((1, HEAD_DIM), lambda b, n: (0, 0)),
        ],
        out_specs=pl.BlockSpec((None, chunk, vw), lambda b, n: (b, n, 0)),
        out_shape=jax.ShapeDtypeStruct(v.shape, BF16),
        scratch_shapes=[pltpu.VMEM((npair, HEAD_DIM, LANES), F32)],
        compiler_params=_params(("arbitrary", "arbitrary")),
        name="gla",
    )(q, k, la, v, r, gn)


def _outproj_kernel(x_ref, a_ref, b_ref, wa_ref, wb_ref, o_ref):
    o_ref[...] = x_ref[...] + _dot(a_ref[...], wa_ref[...]) + _dot(b_ref[...], wb_ref[...])


def _out_proj(x2, a, b, w_out):
    T, D = x2.shape
    wa, wb = a.shape[1], b.shape[1]
    tm = _tile(T, 512)
    return pl.pallas_call(
        _outproj_kernel,
        grid=(T // tm,),
        in_specs=[
            _row_spec(tm, D), _row_spec(tm, wa), _row_spec(tm, wb),
            pl.BlockSpec((wa, D), lambda i: (0, 0)),
            pl.BlockSpec((wb, D), lambda i: (0, 0)),
        ],
        out_specs=_row_spec(tm, D),
        out_shape=jax.ShapeDtypeStruct((T, D), F32),
        compiler_params=_params(("parallel",)),
        name="mix_out_proj",
    )(x2, a, b, w_out[:wa], w_out[wa:])


def _memkv_kernel(m_ref, g_ref, w_ref, kg_ref, k_ref, v_ref, *, nh):
    h = _rms(m_ref[...], g_ref[...]).astype(BF16)
    acc = _dot(h, w_ref[...])
    w = nh * HEAD_DIM
    for hd in range(nh):
        sl = slice(hd * HEAD_DIM, (hd + 1) * HEAD_DIM)
        k_ref[:, sl] = _rms(acc[:, sl], kg_ref[...]).astype(BF16)
    v_ref[...] = acc[:, w:].astype(BF16)


def _mem_kv(mem2, g, w_kv, kg):
    R, D = mem2.shape
    w = w_kv.shape[1] // 2
    return pl.pallas_call(
        functools.partial(_memkv_kernel, nh=w // HEAD_DIM),
        grid=(1,),
        in_specs=[_full_spec(mem2.shape), _full_spec(g.shape), _full_spec(w_kv.shape),
                  _full_spec(kg.shape)],
        out_specs=[_full_spec((R, w)), _full_spec((R, w))],
        out_shape=[jax.ShapeDtypeStruct((R, w), BF16)] * 2,
        compiler_params=_params(("arbitrary",)),
        name="mem_kv",
    )(mem2, g, w_kv, kg)


def _xa_kernel(x_ref, g_ref, wq_ref, qg_ref, k_ref, v_ref, wo_ref, o_ref, *, nh):
    x = x_ref[...]
    h = _rms(x, g_ref[...]).astype(BF16)
    qf = _dot(h, wq_ref[...])
    scale = HEAD_DIM ** -0.5
    outs = []
    for hd in range(nh):
        sl = slice(hd * HEAD_DIM, (hd + 1) * HEAD_DIM)
        q = (_rms(qf[:, sl], qg_ref[...]) * scale).astype(BF16)
        s = _dot_nt(q, k_ref[:, sl])
        s = s - jnp.max(s, axis=-1, keepdims=True)
        e = jnp.exp(s)
        p = e / jnp.sum(e, axis=-1, keepdims=True)
        outs.append(_dot(p.astype(BF16), v_ref[:, sl]).astype(BF16))
    o = jnp.concatenate(outs, axis=-1)
    o_ref[...] = x + _dot(o, wo_ref[...])


def _cross_attention(x, g, w_q, qg, k, v, w_o):
    B, S, D = x.shape
    M, W = k.shape[1], k.shape[2]
    tm = _tile(S, 512)
    return pl.pallas_call(
        functools.partial(_xa_kernel, nh=W // HEAD_DIM),
        grid=(B, S // tm),
        in_specs=[
            pl.BlockSpec((None, tm, D), lambda b, i: (b, i, 0)),
            pl.BlockSpec((1, D), lambda b, i: (0, 0)),
            pl.BlockSpec((D, W), lambda b, i: (0, 0)),
            pl.BlockSpec((1, HEAD_DIM), lambda b, i: (0, 0)),
            pl.BlockSpec((None, M, W), lambda b, i: (b, 0, 0)),
            pl.BlockSpec((None, M, W), lambda b, i: (b, 0, 0)),
            pl.BlockSpec((W, D), lambda b, i: (0, 0)),
        ],
        out_specs=pl.BlockSpec((None, tm, D), lambda b, i: (b, i, 0)),
        out_shape=jax.ShapeDtypeStruct((B, S, D), F32),
        compiler_params=_params(("parallel", "parallel")),
        name="mem_cross_attention",
    )(x, g, w_q, qg, k, v, w_o)


def _row(v):
    return v.reshape(1, -1).astype(F32)


def _sb_gla_mixer(x, h2, w_in, sb_qg, sb_kg, w_gate, b_gate, gla_og, w_out):
    B, S, D = x.shape
    T = B * S
    x2 = x.reshape(T, D)
    n_heads = D // HEAD_DIM
    n_sb = n_heads // 4
    n_gla = n_heads - n_sb
    sbw = n_sb * HEAD_DIM
    kw = n_gla * GLA_DK
    vw = n_gla * HEAD_DIM
    o = 0
    w_sb = w_in[:, o:o + 3 * sbw]; o += 3 * sbw
    w_qk = w_in[:, o:o + 2 * kw]; o += 2 * kw
    w_v = w_in[:, o:o + vw]; o += vw
    w_r = w_in[:, o:o + vw]; o += vw
    w_g = w_in[:, o:o + GLA_GATE_RANK]
    w_qkg = jnp.concatenate([w_qk, w_g, jnp.zeros((D, LANES - GLA_GATE_RANK), w_in.dtype)], axis=1)
    w_gate_p = jnp.concatenate(
        [w_gate, jnp.zeros((LANES - GLA_GATE_RANK, kw), w_gate.dtype)], axis=0).astype(BF16)

    q_sb, k_sb, v_sb = _proj_call(
        functools.partial(_proj_sb_kernel, nh=n_sb), h2,
        [w_sb.astype(BF16), _row(sb_qg), _row(sb_kg)],
        [sbw, sbw, sbw], [BF16] * 3, "proj_sb")
    q_g, k_g, log_a = _proj_call(
        functools.partial(_proj_gla_kernel, kw=kw), h2,
        [w_qkg.astype(BF16), w_gate_p, _row(b_gate)],
        [kw, kw, kw], [F32] * 3, "proj_gla_qk")
    (v_g,) = _proj_call(functools.partial(_proj_plain_kernel, silu=False), h2,
                        [w_v.astype(BF16)], [vw], [BF16], "proj_gla_v")
    (r_g,) = _proj_call(functools.partial(_proj_plain_kernel, silu=True), h2,
                        [w_r.astype(BF16)], [vw], [BF16], "proj_gla_r")

    o_sb = _stick_breaking(q_sb.reshape(B, S, sbw), k_sb.reshape(B, S, sbw),
                           v_sb.reshape(B, S, sbw), n_sb)
    o_g = _gla(q_g.reshape(B, S, kw), k_g.reshape(B, S, kw), log_a.reshape(B, S, kw),
               v_g.reshape(B, S, vw), r_g.reshape(B, S, vw), _row(gla_og))
    out = _out_proj(x2, o_sb.reshape(T, sbw), o_g.reshape(T, vw), w_out.astype(BF16))
    return out.reshape(B, S, D)


def kernel(x, mem, ffn1_norm, ffn1_w_gu, ffn1_w_down, mix_norm, ab_w_in, sb_q_norm, sb_k_norm,
           gla_w_gate, gla_b_gate, gla_o_norm, ab_w_out, conv_w_in, conv_w, conv_w_out, xa_norm,
           mem_norm, xa_w_q, xa_w_kv, xa_q_norm, xa_k_norm, xa_w_o, ffn2_norm, ffn2_w_gu,
           ffn2_w_down):
    B, S, D = x.shape
    T = B * S
    depth = ffn1_norm.shape[0]
    mem2 = mem.reshape(-1, D)
    for layer in range(depth):
        i = layer // 2
        even = layer % 2 == 0
        ffn1 = _swiglu_ffn(x.reshape(T, D), _row(ffn1_norm[layer]), ffn1_w_gu[layer].astype(BF16),
                           ffn1_w_down[layer].astype(BF16), scale=0.5,
                           next_gain=_row(mix_norm[layer]) if even else None)
        if even:
            x = _sb_gla_mixer(ffn1[0].reshape(B, S, D), ffn1[1], ab_w_in[i], sb_q_norm[i],
                              sb_k_norm[i], gla_w_gate[i], gla_b_gate[i], gla_o_norm[i], ab_w_out[i])
        else:
            x = _conv_mixer(ffn1.reshape(B, S, D), _row(mix_norm[layer]), conv_w_in[i].astype(BF16),
                            conv_w[i].astype(F32), conv_w_out[i].astype(BF16))
        k_m, v_m = _mem_kv(mem2, _row(mem_norm[layer]), xa_w_kv[layer].astype(BF16),
                           _row(xa_k_norm[layer]))
        xw = k_m.shape[1]
        x = _cross_attention(x, _row(xa_norm[layer]), xa_w_q[layer].astype(BF16),
                             _row(xa_q_norm[layer]), k_m.reshape(B, -1, xw),
                             v_m.reshape(B, -1, xw), xa_w_o[layer].astype(BF16))
        x = _swiglu_ffn(x.reshape(T, D), _row(ffn2_norm[layer]), ffn2_w_gu[layer].astype(BF16),
                        ffn2_w_down[layer].astype(BF16), scale=0.5).reshape(B, S, D)
    return x
```

```python
import functools

import jax
import jax.numpy as jnp
from jax import lax
from jax.experimental import pallas as pl
from jax.experimental.pallas import tpu as pltpu

F32 = jnp.float32
BF16 = jnp.bfloat16

EPS = 1e-6
HEAD_DIM = 128
GLA_DK = 64
GLA_GATE_RANK = 16
GLA_GATE_TAU = 16.0
LANES = 128
LOG2E = 1.4426950408889634
UNDERFLOW_LOG2 = 160.0
SCORE_BOUND_SLACK = 1.001
CONV_SPLIT = 2
DIAG_BLOCK = 8
GLA_MODERATE_GATE = 10.0
VMEM_LIMIT = 56 * 1024 * 1024

_NT = (((1,), (1,)), ((), ()))
_TN = (((0,), (0,)), ((), ()))


def _params(sem):
    return pltpu.CompilerParams(dimension_semantics=sem, vmem_limit_bytes=VMEM_LIMIT)


def _dot(a, b):
    return jnp.dot(a, b, preferred_element_type=F32)


def _dot_nt(a, b):
    return lax.dot_general(a, b, _NT, preferred_element_type=F32)


def _dot_tn(a, b):
    return lax.dot_general(a, b, _TN, preferred_element_type=F32)


def _rms(xf, g):
    ms = jnp.mean(xf * xf, axis=-1, keepdims=True)
    return xf * lax.rsqrt(ms + EPS) * g


def _sigmoid(x):
    return 1.0 / (1.0 + jnp.exp(-x))


def _softplus(x):
    return jnp.maximum(x, 0.0) + jnp.log(1.0 + jnp.exp(-jnp.abs(x)))


def _tile(n, pref):
    t = min(pref, n)
    while n % t:
        t //= 2
    return t


def _glu_kernel(x_ref, g_ref, wa_ref, wb_ref, wd_ref, *rest, scale, nf, next_norm):
    if next_norm:
        g2_ref, o_ref, hn_ref, h_ref = rest
    else:
        o_ref, h_ref = rest
    i = pl.program_id(0)
    f = pl.program_id(1)
    slot = i % 2

    @pl.when(jnp.logical_and(i == 0, f == 0))
    def _():
        h_ref[0] = _rms(x_ref[...], g_ref[...]).astype(BF16)

    def down():
        h = h_ref[slot]
        a = _dot(h, wa_ref[...])
        b = _dot(h, wb_ref[...])
        return _dot((a * _sigmoid(a) * b).astype(BF16), wd_ref[...])

    @pl.when(f == 0)
    def _():
        o_ref[...] = x_ref[...] * (1.0 / scale) + down()

    @pl.when(jnp.logical_and(f > 0, f < nf - 1))
    def _():
        o_ref[...] += down()

    @pl.when(f == nf - 1)
    def _():
        out = scale * (o_ref[...] + down())
        o_ref[...] = out
        if next_norm:
            hn_ref[...] = _rms(out, g2_ref[...]).astype(BF16)
        h_ref[1 - slot] = _rms(x_ref[...], g_ref[...]).astype(BF16)


def _swiglu_ffn(x2, g, w_gu, w_d, *, scale, next_gain=None):
    T, D = x2.shape
    F = w_d.shape[0]
    tm = _tile(T, 512)
    tf = _tile(F, 1024)
    nf = F // tf
    nt = T // tm
    assert nf >= 2
    next_norm = next_gain is not None
    row = pl.BlockSpec((tm, D), lambda i, f: (i, 0))
    vec = pl.BlockSpec((1, D), lambda i, f: (0, 0))
    in_specs = [
        pl.BlockSpec((tm, D), lambda i, f: (jnp.minimum(i + jnp.minimum(f, 1), nt - 1), 0)),
        vec,
        pl.BlockSpec((D, tf), lambda i, f: (0, f)),
        pl.BlockSpec((D, tf), lambda i, f: (0, f + nf)),
        pl.BlockSpec((tf, D), lambda i, f: (f, 0)),
    ]
    args = [x2, g, w_gu, w_gu, w_d]
    out_specs = row
    out_shape = jax.ShapeDtypeStruct((T, D), F32)
    if next_norm:
        in_specs.append(vec)
        args.append(next_gain)
        out_specs = [row, row]
        out_shape = [out_shape, jax.ShapeDtypeStruct((T, D), BF16)]
    return pl.pallas_call(
        functools.partial(_glu_kernel, scale=scale, nf=nf, next_norm=next_norm),
        grid=(nt, nf),
        in_specs=in_specs,
        out_specs=out_specs,
        out_shape=out_shape,
        scratch_shapes=[pltpu.VMEM((2, tm, D), BF16)],
        compiler_params=_params(("arbitrary", "arbitrary")),
        name="swiglu_ffn",
    )(*args)


def _conv_kernel(x_ref, g_ref, wb_ref, wc_ref, wu_ref, cw_ref, wo_ref, o_ref, h_ref, halo_ref,
                 *, nc):
    i = pl.program_id(1)
    c = pl.program_id(2)
    t = pl.program_id(0) * pl.num_programs(1) + i
    slot = t % 2

    @pl.when(jnp.logical_and(t == 0, c == 0))
    def _():
        h_ref[0] = _rms(x_ref[...], g_ref[...]).astype(BF16)

    @pl.when(i == 0)
    def _():
        halo_ref[c] = jnp.zeros(halo_ref.shape[1:], F32)

    def mix():
        h = h_ref[slot]
        tm, tc = h.shape[0], wb_ref.shape[1]
        prev = halo_ref[c]
        row = lax.broadcasted_iota(jnp.int32, (tm, tc // CONV_SPLIT), 0)
        acc = None
        for s in range(CONV_SPLIT):
            sl = slice(s * (tc // CONV_SPLIT), (s + 1) * (tc // CONV_SPLIT))
            bg = _dot(h, wb_ref[:, sl])
            cu = _dot(h, wc_ref[:, sl]) * _dot(h, wu_ref[:, sl])
            p1 = prev[7:8, sl]
            p2 = prev[6:7, sl]
            r1 = jnp.where(row == 0, p1, pltpu.roll(cu, 1, 0))
            r2 = jnp.where(row == 0, p2, jnp.where(row == 1, p1, pltpu.roll(cu, 2, 0)))
            y = cw_ref[0:1, sl] * r2 + cw_ref[1:2, sl] * r1 + cw_ref[2:3, sl] * cu
            halo_ref[c, :, sl] = cu[tm - 8:, :]
            d = _dot((bg * y).astype(BF16), wo_ref[sl, :])
            acc = d if acc is None else acc + d
        return acc

    @pl.when(c == 0)
    def _():
        o_ref[...] = x_ref[...] + mix()

    @pl.when(jnp.logical_and(c > 0, c < nc - 1))
    def _():
        o_ref[...] += mix()

    @pl.when(c == nc - 1)
    def _():
        o_ref[...] += mix()
        h_ref[1 - slot] = _rms(x_ref[...], g_ref[...]).astype(BF16)


def _conv_mixer(x, g, w_in, conv_w, w_out):
    B, S, D = x.shape
    tm = _tile(S, 512)
    tc = _tile(D, 512)
    nc = D // tc
    ni = S // tm
    assert nc >= 2

    def x_map(b, i, c):
        t = jnp.minimum(b * ni + i + jnp.minimum(c, 1), B * ni - 1)
        return (t // ni, t % ni, 0)

    return pl.pallas_call(
        functools.partial(_conv_kernel, nc=nc),
        grid=(B, ni, nc),
        in_specs=[
            pl.BlockSpec((None, tm, D), x_map),
            pl.BlockSpec((1, D), lambda b, i, c: (0, 0)),
            pl.BlockSpec((D, tc), lambda b, i, c: (0, c)),
            pl.BlockSpec((D, tc), lambda b, i, c: (0, c + nc)),
            pl.BlockSpec((D, tc), lambda b, i, c: (0, c + 2 * nc)),
            pl.BlockSpec((3, tc), lambda b, i, c: (0, c)),
            pl.BlockSpec((tc, D), lambda b, i, c: (c, 0)),
        ],
        out_specs=pl.BlockSpec((None, tm, D), lambda b, i, c: (b, i, 0)),
        out_shape=jax.ShapeDtypeStruct((B, S, D), F32),
        scratch_shapes=[pltpu.VMEM((2, tm, D), BF16), pltpu.VMEM((nc, 8, tc), F32)],
        compiler_params=_params(("arbitrary", "arbitrary", "arbitrary")),
        name="conv_mixer",
    )(x, g, w_in, w_in, w_in, conv_w, w_out)


def _proj_sb_kernel(h_ref, w_ref, qg_ref, kg_ref, q_ref, k_ref, v_ref, *, nh):
    acc = _dot(h_ref[...], w_ref[...])
    w = nh * HEAD_DIM
    qscale = LOG2E * HEAD_DIM ** -0.5
    for hd in range(nh):
        lo = hd * HEAD_DIM
        q = acc[:, lo:lo + HEAD_DIM]
        k = acc[:, w + lo:w + lo + HEAD_DIM]
        q_ref[:, lo:lo + HEAD_DIM] = (_rms(q, qg_ref[...]) * qscale).astype(BF16)
        k_ref[:, lo:lo + HEAD_DIM] = _rms(k, kg_ref[...]).astype(BF16)
    v_ref[...] = acc[:, 2 * w:].astype(BF16)


def _proj_gla_kernel(h_ref, w_ref, wg_ref, bg_ref, q_ref, k_ref, la_ref, *, kw):
    acc = _dot(h_ref[...], w_ref[...])
    q_ref[...] = acc[:, :kw] * (GLA_DK ** -0.5)
    k_ref[...] = acc[:, kw:2 * kw]
    logit = _dot(acc[:, 2 * kw:].astype(BF16), wg_ref[...]) + bg_ref[...]
    la_ref[...] = -_softplus(-logit) * (1.0 / GLA_GATE_TAU)


def _proj_plain_kernel(h_ref, w_ref, o_ref, *, silu):
    acc = _dot(h_ref[...], w_ref[...])
    if silu:
        acc = acc * _sigmoid(acc)
    o_ref[...] = acc.astype(BF16)


def _row_spec(tm, n):
    return pl.BlockSpec((tm, n), lambda i: (i, 0))


def _full_spec(shape):
    return pl.BlockSpec(shape, lambda i: (0,) * len(shape))


def _proj_call(kernel, h2, extra_in, out_widths, out_dtypes, name):
    T, D = h2.shape
    tm = _tile(T, 512)
    in_specs = [_row_spec(tm, D)] + [_full_spec(a.shape) for a in extra_in]
    return pl.pallas_call(
        kernel,
        grid=(T // tm,),
        in_specs=in_specs,
        out_specs=[_row_spec(tm, n) for n in out_widths],
        out_shape=[jax.ShapeDtypeStruct((T, n), dt) for n, dt in zip(out_widths, out_dtypes)],
        compiler_params=_params(("parallel",)),
        name=name,
    )(h2, *extra_in)


def _sb_kernel(q_ref, k_ref, v_ref, o_ref, acc_ref, run_ref, kmax_ref, *, tq, tk, group):
    i = pl.program_id(2)
    span = tk * group
    n_keys = k_ref.shape[0]

    @pl.when(i == 0)
    def _():
        def body(n, m):
            kk = k_ref[pl.ds(pl.multiple_of(n * span, span), span), :].astype(F32)
            return jnp.maximum(m, jnp.max(jnp.sum(kk * kk, axis=1, keepdims=True), axis=0, keepdims=True))
        m = lax.fori_loop(0, n_keys // span, body, jnp.zeros((1, 1), F32))
        kmax_ref[...] = jnp.sqrt(m)

    acc_ref[...] = jnp.zeros_like(acc_ref)
    run_ref[...] = jnp.zeros_like(run_ref)
    q = q_ref[...]
    qf = q.astype(F32)
    zmax = jnp.sqrt(jnp.sum(qf * qf, axis=1, keepdims=True)) * kmax_ref[...] * SCORE_BOUND_SLACK
    row = lax.broadcasted_iota(jnp.int32, (tq, tk), 0)
    col = lax.broadcasted_iota(jnp.int32, (tq, tk), 1)
    krow = lax.broadcasted_iota(jnp.int32, (tk, tk), 0)
    kcol = lax.broadcasted_iota(jnp.int32, (tk, tk), 1)
    suffix = (krow >= kcol).astype(BF16)

    def key_tiles(first_tile, ntiles, masked):
        base = pl.multiple_of(first_tile * tk, tk)
        run = run_ref[...]
        ws = [None] * ntiles
        for c in reversed(range(ntiles)):
            start = pl.multiple_of(base + c * tk, tk)
            z = _dot_nt(q, k_ref[pl.ds(start, tk), :])
            sp = jnp.maximum(z, 0.0) + jnp.log(1.0 + jnp.exp2(-jnp.abs(z))) * LOG2E
            if masked:
                causal = (col + start) < (row + i * tq)
                sp = jnp.where(causal, sp, 0.0)
            within = _dot(sp.astype(BF16), suffix)
            w = jnp.exp2(z - within - run)
            if masked:
                w = jnp.where(causal, w, 0.0)
            ws[c] = w.astype(BF16)
            run = run + jnp.sum(sp, axis=1, keepdims=True)
        acc_ref[...] += _dot(jnp.concatenate(ws, axis=1), v_ref[pl.ds(base, ntiles * tk), :])
        run_ref[...] = run

    def any_weight_left():
        return jnp.max(zmax - run_ref[...]) > -UNDERFLOW_LOG2

    before = jnp.maximum(i - (group - 1), 0)
    key_tiles(before, group, True)

    def sweep(count, first_of, ntiles, alive):
        def cond(carry):
            n, live = carry
            return jnp.logical_and(n < count, live)

        def step(carry):
            n, _ = carry
            key_tiles(first_of(n), ntiles, False)
            return n + 1, any_weight_left()

        return lax.while_loop(cond, step, (jnp.int32(0), alive))[1]

    alive = sweep(before // group, lambda n: before - group * (n + 1), group, any_weight_left())
    sweep(before % group, lambda n: before % group - 1 - n, 1, alive)
    o_ref[...] = acc_ref[...].astype(BF16)


def _stick_breaking(q, k, v, nh):
    B, S, _ = q.shape
    tq = _tile(S, 256)
    tk = tq
    group = _tile(S // tk, 2)
    return pl.pallas_call(
        functools.partial(_sb_kernel, tq=tq, tk=tk, group=group),
        grid=(B, nh, S // tq),
        in_specs=[
            pl.BlockSpec((None, tq, HEAD_DIM), lambda b, h, i: (b, i, h)),
            pl.BlockSpec((None, S, HEAD_DIM), lambda b, h, i: (b, 0, h)),
            pl.BlockSpec((None, S, HEAD_DIM), lambda b, h, i: (b, 0, h)),
        ],
        out_specs=pl.BlockSpec((None, tq, HEAD_DIM), lambda b, h, i: (b, i, h)),
        out_shape=jax.ShapeDtypeStruct(q.shape, BF16),
        scratch_shapes=[pltpu.VMEM((tq, HEAD_DIM), F32), pltpu.VMEM((tq, 1), F32),
                        pltpu.VMEM((1, 1), F32)],
        compiler_params=_params(("arbitrary", "arbitrary", "arbitrary")),
        name="stick_breaking",
    )(q, k, v)


def _split2(x):
    a = x.astype(BF16)
    return a, (x - a.astype(F32)).astype(BF16)


def _gla_kernel(q_ref, k_ref, la_ref, v_ref, r_ref, gn_ref, o_ref, s_ref, *, chunk, npair):
    n = pl.program_id(1)

    @pl.when(n == 0)
    def _():
        s_ref[...] = jnp.zeros_like(s_ref)

    C = chunk
    row = lax.broadcasted_iota(jnp.int32, (C, C), 0)
    col = lax.broadcasted_iota(jnp.int32, (C, C), 1)
    rcol = lax.broadcasted_iota(jnp.int32, (C, 1), 0)
    lane = lax.broadcasted_iota(jnp.int32, (C, LANES), 1)
    first = lane < GLA_DK
    sq_first = lax.broadcasted_iota(jnp.int32, (HEAD_DIM, LANES), 1) < GLA_DK
    zero = jnp.zeros((), BF16)

    def chunk_body(fine):
        q = q_ref[...]
        k = k_ref[...]
        la = la_ref[...]
        g_hi, g_lo = _split2(la)

        def rowsum(mask):
            m = mask.astype(BF16)
            return _dot(m, g_hi) + _dot(m, g_lo)

        def block_row(x, blk, at):
            x3 = x.reshape(C // blk, blk, x.shape[1])
            return jnp.broadcast_to(x3[:, at:at + 1, :], x3.shape).reshape(x.shape)

        b = rowsum(col <= row)
        b_last = b[C - 1:C, :]

        if fine:
            levels = [(q.astype(BF16), k.astype(BF16), row == col)]
            m = 1
        else:
            ref = block_row(b, DIAG_BLOCK, DIAG_BLOCK // 2 - 1)
            levels = [((q * jnp.exp(b - ref)).astype(BF16), (k * jnp.exp(ref - b)).astype(BF16),
                       jnp.logical_and(row // DIAG_BLOCK == col // DIAG_BLOCK, col <= row))]
            m = DIAG_BLOCK
        while m < C:
            if fine:
                ref = rowsum(col <= (row // (2 * m)) * (2 * m) + (m - 1))
            else:
                ref = block_row(b, 2 * m, m - 1)
            odd = (rcol // m) % 2 == 1
            qs = jnp.where(odd, q * jnp.exp(jnp.where(odd, b - ref, 0.0)), 0.0)
            ks = jnp.where(odd, 0.0, k * jnp.exp(jnp.where(odd, 0.0, ref - b)))
            levels.append((qs.astype(BF16), ks.astype(BF16), (row // (2 * m)) == (col // (2 * m))))
            m *= 2

        q_inter = (q * jnp.exp(b)).astype(BF16)
        k_state = (k * jnp.exp(b_last - b)).astype(BF16)

        for p in range(npair):
            ksl = slice(p * LANES, (p + 1) * LANES)
            state = s_ref[p]
            state_b = state.astype(BF16)
            scores = jnp.zeros((2 * C, C), F32)
            for qs, ks, mask in levels:
                qp = qs[:, ksl]
                q2 = jnp.concatenate([jnp.where(first, qp, zero), jnp.where(first, zero, qp)], axis=0)
                mask2 = jnp.concatenate([mask, mask], axis=0)
                scores += jnp.where(mask2, _dot_nt(q2, ks[:, ksl]), 0.0)
            qi = q_inter[:, ksl]
            inter = _dot_nt(jnp.concatenate([jnp.where(first, qi, zero), jnp.where(first, zero, qi)],
                                            axis=0), state_b)
            kv = []
            for e in range(2):
                hd = 2 * p + e
                vsl = slice(hd * HEAD_DIM, (hd + 1) * HEAD_DIM)
                rows = slice(e * C, (e + 1) * C)
                v = v_ref[:, vsl]
                o = _dot(scores[rows].astype(BF16), v) + inter[rows]
                o = _rms(o, gn_ref[...]) * r_ref[:, vsl].astype(F32)
                o_ref[:, vsl] = o.astype(BF16)
                kv.append(_dot_tn(v, k_state[:, ksl]))
            s_ref[p] = jnp.exp(b_last[:, ksl]) * state + jnp.where(sq_first, kv[0], kv[1])

    moderate = jnp.min(la_ref[...]) > -GLA_MODERATE_GATE

    @pl.when(moderate)
    def _():
        chunk_body(fine=False)

    @pl.when(jnp.logical_not(moderate))
    def _():
        chunk_body(fine=True)


def _gla(q, k, la, v, r, gn):
    B, S, kw = q.shape
    vw = v.shape[-1]
    chunk = _tile(S, 128)
    npair = kw // LANES
    return pl.pallas_call(
        functools.partial(_gla_kernel, chunk=chunk, npair=npair),
        grid=(B, S // chunk),
        in_specs=[
            pl.BlockSpec((None, chunk, kw), lambda b, n: (b, n, 0)),
            pl.BlockSpec((None, chunk, kw), lambda b, n: (b, n, 0)),
            pl.BlockSpec((None, chunk, kw), lambda b, n: (b, n, 0)),
            pl.BlockSpec((None, chunk, vw), lambda b, n: (b, n, 0)),
            pl.BlockSpec((None, chunk, vw), lambda b, n: (b, n, 0)),
            pl.BlockSpec((1, HEAD_DIM), lambda b, n: (0, 0)),
        ],
        out_specs=pl.BlockSpec((None, chunk, vw), lambda b, n: (b, n, 0)),
        out_shape=jax.ShapeDtypeStruct(v.shape, BF16),
        scratch_shapes=[pltpu.VMEM((npair, HEAD_DIM, LANES), F32)],
        compiler_params=_params(("arbitrary", "arbitrary")),
        name="gla",
    )(q, k, la, v, r, gn)


def _outproj_kernel(x_ref, a_ref, b_ref, wa_ref, wb_ref, o_ref):
    o_ref[...] = x_ref[...] + _dot(a_ref[...], wa_ref[...]) + _dot(b_ref[...], wb_ref[...])


def _out_proj(x2, a, b, w_out):
    T, D = x2.shape
    wa, wb = a.shape[1], b.shape[1]
    tm = _tile(T, 512)
    return pl.pallas_call(
        _outproj_kernel,
        grid=(T // tm,),
        in_specs=[
            _row_spec(tm, D), _row_spec(tm, wa), _row_spec(tm, wb),
            pl.BlockSpec((wa, D), lambda i: (0, 0)),
            pl.BlockSpec((wb, D), lambda i: (0, 0)),
        ],
        out_specs=_row_spec(tm, D),
        out_shape=jax.ShapeDtypeStruct((T, D), F32),
        compiler_params=_params(("parallel",)),
        name="mix_out_proj",
    )(x2, a, b, w_out[:wa], w_out[wa:])


def _memkv_kernel(m_ref, g_ref, w_ref, kg_ref, k_ref, v_ref, *, nh):
    h = _rms(m_ref[...], g_ref[...]).astype(BF16)
    acc = _dot(h, w_ref[...])
    w = nh * HEAD_DIM
    for hd in range(nh):
        sl = slice(hd * HEAD_DIM, (hd + 1) * HEAD_DIM)
        k_ref[:, sl] = _rms(acc[:, sl], kg_ref[...]).astype(BF16)
    v_ref[...] = acc[:, w:].astype(BF16)


def _mem_kv(mem2, g, w_kv, kg):
    R, D = mem2.shape
    w = w_kv.shape[1] // 2
    return pl.pallas_call(
        functools.partial(_memkv_kernel, nh=w // HEAD_DIM),
        grid=(1,),
        in_specs=[_full_spec(mem2.shape), _full_spec(g.shape), _full_spec(w_kv.shape),
                  _full_spec(kg.shape)],
        out_specs=[_full_spec((R, w)), _full_spec((R, w))],
        out_shape=[jax.ShapeDtypeStruct((R, w), BF16)] * 2,
        compiler_params=_params(("arbitrary",)),
        name="mem_kv",
    )(mem2, g, w_kv, kg)


def _xa_kernel(x_ref, g_ref, wq_ref, qg_ref, k_ref, v_ref, wo_ref, o_ref, *, nh):
    x = x_ref[...]
    h = _rms(x, g_ref[...]).astype(BF16)
    qf = _dot(h, wq_ref[...])
    scale = HEAD_DIM ** -0.5
    outs = []
    for hd in range(nh):
        sl = slice(hd * HEAD_DIM, (hd + 1) * HEAD_DIM)
        q = (_rms(qf[:, sl], qg_ref[...]) * scale).astype(BF16)
        s = _dot_nt(q, k_ref[:, sl])
        s = s - jnp.max(s, axis=-1, keepdims=True)
        e = jnp.exp(s)
        p = e / jnp.sum(e, axis=-1, keepdims=True)
        outs.append(_dot(p.astype(BF16), v_ref[:, sl]).astype(BF16))
    o = jnp.concatenate(outs, axis=-1)
    o_ref[...] = x + _dot(o, wo_ref[...])


def _cross_attention(x, g, w_q, qg, k, v, w_o):
    B, S, D = x.shape
    M, W = k.shape[1], k.shape[2]
    tm = _tile(S, 512)
    return pl.pallas_call(
        functools.partial(_xa_kernel, nh=W // HEAD_DIM),
        grid=(B, S // tm),
        in_specs=[
            pl.BlockSpec((None, tm, D), lambda b, i: (b, i, 0)),
            pl.BlockSpec((1, D), lambda b, i: (0, 0)),
            pl.BlockSpec((D, W), lambda b, i: (0, 0)),
            pl.BlockSpec((1, HEAD_DIM), lambda b, i: (0, 0)),
            pl.BlockSpec((None, M, W), lambda b, i: (b, 0, 0)),
            pl.BlockSpec((None, M, W), lambda b, i: (b, 0, 0)),
            pl.BlockSpec((W, D), lambda b, i: (0, 0)),
        ],
        out_specs=pl.BlockSpec((None, tm, D), lambda b, i: (b, i, 0)),
        out_shape=jax.ShapeDtypeStruct((B, S, D), F32),
        compiler_params=_params(("parallel", "parallel")),
        name="mem_cross_attention",
    )(x, g, w_q, qg, k, v, w_o)


def _row(v):
    return v.reshape(1, -1).astype(F32)


def _sb_gla_mixer(x, h2, w_in, sb_qg, sb_kg, w_gate, b_gate, gla_og, w_out):
    B, S, D = x.shape
    T = B * S
    x2 = x.reshape(T, D)
    n_heads = D // HEAD_DIM
    n_sb = n_heads // 4
    n_gla = n_heads - n_sb
    sbw = n_sb * HEAD_DIM
    kw = n_gla * GLA_DK
    vw = n_gla * HEAD_DIM
    o = 0
    w_sb = w_in[:, o:o + 3 * sbw]; o += 3 * sbw
    w_qk = w_in[:, o:o + 2 * kw]; o += 2 * kw
    w_v = w_in[:, o:o + vw]; o += vw
    w_r = w_in[:, o:o + vw]; o += vw
    w_g = w_in[:, o:o + GLA_GATE_RANK]
    w_qkg = jnp.concatenate([w_qk, w_g, jnp.zeros((D, LANES - GLA_GATE_RANK), w_in.dtype)], axis=1)
    w_gate_p = jnp.concatenate(
        [w_gate, jnp.zeros((LANES - GLA_GATE_RANK, kw), w_gate.dtype)], axis=0).astype(BF16)

    q_sb, k_sb, v_sb = _proj_call(
        functools.partial(_proj_sb_kernel, nh=n_sb), h2,
        [w_sb.astype(BF16), _row(sb_qg), _row(sb_kg)],
        [sbw, sbw, sbw], [BF16] * 3, "proj_sb")
    q_g, k_g, log_a = _proj_call(
        functools.partial(_proj_gla_kernel, kw=kw), h2,
        [w_qkg.astype(BF16), w_gate_p, _row(b_gate)],
        [kw, kw, kw], [F32] * 3, "proj_gla_qk")
    (v_g,) = _proj_call(functools.partial(_proj_plain_kernel, silu=False), h2,
                        [w_v.astype(BF16)], [vw], [BF16], "proj_gla_v")
    (r_g,) = _proj_call(functools.partial(_proj_plain_kernel, silu=True), h2,
                        [w_r.astype(BF16)], [vw], [BF16], "proj_gla_r")

    o_sb = _stick_breaking(q_sb.reshape(B, S, sbw), k_sb.reshape(B, S, sbw),
                           v_sb.reshape(B, S, sbw), n_sb)
    o_g = _gla(q_g.reshape(B, S, kw), k_g.reshape(B, S, kw), log_a.reshape(B, S, kw),
               v_g.reshape(B, S, vw), r_g.reshape(B, S, vw), _row(gla_og))
    out = _out_proj(x2, o_sb.reshape(T, sbw), o_g.reshape(T, vw), w_out.astype(BF16))
    return out.reshape(B, S, D)


def kernel(x, mem, ffn1_norm, ffn1_w_gu, ffn1_w_down, mix_norm, ab_w_in, sb_q_norm, sb_k_norm,
           gla_w_gate, gla_b_gate, gla_o_norm, ab_w_out, conv_w_in, conv_w, conv_w_out, xa_norm,
           mem_norm, xa_w_q, xa_w_kv, xa_q_norm, xa_k_norm, xa_w_o, ffn2_norm, ffn2_w_gu,
           ffn2_w_down):
    B, S, D = x.shape
    T = B * S
    depth = ffn1_norm.shape[0]
    mem2 = mem.reshape(-1, D)
    for layer in range(depth):
        i = layer // 2
        even = layer % 2 == 0
        ffn1 = _swiglu_ffn(x.reshape(T, D), _row(ffn1_norm[layer]), ffn1_w_gu[layer].astype(BF16),
                           ffn1_w_down[layer].astype(BF16), scale=0.5,
                           next_gain=_row(mix_norm[layer]) if even else None)
        if even:
            x = _sb_gla_mixer(ffn1[0].reshape(B, S, D), ffn1[1], ab_w_in[i], sb_q_norm[i],
                              sb_k_norm[i], gla_w_gate[i], gla_b_gate[i], gla_o_norm[i], ab_w_out[i])
        else:
            x = _conv_mixer(ffn1.reshape(B, S, D), _row(mix_norm[layer]), conv_w_in[i].astype(BF16),
                            conv_w[i].astype(F32), conv_w_out[i].astype(BF16))
        k_m, v_m = _mem_kv(mem2, _row(mem_norm[layer]), xa_w_kv[layer].astype(BF16),
                           _row(xa_k_norm[layer]))
        xw = k_m.shape[1]
        x = _cross_attention(x, _row(xa_norm[layer]), xa_w_q[layer].astype(BF16),
                             _row(xa_q_norm[layer]), k_m.reshape(B, -1, xw),
                             v_m.reshape(B, -1, xw), xa_w_o[layer].astype(BF16))
        x = _swiglu_ffn(x.reshape(T, D), _row(ffn2_norm[layer]), ffn2_w_gu[layer].astype(BF16),
                        ffn2_w_down[layer].astype(BF16), scale=0.5).reshape(B, S, D)
    return x
```

```python
import functools

import jax
import jax.numpy as jnp
from jax import lax
from jax.experimental import pallas as pl
from jax.experimental.pallas import tpu as pltpu

F32 = jnp.float32
BF16 = jnp.bfloat16

EPS = 1e-6
HEAD_DIM = 128
GLA_DK = 64
GLA_GATE_RANK = 16
GLA_GATE_TAU = 16.0
LANES = 128
LOG2E = 1.4426950408889634
UNDERFLOW_LOG2 = 160.0
SCORE_BOUND_SLACK = 1.001
CONV_SPLIT = 2
DIAG_BLOCK = 8
GLA_MODERATE_GATE = 10.0
VMEM_LIMIT = 56 * 1024 * 1024

_NT = (((1,), (1,)), ((), ()))
_TN = (((0,), (0,)), ((), ()))


def _params(sem):
    return pltpu.CompilerParams(dimension_semantics=sem, vmem_limit_bytes=VMEM_LIMIT)


def _dot(a, b):
    return jnp.dot(a, b, preferred_element_type=F32)


def _dot_nt(a, b):
    return lax.dot_general(a, b, _NT, preferred_element_type=F32)


def _dot_tn(a, b):
    return lax.dot_general(a, b, _TN, preferred_element_type=F32)


def _rms(xf, g):
    ms = jnp.mean(xf * xf, axis=-1, keepdims=True)
    return xf * lax.rsqrt(ms + EPS) * g


def _sigmoid(x):
    return 1.0 / (1.0 + jnp.exp(-x))


def _softplus(x):
    return jnp.maximum(x, 0.0) + jnp.log(1.0 + jnp.exp(-jnp.abs(x)))


def _column_blocks(w, width):
    k, n = w.shape
    return w.reshape(k, n // width, width).transpose(1, 0, 2)


def _tile(n, pref):
    t = min(pref, n)
    while n % t:
        t //= 2
    return t


def _glu_kernel(x_ref, g_ref, wa_ref, wb_ref, wd_ref, *rest, scale, nf, next_norm):
    if next_norm:
        g2_ref, o_ref, hn_ref, h_ref = rest
    else:
        o_ref, h_ref = rest
    i = pl.program_id(0)
    f = pl.program_id(1)
    slot = i % 2

    @pl.when(jnp.logical_and(i == 0, f == 0))
    def _():
        h_ref[0] = _rms(x_ref[...], g_ref[...]).astype(BF16)

    def down():
        h = h_ref[slot]
        a = _dot(h, wa_ref[...])
        b = _dot(h, wb_ref[...])
        return _dot((a * _sigmoid(a) * b).astype(BF16), wd_ref[...])

    @pl.when(f == 0)
    def _():
        o_ref[...] = x_ref[...] * (1.0 / scale) + down()

    @pl.when(jnp.logical_and(f > 0, f < nf - 1))
    def _():
        o_ref[...] += down()

    @pl.when(f == nf - 1)
    def _():
        out = scale * (o_ref[...] + down())
        o_ref[...] = out
        if next_norm:
            hn_ref[...] = _rms(out, g2_ref[...]).astype(BF16)
        h_ref[1 - slot] = _rms(x_ref[...], g_ref[...]).astype(BF16)


def _swiglu_ffn(x2, g, w_gu, w_d, *, scale, next_gain=None):
    T, D = x2.shape
    F = w_d.shape[0]
    tm = _tile(T, 512)
    tf = _tile(F, 1024)
    nf = F // tf
    nt = T // tm
    assert nf >= 2
    next_norm = next_gain is not None
    row = pl.BlockSpec((tm, D), lambda i, f: (i, 0))
    vec = pl.BlockSpec((1, D), lambda i, f: (0, 0))
    in_specs = [
        pl.BlockSpec((tm, D), lambda i, f: (jnp.minimum(i + jnp.minimum(f, 1), nt - 1), 0)),
        vec,
        pl.BlockSpec((None, D, tf), lambda i, f: (f, 0, 0)),
        pl.BlockSpec((None, D, tf), lambda i, f: (f + nf, 0, 0)),
        pl.BlockSpec((tf, D), lambda i, f: (f, 0)),
    ]
    w_gu = _column_blocks(w_gu, tf)
    args = [x2, g, w_gu, w_gu, w_d]
    out_specs = row
    out_shape = jax.ShapeDtypeStruct((T, D), F32)
    if next_norm:
        in_specs.append(vec)
        args.append(next_gain)
        out_specs = [row, row]
        out_shape = [out_shape, jax.ShapeDtypeStruct((T, D), BF16)]
    return pl.pallas_call(
        functools.partial(_glu_kernel, scale=scale, nf=nf, next_norm=next_norm),
        grid=(nt, nf),
        in_specs=in_specs,
        out_specs=out_specs,
        out_shape=out_shape,
        scratch_shapes=[pltpu.VMEM((2, tm, D), BF16)],
        compiler_params=_params(("arbitrary", "arbitrary")),
        name="swiglu_ffn",
    )(*args)


def _conv_kernel(x_ref, g_ref, wb_ref, wc_ref, wu_ref, cw_ref, wo_ref, o_ref, h_ref, halo_ref,
                 *, nc):
    i = pl.program_id(1)
    c = pl.program_id(2)
    t = pl.program_id(0) * pl.num_programs(1) + i
    slot = t % 2

    @pl.when(jnp.logical_and(t == 0, c == 0))
    def _():
        h_ref[0] = _rms(x_ref[...], g_ref[...]).astype(BF16)

    @pl.when(i == 0)
    def _():
        halo_ref[c] = jnp.zeros(halo_ref.shape[1:], F32)

    def mix():
        h = h_ref[slot]
        tm, tc = h.shape[0], wb_ref.shape[1]
        prev = halo_ref[c]
        row = lax.broadcasted_iota(jnp.int32, (tm, tc // CONV_SPLIT), 0)
        acc = None
        for s in range(CONV_SPLIT):
            sl = slice(s * (tc // CONV_SPLIT), (s + 1) * (tc // CONV_SPLIT))
            bg = _dot(h, wb_ref[:, sl])
            cu = _dot(h, wc_ref[:, sl]) * _dot(h, wu_ref[:, sl])
            p1 = prev[7:8, sl]
            p2 = prev[6:7, sl]
            r1 = jnp.where(row == 0, p1, pltpu.roll(cu, 1, 0))
            r2 = jnp.where(row == 0, p2, jnp.where(row == 1, p1, pltpu.roll(cu, 2, 0)))
            y = cw_ref[0:1, sl] * r2 + cw_ref[1:2, sl] * r1 + cw_ref[2:3, sl] * cu
            halo_ref[c, :, sl] = cu[tm - 8:, :]
            d = _dot((bg * y).astype(BF16), wo_ref[sl, :])
            acc = d if acc is None else acc + d
        return acc

    @pl.when(c == 0)
    def _():
        o_ref[...] = x_ref[...] + mix()

    @pl.when(jnp.logical_and(c > 0, c < nc - 1))
    def _():
        o_ref[...] += mix()

    @pl.when(c == nc - 1)
    def _():
        o_ref[...] += mix()
        h_ref[1 - slot] = _rms(x_ref[...], g_ref[...]).astype(BF16)


def _conv_mixer(x, g, w_in, conv_w, w_out):
    B, S, D = x.shape
    tm = _tile(S, 512)
    tc = _tile(D, 512)
    nc = D // tc
    ni = S // tm
    assert nc >= 2

    def x_map(b, i, c):
        t = jnp.minimum(b * ni + i + jnp.minimum(c, 1), B * ni - 1)
        return (t // ni, t % ni, 0)

    return pl.pallas_call(
        functools.partial(_conv_kernel, nc=nc),
        grid=(B, ni, nc),
        in_specs=[
            pl.BlockSpec((None, tm, D), x_map),
            pl.BlockSpec((1, D), lambda b, i, c: (0, 0)),
            pl.BlockSpec((None, D, tc), lambda b, i, c: (c, 0, 0)),
            pl.BlockSpec((None, D, tc), lambda b, i, c: (c + nc, 0, 0)),
            pl.BlockSpec((None, D, tc), lambda b, i, c: (c + 2 * nc, 0, 0)),
            pl.BlockSpec((3, tc), lambda b, i, c: (0, c)),
            pl.BlockSpec((tc, D), lambda b, i, c: (c, 0)),
        ],
        out_specs=pl.BlockSpec((None, tm, D), lambda b, i, c: (b, i, 0)),
        out_shape=jax.ShapeDtypeStruct((B, S, D), F32),
        scratch_shapes=[pltpu.VMEM((2, tm, D), BF16), pltpu.VMEM((nc, 8, tc), F32)],
        compiler_params=_params(("arbitrary", "arbitrary", "arbitrary")),
        name="conv_mixer",
    )(x, g, *([_column_blocks(w_in, tc)] * 3), conv_w, w_out)


def _proj_sb_kernel(h_ref, w_ref, qg_ref, kg_ref, q_ref, k_ref, v_ref, *, nh):
    acc = _dot(h_ref[...], w_ref[...])
    w = nh * HEAD_DIM
    qscale = LOG2E * HEAD_DIM ** -0.5
    for hd in range(nh):
        lo = hd * HEAD_DIM
        q = acc[:, lo:lo + HEAD_DIM]
        k = acc[:, w + lo:w + lo + HEAD_DIM]
        q_ref[:, lo:lo + HEAD_DIM] = (_rms(q, qg_ref[...]) * qscale).astype(BF16)
        k_ref[:, lo:lo + HEAD_DIM] = _rms(k, kg_ref[...]).astype(BF16)
    v_ref[...] = acc[:, 2 * w:].astype(BF16)


def _proj_gla_kernel(h_ref, w_ref, wg_ref, bg_ref, q_ref, k_ref, la_ref, *, kw):
    acc = _dot(h_ref[...], w_ref[...])
    q_ref[...] = acc[:, :kw] * (GLA_DK ** -0.5)
    k_ref[...] = acc[:, kw:2 * kw]
    logit = _dot(acc[:, 2 * kw:].astype(BF16), wg_ref[...]) + bg_ref[...]
    la_ref[...] = -_softplus(-logit) * (1.0 / GLA_GATE_TAU)


def _proj_plain_kernel(h_ref, w_ref, o_ref, *, silu):
    acc = _dot(h_ref[...], w_ref[...])
    if silu:
        acc = acc * _sigmoid(acc)
    o_ref[...] = acc.astype(BF16)


def _row_spec(tm, n):
    return pl.BlockSpec((tm, n), lambda i: (i, 0))


def _full_spec(shape):
    return pl.BlockSpec(shape, lambda i: (0,) * len(shape))


def _proj_call(kernel, h2, extra_in, out_widths, out_dtypes, name):
    T, D = h2.shape
    tm = _tile(T, 512)
    in_specs = [_row_spec(tm, D)] + [_full_spec(a.shape) for a in extra_in]
    return pl.pallas_call(
        kernel,
        grid=(T // tm,),
        in_specs=in_specs,
        out_specs=[_row_spec(tm, n) for n in out_widths],
        out_shape=[jax.ShapeDtypeStruct((T, n), dt) for n, dt in zip(out_widths, out_dtypes)],
        compiler_params=_params(("parallel",)),
        name=name,
    )(h2, *extra_in)


def _sb_kernel(q_ref, k_ref, v_ref, o_ref, acc_ref, run_ref, kmax_ref, *, tq, tk, group):
    i = pl.program_id(2)
    span = tk * group
    n_keys = k_ref.shape[0]

    @pl.when(i == 0)
    def _():
        def body(n, m):
            kk = k_ref[pl.ds(pl.multiple_of(n * span, span), span), :].astype(F32)
            return jnp.maximum(m, jnp.max(jnp.sum(kk * kk, axis=1, keepdims=True), axis=0, keepdims=True))
        m = lax.fori_loop(0, n_keys // span, body, jnp.zeros((1, 1), F32))
        kmax_ref[...] = jnp.sqrt(m)

    acc_ref[...] = jnp.zeros_like(acc_ref)
    run_ref[...] = jnp.zeros_like(run_ref)
    q = q_ref[...]
    qf = q.astype(F32)
    zmax = jnp.sqrt(jnp.sum(qf * qf, axis=1, keepdims=True)) * kmax_ref[...] * SCORE_BOUND_SLACK
    row = lax.broadcasted_iota(jnp.int32, (tq, tk), 0)
    col = lax.broadcasted_iota(jnp.int32, (tq, tk), 1)
    krow = lax.broadcasted_iota(jnp.int32, (tk, tk), 0)
    kcol = lax.broadcasted_iota(jnp.int32, (tk, tk), 1)
    suffix = (krow >= kcol).astype(BF16)

    def key_tiles(first_tile, ntiles, masked):
        base = pl.multiple_of(first_tile * tk, tk)
        run = run_ref[...]
        ws = [None] * ntiles
        for c in reversed(range(ntiles)):
            start = pl.multiple_of(base + c * tk, tk)
            z = _dot_nt(q, k_ref[pl.ds(start, tk), :])
            sp = jnp.maximum(z, 0.0) + jnp.log(1.0 + jnp.exp2(-jnp.abs(z))) * LOG2E
            if masked:
                causal = (col + start) < (row + i * tq)
                sp = jnp.where(causal, sp, 0.0)
            within = _dot(sp.astype(BF16), suffix)
            w = jnp.exp2(z - within - run)
            if masked:
                w = jnp.where(causal, w, 0.0)
            ws[c] = w.astype(BF16)
            run = run + jnp.sum(sp, axis=1, keepdims=True)
        acc_ref[...] += _dot(jnp.concatenate(ws, axis=1), v_ref[pl.ds(base, ntiles * tk), :])
        run_ref[...] = run

    def any_weight_left():
        return jnp.max(zmax - run_ref[...]) > -UNDERFLOW_LOG2

    before = jnp.maximum(i - (group - 1), 0)
    key_tiles(before, group, True)

    def sweep(count, first_of, ntiles, alive):
        def cond(carry):
            n, live = carry
            return jnp.logical_and(n < count, live)

        def step(carry):
            n, _ = carry
            key_tiles(first_of(n), ntiles, False)
            return n + 1, any_weight_left()

        return lax.while_loop(cond, step, (jnp.int32(0), alive))[1]

    alive = sweep(before // group, lambda n: before - group * (n + 1), group, any_weight_left())
    sweep(before % group, lambda n: before % group - 1 - n, 1, alive)
    o_ref[...] = acc_ref[...].astype(BF16)


def _stick_breaking(q, k, v, nh):
    B, S, _ = q.shape
    tq = _tile(S, 256)
    tk = tq
    group = _tile(S // tk, 2)
    return pl.pallas_call(
        functools.partial(_sb_kernel, tq=tq, tk=tk, group=group),
        grid=(B, nh, S // tq),
        in_specs=[
            pl.BlockSpec((None, tq, HEAD_DIM), lambda b, h, i: (b, i, h)),
            pl.BlockSpec((None, S, HEAD_DIM), lambda b, h, i: (b, 0, h)),
            pl.BlockSpec((None, S, HEAD_DIM), lambda b, h, i: (b, 0, h)),
        ],
        out_specs=pl.BlockSpec((None, tq, HEAD_DIM), lambda b, h, i: (b, i, h)),
        out_shape=jax.ShapeDtypeStruct(q.shape, BF16),
        scratch_shapes=[pltpu.VMEM((tq, HEAD_DIM), F32), pltpu.VMEM((tq, 1), F32),
                        pltpu.VMEM((1, 1), F32)],
        compiler_params=_params(("arbitrary", "arbitrary", "arbitrary")),
        name="stick_breaking",
    )(q, k, v)


def _split2(x):
    a = x.astype(BF16)
    return a, (x - a.astype(F32)).astype(BF16)


def _gla_kernel(q_ref, k_ref, la_ref, v_ref, r_ref, gn_ref, o_ref, s_ref, *, chunk, npair):
    n = pl.program_id(1)

    @pl.when(n == 0)
    def _():
        s_ref[...] = jnp.zeros_like(s_ref)

    C = chunk
    row = lax.broadcasted_iota(jnp.int32, (C, C), 0)
    col = lax.broadcasted_iota(jnp.int32, (C, C), 1)
    rcol = lax.broadcasted_iota(jnp.int32, (C, 1), 0)
    lane = lax.broadcasted_iota(jnp.int32, (C, LANES), 1)
    first = lane < GLA_DK
    sq_first = lax.broadcasted_iota(jnp.int32, (HEAD_DIM, LANES), 1) < GLA_DK
    zero = jnp.zeros((), BF16)

    def chunk_body(fine):
        q = q_ref[...]
        k = k_ref[...]
        la = la_ref[...]
        g_hi, g_lo = _split2(la)

        def rowsum(mask):
            m = mask.astype(BF16)
            return _dot(m, g_hi) + _dot(m, g_lo)

        def block_row(x, blk, at):
            x3 = x.reshape(C // blk, blk, x.shape[1])
            return jnp.broadcast_to(x3[:, at:at + 1, :], x3.shape).reshape(x.shape)

        b = rowsum(col <= row)
        b_last = b[C - 1:C, :]

        if fine:
            levels = [(q.astype(BF16), k.astype(BF16), row == col)]
            m = 1
        else:
            ref = block_row(b, DIAG_BLOCK, DIAG_BLOCK // 2 - 1)
            levels = [((q * jnp.exp(b - ref)).astype(BF16), (k * jnp.exp(ref - b)).astype(BF16),
                       jnp.logical_and(row // DIAG_BLOCK == col // DIAG_BLOCK, col <= row))]
            m = DIAG_BLOCK
        while m < C:
            if fine:
                ref = rowsum(col <= (row // (2 * m)) * (2 * m) + (m - 1))
            else:
                ref = block_row(b, 2 * m, m - 1)
            odd = (rcol // m) % 2 == 1
            qs = jnp.where(odd, q * jnp.exp(jnp.where(odd, b - ref, 0.0)), 0.0)
            ks = jnp.where(odd, 0.0, k * jnp.exp(jnp.where(odd, 0.0, ref - b)))
            levels.append((qs.astype(BF16), ks.astype(BF16), (row // (2 * m)) == (col // (2 * m))))
            m *= 2

        q_inter = (q * jnp.exp(b)).astype(BF16)
        k_state = (k * jnp.exp(b_last - b)).astype(BF16)

        for p in range(npair):
            ksl = slice(p * LANES, (p + 1) * LANES)
            state = s_ref[p]
            state_b = state.astype(BF16)
            scores = jnp.zeros((2 * C, C), F32)
            for qs, ks, mask in levels:
                qp = qs[:, ksl]
                q2 = jnp.concatenate([jnp.where(first, qp, zero), jnp.where(first, zero, qp)], axis=0)
                mask2 = jnp.concatenate([mask, mask], axis=0)
                scores += jnp.where(mask2, _dot_nt(q2, ks[:, ksl]), 0.0)
            qi = q_inter[:, ksl]
            inter = _dot_nt(jnp.concatenate([jnp.where(first, qi, zero), jnp.where(first, zero, qi)],
                                            axis=0), state_b)
            kv = []
            for e in range(2):
                hd = 2 * p + e
                vsl = slice(hd * HEAD_DIM, (hd + 1) * HEAD_DIM)
                rows = slice(e * C, (e + 1) * C)
                v = v_ref[:, vsl]
                o = _dot(scores[rows].astype(BF16), v) + inter[rows]
                o = _rms(o, gn_ref[...]) * r_ref[:, vsl].astype(F32)
                o_ref[:, vsl] = o.astype(BF16)
                kv.append(_dot_tn(v, k_state[:, ksl]))
            s_ref[p] = jnp.exp(b_last[:, ksl]) * state + jnp.where(sq_first, kv[0], kv[1])

    moderate = jnp.min(la_ref[...]) > -GLA_MODERATE_GATE

    @pl.when(moderate)
    def _():
        chunk_body(fine=False)

    @pl.when(jnp.logical_not(moderate))
    def _():
        chunk_body(fine=True)


def _gla(q, k, la, v, r, gn):
    B, S, kw = q.shape
    vw = v.shape[-1]
    chunk = _tile(S, 128)
    npair = kw // LANES
    return pl.pallas_call(
        functools.partial(_gla_kernel, chunk=chunk, npair=npair),
        grid=(B, S // chunk),
        in_specs=[
            pl.BlockSpec((None, chunk, kw), lambda b, n: (b, n, 0)),
            pl.BlockSpec((None, chunk, kw), lambda b, n: (b, n, 0)),
            pl.BlockSpec((None, chunk, kw), lambda b, n: (b, n, 0)),
            pl.BlockSpec((None, chunk, vw), lambda b, n: (b, n, 0)),
            pl.BlockSpec((None, chunk, vw), lambda b, n: (b, n, 0)),
            pl.BlockSpec((1, HEAD_DIM), lambda b, n: (0, 0)),
        ],
        out_specs=pl.BlockSpec((None, chunk, vw), lambda b, n: (b, n, 0)),
        out_shape=jax.ShapeDtypeStruct(v.shape, BF16),
        scratch_shapes=[pltpu.VMEM((npair, HEAD_DIM, LANES), F32)],
        compiler_params=_params(("arbitrary", "arbitrary")),
        name="gla",
    )(q, k, la, v, r, gn)


def _outproj_kernel(x_ref, a_ref, b_ref, wa_ref, wb_ref, o_ref):
    o_ref[...] = x_ref[...] + _dot(a_ref[...], wa_ref[...]) + _dot(b_ref[...], wb_ref[...])


def _out_proj(x2, a, b, w_out):
    T, D = x2.shape
    wa, wb = a.shape[1], b.shape[1]
    tm = _tile(T, 512)
    return pl.pallas_call(
        _outproj_kernel,
        grid=(T // tm,),
        in_specs=[
            _row_spec(tm, D), _row_spec(tm, wa), _row_spec(tm, wb),
            pl.BlockSpec((wa, D), lambda i: (0, 0)),
            pl.BlockSpec((wb, D), lambda i: (0, 0)),
        ],
        out_specs=_row_spec(tm, D),
        out_shape=jax.ShapeDtypeStruct((T, D), F32),
        compiler_params=_params(("parallel",)),
        name="mix_out_proj",
    )(x2, a, b, w_out[:wa], w_out[wa:])


def _memkv_kernel(m_ref, g_ref, w_ref, kg_ref, k_ref, v_ref, *, nh):
    h = _rms(m_ref[...], g_ref[...]).astype(BF16)
    acc = _dot(h, w_ref[...])
    w = nh * HEAD_DIM
    for hd in range(nh):
        sl = slice(hd * HEAD_DIM, (hd + 1) * HEAD_DIM)
        k_ref[:, sl] = _rms(acc[:, sl], kg_ref[...]).astype(BF16)
    v_ref[...] = acc[:, w:].astype(BF16)


def _mem_kv(mem2, g, w_kv, kg):
    R, D = mem2.shape
    w = w_kv.shape[1] // 2
    return pl.pallas_call(
        functools.partial(_memkv_kernel, nh=w // HEAD_DIM),
        grid=(1,),
        in_specs=[_full_spec(mem2.shape), _full_spec(g.shape), _full_spec(w_kv.shape),
                  _full_spec(kg.shape)],
        out_specs=[_full_spec((R, w)), _full_spec((R, w))],
        out_shape=[jax.ShapeDtypeStruct((R, w), BF16)] * 2,
        compiler_params=_params(("arbitrary",)),
        name="mem_kv",
    )(mem2, g, w_kv, kg)


def _xa_kernel(x_ref, g_ref, wq_ref, qg_ref, k_ref, v_ref, wo_ref, o_ref, *, nh):
    x = x_ref[...]
    h = _rms(x, g_ref[...]).astype(BF16)
    qf = _dot(h, wq_ref[...])
    scale = HEAD_DIM ** -0.5
    outs = []
    for hd in range(nh):
        sl = slice(hd * HEAD_DIM, (hd + 1) * HEAD_DIM)
        q = (_rms(qf[:, sl], qg_ref[...]) * scale).astype(BF16)
        s = _dot_nt(q, k_ref[:, sl])
        s = s - jnp.max(s, axis=-1, keepdims=True)
        e = jnp.exp(s)
        p = e / jnp.sum(e, axis=-1, keepdims=True)
        outs.append(_dot(p.astype(BF16), v_ref[:, sl]).astype(BF16))
    o = jnp.concatenate(outs, axis=-1)
    o_ref[...] = x + _dot(o, wo_ref[...])


def _cross_attention(x, g, w_q, qg, k, v, w_o):
    B, S, D = x.shape
    M, W = k.shape[1], k.shape[2]
    tm = _tile(S, 512)
    return pl.pallas_call(
        functools.partial(_xa_kernel, nh=W // HEAD_DIM),
        grid=(B, S // tm),
        in_specs=[
            pl.BlockSpec((None, tm, D), lambda b, i: (b, i, 0)),
            pl.BlockSpec((1, D), lambda b, i: (0, 0)),
            pl.BlockSpec((D, W), lambda b, i: (0, 0)),
            pl.BlockSpec((1, HEAD_DIM), lambda b, i: (0, 0)),
            pl.BlockSpec((None, M, W), lambda b, i: (b, 0, 0)),
            pl.BlockSpec((None, M, W), lambda b, i: (b, 0, 0)),
            pl.BlockSpec((W, D), lambda b, i: (0, 0)),
        ],
        out_specs=pl.BlockSpec((None, tm, D), lambda b, i: (b, i, 0)),
        out_shape=jax.ShapeDtypeStruct((B, S, D), F32),
        compiler_params=_params(("parallel", "parallel")),
        name="mem_cross_attention",
    )(x, g, w_q, qg, k, v, w_o)


def _row(v):
    return v.reshape(1, -1).astype(F32)


def _sb_gla_mixer(x, h2, w_in, sb_qg, sb_kg, w_gate, b_gate, gla_og, w_out):
    B, S, D = x.shape
    T = B * S
    x2 = x.reshape(T, D)
    n_heads = D // HEAD_DIM
    n_sb = n_heads // 4
    n_gla = n_heads - n_sb
    sbw = n_sb * HEAD_DIM
    kw = n_gla * GLA_DK
    vw = n_gla * HEAD_DIM
    o = 0
    w_sb = w_in[:, o:o + 3 * sbw]; o += 3 * sbw
    w_qk = w_in[:, o:o + 2 * kw]; o += 2 * kw
    w_v = w_in[:, o:o + vw]; o += vw
    w_r = w_in[:, o:o + vw]; o += vw
    w_g = w_in[:, o:o + GLA_GATE_RANK]
    w_qkg = jnp.concatenate([w_qk, w_g, jnp.zeros((D, LANES - GLA_GATE_RANK), w_in.dtype)], axis=1)
    w_gate_p = jnp.concatenate(
        [w_gate, jnp.zeros((LANES - GLA_GATE_RANK, kw), w_gate.dtype)], axis=0).astype(BF16)

    q_sb, k_sb, v_sb = _proj_call(
        functools.partial(_proj_sb_kernel, nh=n_sb), h2,
        [w_sb.astype(BF16), _row(sb_qg), _row(sb_kg)],
        [sbw, sbw, sbw], [BF16] * 3, "proj_sb")
    q_g, k_g, log_a = _proj_call(
        functools.partial(_proj_gla_kernel, kw=kw), h2,
        [w_qkg.astype(BF16), w_gate_p, _row(b_gate)],
        [kw, kw, kw], [F32] * 3, "proj_gla_qk")
    (v_g,) = _proj_call(functools.partial(_proj_plain_kernel, silu=False), h2,
                        [w_v.astype(BF16)], [vw], [BF16], "proj_gla_v")
    (r_g,) = _proj_call(functools.partial(_proj_plain_kernel, silu=True), h2,
                        [w_r.astype(BF16)], [vw], [BF16], "proj_gla_r")

    o_sb = _stick_breaking(q_sb.reshape(B, S, sbw), k_sb.reshape(B, S, sbw),
                           v_sb.reshape(B, S, sbw), n_sb)
    o_g = _gla(q_g.reshape(B, S, kw), k_g.reshape(B, S, kw), log_a.reshape(B, S, kw),
               v_g.reshape(B, S, vw), r_g.reshape(B, S, vw), _row(gla_og))
    out = _out_proj(x2, o_sb.reshape(T, sbw), o_g.reshape(T, vw), w_out.astype(BF16))
    return out.reshape(B, S, D)


def kernel(x, mem, ffn1_norm, ffn1_w_gu, ffn1_w_down, mix_norm, ab_w_in, sb_q_norm, sb_k_norm,
           gla_w_gate, gla_b_gate, gla_o_norm, ab_w_out, conv_w_in, conv_w, conv_w_out, xa_norm,
           mem_norm, xa_w_q, xa_w_kv, xa_q_norm, xa_k_norm, xa_w_o, ffn2_norm, ffn2_w_gu,
           ffn2_w_down):
    B, S, D = x.shape
    T = B * S
    depth = ffn1_norm.shape[0]
    mem2 = mem.reshape(-1, D)
    for layer in range(depth):
        i = layer // 2
        even = layer % 2 == 0
        ffn1 = _swiglu_ffn(x.reshape(T, D), _row(ffn1_norm[layer]), ffn1_w_gu[layer].astype(BF16),
                           ffn1_w_down[layer].astype(BF16), scale=0.5,
                           next_gain=_row(mix_norm[layer]) if even else None)
        if even:
            x = _sb_gla_mixer(ffn1[0].reshape(B, S, D), ffn1[1], ab_w_in[i], sb_q_norm[i],
                              sb_k_norm[i], gla_w_gate[i], gla_b_gate[i], gla_o_norm[i], ab_w_out[i])
        else:
            x = _conv_mixer(ffn1.reshape(B, S, D), _row(mix_norm[layer]), conv_w_in[i].astype(BF16),
                            conv_w[i].astype(F32), conv_w_out[i].astype(BF16))
        k_m, v_m = _mem_kv(mem2, _row(mem_norm[layer]), xa_w_kv[layer].astype(BF16),
                           _row(xa_k_norm[layer]))
        xw = k_m.shape[1]
        x = _cross_attention(x, _row(xa_norm[layer]), xa_w_q[layer].astype(BF16),
                             _row(xa_q_norm[layer]), k_m.reshape(B, -1, xw),
                             v_m.reshape(B, -1, xw), xa_w_o[layer].astype(BF16))
        x = _swiglu_ffn(x.reshape(T, D), _row(ffn2_norm[layer]), ffn2_w_gu[layer].astype(BF16),
                        ffn2_w_down[layer].astype(BF16), scale=0.5).reshape(B, S, D)
    return x
```

```python
import functools

import jax
import jax.numpy as jnp
from jax import lax
from jax.experimental import pallas as pl
from jax.experimental.pallas import tpu as pltpu

F32 = jnp.float32
BF16 = jnp.bfloat16

EPS = 1e-6
HEAD_DIM = 128
GLA_DK = 64
GLA_GATE_RANK = 16
GLA_GATE_TAU = 16.0
LANES = 128
LOG2E = 1.4426950408889634
UNDERFLOW_LOG2 = 160.0
SCORE_BOUND_SLACK = 1.001
CONV_SPLIT = 2
DIAG_BLOCK = 8
GLA_MODERATE_GATE = 10.0
CAST_BLOCK_BYTES = 8 * 1024 * 1024
VMEM_LIMIT = 56 * 1024 * 1024

_NT = (((1,), (1,)), ((), ()))
_TN = (((0,), (0,)), ((), ()))


def _params(sem):
    return pltpu.CompilerParams(dimension_semantics=sem, vmem_limit_bytes=VMEM_LIMIT)


def _dot(a, b):
    return jnp.dot(a, b, preferred_element_type=F32)


def _dot_nt(a, b):
    return lax.dot_general(a, b, _NT, preferred_element_type=F32)


def _dot_tn(a, b):
    return lax.dot_general(a, b, _TN, preferred_element_type=F32)


def _rms(xf, g):
    ms = jnp.mean(xf * xf, axis=-1, keepdims=True)
    return xf * lax.rsqrt(ms + EPS) * g


def _sigmoid(x):
    return 1.0 / (1.0 + jnp.exp(-x))


def _softplus(x):
    return jnp.maximum(x, 0.0) + jnp.log(1.0 + jnp.exp(-jnp.abs(x)))


def _tile(n, pref):
    t = min(pref, n)
    while n % t:
        t //= 2
    return t


def _cast_kernel(w_ref, o_ref):
    o_ref[...] = w_ref[...].astype(BF16)


def _weight_bf16(w3, layer, col_block=0, ncols=None):
    _, K, N = w3.shape
    ncols = N if ncols is None else ncols
    limit = max(8, CAST_BLOCK_BYTES // (4 * ncols))
    rb = _tile(K, 1 << (limit.bit_length() - 1))
    return pl.pallas_call(
        _cast_kernel,
        grid=(K // rb,),
        in_specs=[pl.BlockSpec((None, rb, ncols), lambda r: (layer, r, col_block))],
        out_specs=pl.BlockSpec((rb, ncols), lambda r: (r, 0)),
        out_shape=jax.ShapeDtypeStruct((K, ncols), BF16),
        compiler_params=_params(("parallel",)),
        name="weight_to_bf16",
    )(w3)


def _glu_kernel(x_ref, g_ref, wa_ref, wb_ref, wd_ref, *rest, scale, nf, next_norm):
    if next_norm:
        g2_ref, o_ref, hn_ref, h_ref = rest
    else:
        o_ref, h_ref = rest
    i = pl.program_id(0)
    f = pl.program_id(1)
    slot = i % 2

    @pl.when(jnp.logical_and(i == 0, f == 0))
    def _():
        h_ref[0] = _rms(x_ref[...], g_ref[...]).astype(BF16)

    def down():
        h = h_ref[slot]
        a = _dot(h, wa_ref[...])
        b = _dot(h, wb_ref[...])
        return _dot((a * _sigmoid(a) * b).astype(BF16), wd_ref[...])

    @pl.when(f == 0)
    def _():
        o_ref[...] = x_ref[...] * (1.0 / scale) + down()

    @pl.when(jnp.logical_and(f > 0, f < nf - 1))
    def _():
        o_ref[...] += down()

    @pl.when(f == nf - 1)
    def _():
        out = scale * (o_ref[...] + down())
        o_ref[...] = out
        if next_norm:
            hn_ref[...] = _rms(out, g2_ref[...]).astype(BF16)
        h_ref[1 - slot] = _rms(x_ref[...], g_ref[...]).astype(BF16)


def _swiglu_ffn(x2, g, w_gu, w_d, *, scale, next_gain=None):
    T, D = x2.shape
    F = w_d.shape[0]
    tm = _tile(T, 512)
    tf = _tile(F, 1024)
    nf = F // tf
    nt = T // tm
    assert nf >= 2
    next_norm = next_gain is not None
    row = pl.BlockSpec((tm, D), lambda i, f: (i, 0))
    vec = pl.BlockSpec((1, D), lambda i, f: (0, 0))
    in_specs = [
        pl.BlockSpec((tm, D), lambda i, f: (jnp.minimum(i + jnp.minimum(f, 1), nt - 1), 0)),
        vec,
        pl.BlockSpec((D, tf), lambda i, f: (0, f)),
        pl.BlockSpec((D, tf), lambda i, f: (0, f + nf)),
        pl.BlockSpec((tf, D), lambda i, f: (f, 0)),
    ]
    args = [x2, g, w_gu, w_gu, w_d]
    out_specs = row
    out_shape = jax.ShapeDtypeStruct((T, D), F32)
    if next_norm:
        in_specs.append(vec)
        args.append(next_gain)
        out_specs = [row, row]
        out_shape = [out_shape, jax.ShapeDtypeStruct((T, D), BF16)]
    return pl.pallas_call(
        functools.partial(_glu_kernel, scale=scale, nf=nf, next_norm=next_norm),
        grid=(nt, nf),
        in_specs=in_specs,
        out_specs=out_specs,
        out_shape=out_shape,
        scratch_shapes=[pltpu.VMEM((2, tm, D), BF16)],
        compiler_params=_params(("arbitrary", "arbitrary")),
        name="swiglu_ffn",
    )(*args)


def _conv_kernel(x_ref, g_ref, wb_ref, wc_ref, wu_ref, cw_ref, wo_ref, o_ref, h_ref, halo_ref,
                 *, nc):
    i = pl.program_id(1)
    c = pl.program_id(2)
    t = pl.program_id(0) * pl.num_programs(1) + i
    slot = t % 2

    @pl.when(jnp.logical_and(t == 0, c == 0))
    def _():
        h_ref[0] = _rms(x_ref[...], g_ref[...]).astype(BF16)

    @pl.when(i == 0)
    def _():
        halo_ref[c] = jnp.zeros(halo_ref.shape[1:], F32)

    def mix():
        h = h_ref[slot]
        tm, tc = h.shape[0], wb_ref.shape[1]
        prev = halo_ref[c]
        row = lax.broadcasted_iota(jnp.int32, (tm, tc // CONV_SPLIT), 0)
        acc = None
        for s in range(CONV_SPLIT):
            sl = slice(s * (tc // CONV_SPLIT), (s + 1) * (tc // CONV_SPLIT))
            bg = _dot(h, wb_ref[:, sl])
            cu = _dot(h, wc_ref[:, sl]) * _dot(h, wu_ref[:, sl])
            p1 = prev[7:8, sl]
            p2 = prev[6:7, sl]
            r1 = jnp.where(row == 0, p1, pltpu.roll(cu, 1, 0))
            r2 = jnp.where(row == 0, p2, jnp.where(row == 1, p1, pltpu.roll(cu, 2, 0)))
            y = cw_ref[0:1, sl] * r2 + cw_ref[1:2, sl] * r1 + cw_ref[2:3, sl] * cu
            halo_ref[c, :, sl] = cu[tm - 8:, :]
            d = _dot((bg * y).astype(BF16), wo_ref[sl, :])
            acc = d if acc is None else acc + d
        return acc

    @pl.when(c == 0)
    def _():
        o_ref[...] = x_ref[...] + mix()

    @pl.when(jnp.logical_and(c > 0, c < nc - 1))
    def _():
        o_ref[...] += mix()

    @pl.when(c == nc - 1)
    def _():
        o_ref[...] += mix()
        h_ref[1 - slot] = _rms(x_ref[...], g_ref[...]).astype(BF16)


def _conv_mixer(x, g, w_in, conv_w, w_out):
    B, S, D = x.shape
    tm = _tile(S, 512)
    tc = _tile(D, 512)
    nc = D // tc
    ni = S // tm
    assert nc >= 2

    def x_map(b, i, c):
        t = jnp.minimum(b * ni + i + jnp.minimum(c, 1), B * ni - 1)
        return (t // ni, t % ni, 0)

    return pl.pallas_call(
        functools.partial(_conv_kernel, nc=nc),
        grid=(B, ni, nc),
        in_specs=[
            pl.BlockSpec((None, tm, D), x_map),
            pl.BlockSpec((1, D), lambda b, i, c: (0, 0)),
            pl.BlockSpec((D, tc), lambda b, i, c: (0, c)),
            pl.BlockSpec((D, tc), lambda b, i, c: (0, c + nc)),
            pl.BlockSpec((D, tc), lambda b, i, c: (0, c + 2 * nc)),
            pl.BlockSpec((3, tc), lambda b, i, c: (0, c)),
            pl.BlockSpec((tc, D), lambda b, i, c: (c, 0)),
        ],
        out_specs=pl.BlockSpec((None, tm, D), lambda b, i, c: (b, i, 0)),
        out_shape=jax.ShapeDtypeStruct((B, S, D), F32),
        scratch_shapes=[pltpu.VMEM((2, tm, D), BF16), pltpu.VMEM((nc, 8, tc), F32)],
        compiler_params=_params(("arbitrary", "arbitrary", "arbitrary")),
        name="conv_mixer",
    )(x, g, w_in, w_in, w_in, conv_w, w_out)


def _proj_sb_kernel(h_ref, w_ref, qg_ref, kg_ref, q_ref, k_ref, v_ref, *, nh):
    acc = _dot(h_ref[...], w_ref[...])
    w = nh * HEAD_DIM
    qscale = LOG2E * HEAD_DIM ** -0.5
    for hd in range(nh):
        lo = hd * HEAD_DIM
        q = acc[:, lo:lo + HEAD_DIM]
        k = acc[:, w + lo:w + lo + HEAD_DIM]
        q_ref[:, lo:lo + HEAD_DIM] = (_rms(q, qg_ref[...]) * qscale).astype(BF16)
        k_ref[:, lo:lo + HEAD_DIM] = _rms(k, kg_ref[...]).astype(BF16)
    v_ref[...] = acc[:, 2 * w:].astype(BF16)


def _proj_gla_kernel(h_ref, w_ref, wlr_ref, wg_ref, bg_ref, q_ref, k_ref, la_ref, *, kw):
    h = h_ref[...]
    acc = _dot(h, w_ref[...])
    q_ref[...] = acc[:, :kw] * (GLA_DK ** -0.5)
    k_ref[...] = acc[:, kw:]
    low_rank = _dot(h, wlr_ref[...])
    logit = _dot(low_rank.astype(BF16), wg_ref[...]) + bg_ref[...]
    la_ref[...] = -_softplus(-logit) * (1.0 / GLA_GATE_TAU)


def _proj_plain_kernel(h_ref, w_ref, o_ref, *, silu):
    acc = _dot(h_ref[...], w_ref[...])
    if silu:
        acc = acc * _sigmoid(acc)
    o_ref[...] = acc.astype(BF16)


def _row_spec(tm, n):
    return pl.BlockSpec((tm, n), lambda i: (i, 0))


def _full_spec(shape):
    return pl.BlockSpec(shape, lambda i: (0,) * len(shape))


def _proj_call(kernel, h2, extra_in, out_widths, out_dtypes, name):
    T, D = h2.shape
    tm = _tile(T, 512)
    in_specs = [_row_spec(tm, D)] + [_full_spec(a.shape) for a in extra_in]
    return pl.pallas_call(
        kernel,
        grid=(T // tm,),
        in_specs=in_specs,
        out_specs=[_row_spec(tm, n) for n in out_widths],
        out_shape=[jax.ShapeDtypeStruct((T, n), dt) for n, dt in zip(out_widths, out_dtypes)],
        compiler_params=_params(("parallel",)),
        name=name,
    )(h2, *extra_in)


def _sb_kernel(q_ref, k_ref, v_ref, o_ref, acc_ref, run_ref, kmax_ref, *, tq, tk, group, nsub):
    step = pl.program_id(2)
    span = tk * group
    n_keys = k_ref.shape[0]

    @pl.when(step == 0)
    def _():
        def body(n, m):
            kk = k_ref[pl.ds(pl.multiple_of(n * span, span), span), :].astype(F32)
            return jnp.maximum(m, jnp.max(jnp.sum(kk * kk, axis=1, keepdims=True), axis=0, keepdims=True))
        m = lax.fori_loop(0, n_keys // span, body, jnp.zeros((1, 1), F32))
        kmax_ref[...] = jnp.sqrt(m)

    acc_ref[...] = jnp.zeros_like(acc_ref)
    run_ref[...] = jnp.zeros_like(run_ref)
    row = lax.broadcasted_iota(jnp.int32, (tq, tk), 0)
    col = lax.broadcasted_iota(jnp.int32, (tq, tk), 1)
    krow = lax.broadcasted_iota(jnp.int32, (tk, tk), 0)
    kcol = lax.broadcasted_iota(jnp.int32, (tk, tk), 1)
    suffix = (krow >= kcol).astype(BF16)

    def key_tiles(r, i, q, first_tile, ntiles, masked):
        base = pl.multiple_of(first_tile * tk, tk)
        run = run_ref[r]
        ws = [None] * ntiles
        for c in reversed(range(ntiles)):
            start = pl.multiple_of(base + c * tk, tk)
            z = _dot_nt(q, k_ref[pl.ds(start, tk), :])
            sp = jnp.maximum(z, 0.0) + jnp.log(1.0 + jnp.exp2(-jnp.abs(z))) * LOG2E
            if masked:
                causal = (col + start) < (row + i * tq)
                sp = jnp.where(causal, sp, 0.0)
            within = _dot(sp.astype(BF16), suffix)
            w = jnp.exp2(z - within - run)
            if masked:
                w = jnp.where(causal, w, 0.0)
            ws[c] = w.astype(BF16)
            run = run + jnp.sum(sp, axis=1, keepdims=True)
        acc_ref[r] += _dot(jnp.concatenate(ws, axis=1), v_ref[pl.ds(base, ntiles * tk), :])
        run_ref[r] = run

    subs = []
    for r in range(nsub):
        i = step * nsub + r
        q = q_ref[r * tq:(r + 1) * tq, :]
        qf = q.astype(F32)
        zmax = jnp.sqrt(jnp.sum(qf * qf, axis=1, keepdims=True)) * kmax_ref[...] * SCORE_BOUND_SLACK
        before = jnp.maximum(i - (group - 1), 0)
        key_tiles(r, i, q, before, group, True)
        subs.append((i, q, zmax, before))

    for r, (i, q, zmax, before) in enumerate(subs):
        def any_weight_left(r=r, zmax=zmax):
            return jnp.max(zmax - run_ref[r]) > -UNDERFLOW_LOG2

        def sweep(count, first_of, ntiles, alive, r=r, i=i, q=q, any_weight_left=any_weight_left):
            def cond(carry):
                n, live = carry
                return jnp.logical_and(n < count, live)

            def body(carry):
                n, _ = carry
                key_tiles(r, i, q, first_of(n), ntiles, False)
                return n + 1, any_weight_left()

            return lax.while_loop(cond, body, (jnp.int32(0), alive))[1]

        alive = sweep(before // group, lambda n, before=before: before - group * (n + 1), group,
                      any_weight_left())
        sweep(before % group, lambda n, before=before: before % group - 1 - n, 1, alive)
        o_ref[r * tq:(r + 1) * tq, :] = acc_ref[r].astype(BF16)


def _stick_breaking(q, k, v, nh):
    B, S, _ = q.shape
    tq = _tile(S, 256)
    tk = tq
    group = _tile(S // tk, 2)
    nsub = _tile(S // tq, 2)
    rows = tq * nsub
    return pl.pallas_call(
        functools.partial(_sb_kernel, tq=tq, tk=tk, group=group, nsub=nsub),
        grid=(B, nh, S // rows),
        in_specs=[
            pl.BlockSpec((None, rows, HEAD_DIM), lambda b, h, i: (b, i, h)),
            pl.BlockSpec((None, S, HEAD_DIM), lambda b, h, i: (b, 0, h)),
            pl.BlockSpec((None, S, HEAD_DIM), lambda b, h, i: (b, 0, h)),
        ],
        out_specs=pl.BlockSpec((None, rows, HEAD_DIM), lambda b, h, i: (b, i, h)),
        out_shape=jax.ShapeDtypeStruct(q.shape, BF16),
        scratch_shapes=[pltpu.VMEM((nsub, tq, HEAD_DIM), F32), pltpu.VMEM((nsub, tq, 1), F32),
                        pltpu.VMEM((1, 1), F32)],
        compiler_params=_params(("arbitrary", "arbitrary", "arbitrary")),
        name="stick_breaking",
    )(q, k, v)


def _split2(x):
    a = x.astype(BF16)
    return a, (x - a.astype(F32)).astype(BF16)


def _gla_kernel(q_ref, k_ref, la_ref, v_ref, r_ref, gn_ref, o_ref, s_ref, *, chunk, npair):
    n = pl.program_id(1)

    @pl.when(n == 0)
    def _():
        s_ref[...] = jnp.zeros_like(s_ref)

    C = chunk
    row = lax.broadcasted_iota(jnp.int32, (C, C), 0)
    col = lax.broadcasted_iota(jnp.int32, (C, C), 1)
    rcol = lax.broadcasted_iota(jnp.int32, (C, 1), 0)
    lane = lax.broadcasted_iota(jnp.int32, (C, LANES), 1)
    first = lane < GLA_DK
    sq_first = lax.broadcasted_iota(jnp.int32, (HEAD_DIM, LANES), 1) < GLA_DK
    zero = jnp.zeros((), BF16)

    def chunk_body(fine):
        q = q_ref[...]
        k = k_ref[...]
        la = la_ref[...]
        g_hi, g_lo = _split2(la)

        def rowsum(mask):
            m = mask.astype(BF16)
            return _dot(m, g_hi) + _dot(m, g_lo)

        def block_row(x, blk, at):
            x3 = x.reshape(C // blk, blk, x.shape[1])
            return jnp.broadcast_to(x3[:, at:at + 1, :], x3.shape).reshape(x.shape)

        b = rowsum(col <= row)
        b_last = b[C - 1:C, :]

        if fine:
            levels = [(q.astype(BF16), k.astype(BF16), row == col)]
            m = 1
        else:
            ref = block_row(b, DIAG_BLOCK, DIAG_BLOCK // 2 - 1)
            levels = [((q * jnp.exp(b - ref)).astype(BF16), (k * jnp.exp(ref - b)).astype(BF16),
                       jnp.logical_and(row // DIAG_BLOCK == col // DIAG_BLOCK, col <= row))]
            m = DIAG_BLOCK
        while m < C:
            if fine:
                ref = rowsum(col <= (row // (2 * m)) * (2 * m) + (m - 1))
            else:
                ref = block_row(b, 2 * m, m - 1)
            odd = (rcol // m) % 2 == 1
            qs = jnp.where(odd, q * jnp.exp(jnp.where(odd, b - ref, 0.0)), 0.0)
            ks = jnp.where(odd, 0.0, k * jnp.exp(jnp.where(odd, 0.0, ref - b)))
            levels.append((qs.astype(BF16), ks.astype(BF16), (row // (2 * m)) == (col // (2 * m))))
            m *= 2

        q_inter = (q * jnp.exp(b)).astype(BF16)
        k_state = (k * jnp.exp(b_last - b)).astype(BF16)

        for p in range(npair):
            ksl = slice(p * LANES, (p + 1) * LANES)
            state = s_ref[p]
            state_b = state.astype(BF16)
            scores = jnp.zeros((2 * C, C), F32)
            for qs, ks, mask in levels:
                qp = qs[:, ksl]
                q2 = jnp.concatenate([jnp.where(first, qp, zero), jnp.where(first, zero, qp)], axis=0)
                mask2 = jnp.concatenate([mask, mask], axis=0)
                scores += jnp.where(mask2, _dot_nt(q2, ks[:, ksl]), 0.0)
            qi = q_inter[:, ksl]
            inter = _dot_nt(jnp.concatenate([jnp.where(first, qi, zero), jnp.where(first, zero, qi)],
                                            axis=0), state_b)
            kv = []
            for e in range(2):
                hd = 2 * p + e
                vsl = slice(hd * HEAD_DIM, (hd + 1) * HEAD_DIM)
                rows = slice(e * C, (e + 1) * C)
                v = v_ref[:, vsl]
                o = _dot(scores[rows].astype(BF16), v) + inter[rows]
                o = _rms(o, gn_ref[...]) * r_ref[:, vsl].astype(F32)
                o_ref[:, vsl] = o.astype(BF16)
                kv.append(_dot_tn(v, k_state[:, ksl]))
            s_ref[p] = jnp.exp(b_last[:, ksl]) * state + jnp.where(sq_first, kv[0], kv[1])

    moderate = jnp.min(la_ref[...]) > -GLA_MODERATE_GATE

    @pl.when(moderate)
    def _():
        chunk_body(fine=False)

    @pl.when(jnp.logical_not(moderate))
    def _():
        chunk_body(fine=True)


def _gla(q, k, la, v, r, gn):
    B, S, kw = q.shape
    vw = v.shape[-1]
    chunk = _tile(S, 128)
    npair = kw // LANES
    return pl.pallas_call(
        functools.partial(_gla_kernel, chunk=chunk, npair=npair),
        grid=(B, S // chunk),
        in_specs=[
            pl.BlockSpec((None, chunk, kw), lambda b, n: (b, n, 0)),
            pl.BlockSpec((None, chunk, kw), lambda b, n: (b, n, 0)),
            pl.BlockSpec((None, chunk, kw), lambda b, n: (b, n, 0)),
            pl.BlockSpec((None, chunk, vw), lambda b, n: (b, n, 0)),
            pl.BlockSpec((None, chunk, vw), lambda b, n: (b, n, 0)),
            pl.BlockSpec((1, HEAD_DIM), lambda b, n: (0, 0)),
        ],
        out_specs=pl.BlockSpec((None, chunk, vw), lambda b, n: (b, n, 0)),
        out_shape=jax.ShapeDtypeStruct(v.shape, BF16),
        scratch_shapes=[pltpu.VMEM((npair, HEAD_DIM, LANES), F32)],
        compiler_params=_params(("arbitrary", "arbitrary")),
        name="gla",
    )(q, k, la, v, r, gn)


def _outproj_kernel(x_ref, a_ref, b_ref, wa_ref, wb_ref, o_ref):
    o_ref[...] = x_ref[...] + _dot(a_ref[...], wa_ref[...]) + _dot(b_ref[...], wb_ref[...])


def _out_proj(x2, a, b, w_out):
    T, D = x2.shape
    wa, wb = a.shape[1], b.shape[1]
    tm = _tile(T, 512)
    return pl.pallas_call(
        _outproj_kernel,
        grid=(T // tm,),
        in_specs=[
            _row_spec(tm, D), _row_spec(tm, wa), _row_spec(tm, wb),
            pl.BlockSpec((wa, D), lambda i: (0, 0)),
            pl.BlockSpec((wb, D), lambda i: (0, 0)),
        ],
        out_specs=_row_spec(tm, D),
        out_shape=jax.ShapeDtypeStruct((T, D), F32),
        compiler_params=_params(("parallel",)),
        name="mix_out_proj",
    )(x2, a, b, w_out[:wa], w_out[wa:])


def _memkv_kernel(m_ref, g_ref, w_ref, kg_ref, k_ref, v_ref, *, nh):
    h = _rms(m_ref[...], g_ref[...]).astype(BF16)
    acc = _dot(h, w_ref[...])
    w = nh * HEAD_DIM
    for hd in range(nh):
        sl = slice(hd * HEAD_DIM, (hd + 1) * HEAD_DIM)
        k_ref[:, sl] = _rms(acc[:, sl], kg_ref[...]).astype(BF16)
    v_ref[...] = acc[:, w:].astype(BF16)


def _mem_kv(mem2, g, w_kv, kg):
    R, D = mem2.shape
    w = w_kv.shape[1] // 2
    return pl.pallas_call(
        functools.partial(_memkv_kernel, nh=w // HEAD_DIM),
        grid=(1,),
        in_specs=[_full_spec(mem2.shape), _full_spec(g.shape), _full_spec(w_kv.shape),
                  _full_spec(kg.shape)],
        out_specs=[_full_spec((R, w)), _full_spec((R, w))],
        out_shape=[jax.ShapeDtypeStruct((R, w), BF16)] * 2,
        compiler_params=_params(("arbitrary",)),
        name="mem_kv",
    )(mem2, g, w_kv, kg)


def _xa_kernel(x_ref, g_ref, wq_ref, qg_ref, k_ref, v_ref, wo_ref, o_ref, *, nh):
    x = x_ref[...]
    h = _rms(x, g_ref[...]).astype(BF16)
    qf = _dot(h, wq_ref[...])
    scale = HEAD_DIM ** -0.5
    outs = []
    for hd in range(nh):
        sl = slice(hd * HEAD_DIM, (hd + 1) * HEAD_DIM)
        q = (_rms(qf[:, sl], qg_ref[...]) * scale).astype(BF16)
        s = _dot_nt(q, k_ref[:, sl])
        s = s - jnp.max(s, axis=-1, keepdims=True)
        e = jnp.exp(s)
        p = e / jnp.sum(e, axis=-1, keepdims=True)
        outs.append(_dot(p.astype(BF16), v_ref[:, sl]).astype(BF16))
    o = jnp.concatenate(outs, axis=-1)
    o_ref[...] = x + _dot(o, wo_ref[...])


def _cross_attention(x, g, w_q, qg, k, v, w_o):
    B, S, D = x.shape
    M, W = k.shape[1], k.shape[2]
    tm = _tile(S, 512)
    return pl.pallas_call(
        functools.partial(_xa_kernel, nh=W // HEAD_DIM),
        grid=(B, S // tm),
        in_specs=[
            pl.BlockSpec((None, tm, D), lambda b, i: (b, i, 0)),
            pl.BlockSpec((1, D), lambda b, i: (0, 0)),
            pl.BlockSpec((D, W), lambda b, i: (0, 0)),
            pl.BlockSpec((1, HEAD_DIM), lambda b, i: (0, 0)),
            pl.BlockSpec((None, M, W), lambda b, i: (b, 0, 0)),
            pl.BlockSpec((None, M, W), lambda b, i: (b, 0, 0)),
            pl.BlockSpec((W, D), lambda b, i: (0, 0)),
        ],
        out_specs=pl.BlockSpec((None, tm, D), lambda b, i: (b, i, 0)),
        out_shape=jax.ShapeDtypeStruct((B, S, D), F32),
        compiler_params=_params(("parallel", "parallel")),
        name="mem_cross_attention",
    )(x, g, w_q, qg, k, v, w_o)


def _row(v):
    return v.reshape(1, -1).astype(F32)


def _sb_gla_mixer(x, h2, w_in3, layer, sb_qg, sb_kg, w_gate, b_gate, gla_og, w_out3):
    B, S, D = x.shape
    T = B * S
    x2 = x.reshape(T, D)
    n_heads = D // HEAD_DIM
    n_sb = n_heads // 4
    n_gla = n_heads - n_sb
    sbw = n_sb * HEAD_DIM
    kw = n_gla * GLA_DK
    vw = n_gla * HEAD_DIM
    seg = 3 * sbw
    assert 2 * kw == seg and vw == seg
    w_lr = w_in3[layer][:, 4 * seg:4 * seg + GLA_GATE_RANK]
    w_lr_p = jnp.pad(w_lr, ((0, 0), (0, LANES - GLA_GATE_RANK))).astype(BF16)
    w_gate_p = jnp.pad(w_gate, ((0, LANES - GLA_GATE_RANK), (0, 0))).astype(BF16)

    q_sb, k_sb, v_sb = _proj_call(
        functools.partial(_proj_sb_kernel, nh=n_sb), h2,
        [_weight_bf16(w_in3, layer, 0, seg), _row(sb_qg), _row(sb_kg)],
        [sbw, sbw, sbw], [BF16] * 3, "proj_sb")
    q_g, k_g, log_a = _proj_call(
        functools.partial(_proj_gla_kernel, kw=kw), h2,
        [_weight_bf16(w_in3, layer, 1, seg), w_lr_p, w_gate_p, _row(b_gate)],
        [kw, kw, kw], [F32] * 3, "proj_gla_qk")
    (v_g,) = _proj_call(functools.partial(_proj_plain_kernel, silu=False), h2,
                        [_weight_bf16(w_in3, layer, 2, seg)], [vw], [BF16], "proj_gla_v")
    (r_g,) = _proj_call(functools.partial(_proj_plain_kernel, silu=True), h2,
                        [_weight_bf16(w_in3, layer, 3, seg)], [vw], [BF16], "proj_gla_r")

    o_sb = _stick_breaking(q_sb.reshape(B, S, sbw), k_sb.reshape(B, S, sbw),
                           v_sb.reshape(B, S, sbw), n_sb)
    o_g = _gla(q_g.reshape(B, S, kw), k_g.reshape(B, S, kw), log_a.reshape(B, S, kw),
               v_g.reshape(B, S, vw), r_g.reshape(B, S, vw), _row(gla_og))
    out = _out_proj(x2, o_sb.reshape(T, sbw), o_g.reshape(T, vw), _weight_bf16(w_out3, layer))
    return out.reshape(B, S, D)


def kernel(x, mem, ffn1_norm, ffn1_w_gu, ffn1_w_down, mix_norm, ab_w_in, sb_q_norm, sb_k_norm,
           gla_w_gate, gla_b_gate, gla_o_norm, ab_w_out, conv_w_in, conv_w, conv_w_out, xa_norm,
           mem_norm, xa_w_q, xa_w_kv, xa_q_norm, xa_k_norm, xa_w_o, ffn2_norm, ffn2_w_gu,
           ffn2_w_down):
    B, S, D = x.shape
    T = B * S
    depth = ffn1_norm.shape[0]
    mem2 = mem.reshape(-1, D)
    for layer in range(depth):
        i = layer // 2
        even = layer % 2 == 0
        ffn1 = _swiglu_ffn(x.reshape(T, D), _row(ffn1_norm[layer]), _weight_bf16(ffn1_w_gu, layer),
                           _weight_bf16(ffn1_w_down, layer), scale=0.5,
                           next_gain=_row(mix_norm[layer]) if even else None)
        if even:
            x = _sb_gla_mixer(ffn1[0].reshape(B, S, D), ffn1[1], ab_w_in, i, sb_q_norm[i],
                              sb_k_norm[i], gla_w_gate[i], gla_b_gate[i], gla_o_norm[i], ab_w_out)
        else:
            x = _conv_mixer(ffn1.reshape(B, S, D), _row(mix_norm[layer]), _weight_bf16(conv_w_in, i),
                            conv_w[i].astype(F32), _weight_bf16(conv_w_out, i))
        k_m, v_m = _mem_kv(mem2, _row(mem_norm[layer]), _weight_bf16(xa_w_kv, layer),
                           _row(xa_k_norm[layer]))
        xw = k_m.shape[1]
        x = _cross_attention(x, _row(xa_norm[layer]), _weight_bf16(xa_w_q, layer),
                             _row(xa_q_norm[layer]), k_m.reshape(B, -1, xw),
                             v_m.reshape(B, -1, xw), _weight_bf16(xa_w_o, layer))
        x = _swiglu_ffn(x.reshape(T, D), _row(ffn2_norm[layer]), _weight_bf16(ffn2_w_gu, layer),
                        _weight_bf16(ffn2_w_down, layer), scale=0.5).reshape(B, S, D)
    return x
```

```python
import functools

import jax
import jax.numpy as jnp
from jax import lax
from jax.experimental import pallas as pl
from jax.experimental.pallas import tpu as pltpu

F32 = jnp.float32
BF16 = jnp.bfloat16

EPS = 1e-6
HEAD_DIM = 128
GLA_DK = 64
GLA_GATE_RANK = 16
GLA_GATE_TAU = 16.0
LANES = 128
LOG2E = 1.4426950408889634
UNDERFLOW_LOG2 = 160.0
SCORE_BOUND_SLACK = 1.001
CONV_SPLIT = 2
DIAG_BLOCK = 16
GLA_MODERATE_GATE = 5.0
CAST_BLOCK_BYTES = 8 * 1024 * 1024
VMEM_LIMIT = 56 * 1024 * 1024

_NT = (((1,), (1,)), ((), ()))
_TN = (((0,), (0,)), ((), ()))


def _params(sem):
    return pltpu.CompilerParams(dimension_semantics=sem, vmem_limit_bytes=VMEM_LIMIT)


def _dot(a, b):
    return jnp.dot(a, b, preferred_element_type=F32)


def _dot_nt(a, b):
    return lax.dot_general(a, b, _NT, preferred_element_type=F32)


def _dot_tn(a, b):
    return lax.dot_general(a, b, _TN, preferred_element_type=F32)


def _rms(xf, g):
    ms = jnp.mean(xf * xf, axis=-1, keepdims=True)
    return xf * lax.rsqrt(ms + EPS) * g


def _sigmoid(x):
    return 1.0 / (1.0 + jnp.exp(-x))


def _softplus(x):
    return jnp.maximum(x, 0.0) + jnp.log(1.0 + jnp.exp(-jnp.abs(x)))


def _tile(n, pref):
    t = min(pref, n)
    while n % t:
        t //= 2
    return t


def _cast_kernel(w_ref, o_ref):
    o_ref[...] = w_ref[...].astype(BF16)


def _weight_bf16(w3, layer, col_block=0, ncols=None):
    _, K, N = w3.shape
    ncols = N if ncols is None else ncols
    limit = max(8, CAST_BLOCK_BYTES // (4 * ncols))
    rb = _tile(K, 1 << (limit.bit_length() - 1))
    return pl.pallas_call(
        _cast_kernel,
        grid=(K // rb,),
        in_specs=[pl.BlockSpec((None, rb, ncols), lambda r: (layer, r, col_block))],
        out_specs=pl.BlockSpec((rb, ncols), lambda r: (r, 0)),
        out_shape=jax.ShapeDtypeStruct((K, ncols), BF16),
        compiler_params=_params(("parallel",)),
        name="weight_to_bf16",
    )(w3)


def _glu_kernel(x_ref, g_ref, wa_ref, wb_ref, wd_ref, *rest, scale, nf, next_norm):
    if next_norm:
        g2_ref, o_ref, hn_ref, h_ref = rest
    else:
        o_ref, h_ref = rest
    i = pl.program_id(0)
    f = pl.program_id(1)
    slot = i % 2

    @pl.when(jnp.logical_and(i == 0, f == 0))
    def _():
        h_ref[0] = _rms(x_ref[...], g_ref[...]).astype(BF16)

    def down():
        h = h_ref[slot]
        a = _dot(h, wa_ref[...])
        b = _dot(h, wb_ref[...])
        return _dot((a * _sigmoid(a) * b).astype(BF16), wd_ref[...])

    @pl.when(f == 0)
    def _():
        o_ref[...] = x_ref[...] * (1.0 / scale) + down()

    @pl.when(jnp.logical_and(f > 0, f < nf - 1))
    def _():
        o_ref[...] += down()

    @pl.when(f == nf - 1)
    def _():
        out = scale * (o_ref[...] + down())
        o_ref[...] = out
        if next_norm:
            hn_ref[...] = _rms(out, g2_ref[...]).astype(BF16)
        h_ref[1 - slot] = _rms(x_ref[...], g_ref[...]).astype(BF16)


def _swiglu_ffn(x2, g, w_gu, w_d, *, scale, next_gain=None):
    T, D = x2.shape
    F = w_d.shape[0]
    tm = _tile(T, 512)
    tf = _tile(F, 1024)
    nf = F // tf
    nt = T // tm
    assert nf >= 2
    next_norm = next_gain is not None
    row = pl.BlockSpec((tm, D), lambda i, f: (i, 0))
    vec = pl.BlockSpec((1, D), lambda i, f: (0, 0))
    in_specs = [
        pl.BlockSpec((tm, D), lambda i, f: (jnp.minimum(i + jnp.minimum(f, 1), nt - 1), 0)),
        vec,
        pl.BlockSpec((D, tf), lambda i, f: (0, f)),
        pl.BlockSpec((D, tf), lambda i, f: (0, f + nf)),
        pl.BlockSpec((tf, D), lambda i, f: (f, 0)),
    ]
    args = [x2, g, w_gu, w_gu, w_d]
    out_specs = row
    out_shape = jax.ShapeDtypeStruct((T, D), F32)
    if next_norm:
        in_specs.append(vec)
        args.append(next_gain)
        out_specs = [row, row]
        out_shape = [out_shape, jax.ShapeDtypeStruct((T, D), BF16)]
    return pl.pallas_call(
        functools.partial(_glu_kernel, scale=scale, nf=nf, next_norm=next_norm),
        grid=(nt, nf),
        in_specs=in_specs,
        out_specs=out_specs,
        out_shape=out_shape,
        scratch_shapes=[pltpu.VMEM((2, tm, D), BF16)],
        compiler_params=_params(("arbitrary", "arbitrary")),
        name="swiglu_ffn",
    )(*args)


def _conv_kernel(x_ref, g_ref, wb_ref, wc_ref, wu_ref, cw_ref, wo_ref, o_ref, h_ref, halo_ref,
                 *, nc):
    i = pl.program_id(1)
    c = pl.program_id(2)
    t = pl.program_id(0) * pl.num_programs(1) + i
    slot = t % 2

    @pl.when(jnp.logical_and(t == 0, c == 0))
    def _():
        h_ref[0] = _rms(x_ref[...], g_ref[...]).astype(BF16)

    @pl.when(i == 0)
    def _():
        halo_ref[c] = jnp.zeros(halo_ref.shape[1:], F32)

    def mix():
        h = h_ref[slot]
        tm, tc = h.shape[0], wb_ref.shape[1]
        prev = halo_ref[c]
        row = lax.broadcasted_iota(jnp.int32, (tm, tc // CONV_SPLIT), 0)
        acc = None
        for s in range(CONV_SPLIT):
            sl = slice(s * (tc // CONV_SPLIT), (s + 1) * (tc // CONV_SPLIT))
            bg = _dot(h, wb_ref[:, sl])
            cu = _dot(h, wc_ref[:, sl]) * _dot(h, wu_ref[:, sl])
            p1 = prev[7:8, sl]
            p2 = prev[6:7, sl]
            r1 = jnp.where(row == 0, p1, pltpu.roll(cu, 1, 0))
            r2 = jnp.where(row == 0, p2, jnp.where(row == 1, p1, pltpu.roll(cu, 2, 0)))
            taps = cw_ref[:, pl.ds(pl.multiple_of(c * tc + sl.start, LANES), sl.stop - sl.start)]
            y = taps[0:1] * r2 + taps[1:2] * r1 + taps[2:3] * cu
            halo_ref[c, :, sl] = cu[tm - 8:, :]
            d = _dot((bg * y).astype(BF16), wo_ref[sl, :])
            acc = d if acc is None else acc + d
        return acc

    @pl.when(c == 0)
    def _():
        o_ref[...] = x_ref[...] + mix()

    @pl.when(jnp.logical_and(c > 0, c < nc - 1))
    def _():
        o_ref[...] += mix()

    @pl.when(c == nc - 1)
    def _():
        o_ref[...] += mix()
        h_ref[1 - slot] = _rms(x_ref[...], g_ref[...]).astype(BF16)


def _conv_mixer(x, g, w_in, conv_w, w_out):
    B, S, D = x.shape
    tm = _tile(S, 512)
    tc = _tile(D, 512)
    nc = D // tc
    ni = S // tm
    assert nc >= 2

    def x_map(b, i, c):
        t = jnp.minimum(b * ni + i + jnp.minimum(c, 1), B * ni - 1)
        return (t // ni, t % ni, 0)

    return pl.pallas_call(
        functools.partial(_conv_kernel, nc=nc),
        grid=(B, ni, nc),
        in_specs=[
            pl.BlockSpec((None, tm, D), x_map),
            pl.BlockSpec((1, D), lambda b, i, c: (0, 0)),
            pl.BlockSpec((D, tc), lambda b, i, c: (0, c)),
            pl.BlockSpec((D, tc), lambda b, i, c: (0, c + nc)),
            pl.BlockSpec((D, tc), lambda b, i, c: (0, c + 2 * nc)),
            pl.BlockSpec((3, D), lambda b, i, c: (0, 0)),
            pl.BlockSpec((tc, D), lambda b, i, c: (c, 0)),
        ],
        out_specs=pl.BlockSpec((None, tm, D), lambda b, i, c: (b, i, 0)),
        out_shape=jax.ShapeDtypeStruct((B, S, D), F32),
        scratch_shapes=[pltpu.VMEM((2, tm, D), BF16), pltpu.VMEM((nc, 8, tc), F32)],
        compiler_params=_params(("arbitrary", "arbitrary", "arbitrary")),
        name="conv_mixer",
    )(x, g, w_in, w_in, w_in, conv_w, w_out)


def _proj_sb_kernel(h_ref, w_ref, qg_ref, kg_ref, q_ref, k_ref, v_ref, *, nh):
    acc = _dot(h_ref[...], w_ref[...])
    w = nh * HEAD_DIM
    qscale = LOG2E * HEAD_DIM ** -0.5
    for hd in range(nh):
        lo = hd * HEAD_DIM
        q = acc[:, lo:lo + HEAD_DIM]
        k = acc[:, w + lo:w + lo + HEAD_DIM]
        q_ref[:, lo:lo + HEAD_DIM] = (_rms(q, qg_ref[...]) * qscale).astype(BF16)
        k_ref[:, lo:lo + HEAD_DIM] = _rms(k, kg_ref[...]).astype(BF16)
    v_ref[...] = acc[:, 2 * w:].astype(BF16)


def _proj_gla_kernel(h_ref, w_ref, wlr_ref, wg_ref, bg_ref, q_ref, k_ref, la_ref, *, kw):
    h = h_ref[...]
    acc = _dot(h, w_ref[...])
    q_ref[...] = acc[:, :kw] * (GLA_DK ** -0.5)
    k_ref[...] = acc[:, kw:]
    low_rank = _dot(h, wlr_ref[...])
    logit = _dot(low_rank.astype(BF16), wg_ref[...]) + bg_ref[...]
    la_ref[...] = -_softplus(-logit) * (1.0 / GLA_GATE_TAU)


def _proj_plain_kernel(h_ref, w_ref, o_ref, *, silu):
    acc = _dot(h_ref[...], w_ref[...])
    if silu:
        acc = acc * _sigmoid(acc)
    o_ref[...] = acc.astype(BF16)


def _row_spec(tm, n):
    return pl.BlockSpec((tm, n), lambda i: (i, 0))


def _full_spec(shape):
    return pl.BlockSpec(shape, lambda i: (0,) * len(shape))


def _proj_call(kernel, h2, extra_in, out_widths, out_dtypes, name):
    T, D = h2.shape
    tm = _tile(T, 512)
    in_specs = [_row_spec(tm, D)] + [_full_spec(a.shape) for a in extra_in]
    return pl.pallas_call(
        kernel,
        grid=(T // tm,),
        in_specs=in_specs,
        out_specs=[_row_spec(tm, n) for n in out_widths],
        out_shape=[jax.ShapeDtypeStruct((T, n), dt) for n, dt in zip(out_widths, out_dtypes)],
        compiler_params=_params(("parallel",)),
        name=name,
    )(h2, *extra_in)


def _sb_kernel(q_ref, k_ref, v_ref, o_ref, acc_ref, run_ref, kmax_ref, *, tq, tk, group, nsub):
    step = pl.program_id(2)
    span = tk * group
    n_keys = k_ref.shape[0]

    @pl.when(step == 0)
    def _():
        def body(n, m):
            kk = k_ref[pl.ds(pl.multiple_of(n * span, span), span), :].astype(F32)
            return jnp.maximum(m, jnp.max(jnp.sum(kk * kk, axis=1, keepdims=True), axis=0, keepdims=True))
        m = lax.fori_loop(0, n_keys // span, body, jnp.zeros((1, 1), F32))
        kmax_ref[...] = jnp.sqrt(m)

    acc_ref[...] = jnp.zeros_like(acc_ref)
    run_ref[...] = jnp.zeros_like(run_ref)
    row = lax.broadcasted_iota(jnp.int32, (tq, tk), 0)
    col = lax.broadcasted_iota(jnp.int32, (tq, tk), 1)
    krow = lax.broadcasted_iota(jnp.int32, (tk, tk), 0)
    kcol = lax.broadcasted_iota(jnp.int32, (tk, tk), 1)
    suffix = (krow >= kcol).astype(BF16)

    def key_tiles(r, i, q, first_tile, ntiles, masked):
        base = pl.multiple_of(first_tile * tk, tk)
        run = run_ref[r]
        ws = [None] * ntiles
        for c in reversed(range(ntiles)):
            start = pl.multiple_of(base + c * tk, tk)
            z = _dot_nt(q, k_ref[pl.ds(start, tk), :])
            sp = jnp.maximum(z, 0.0) + jnp.log(1.0 + jnp.exp2(-jnp.abs(z))) * LOG2E
            if masked:
                causal = (col + start) < (row + i * tq)
                sp = jnp.where(causal, sp, 0.0)
            within = _dot(sp.astype(BF16), suffix)
            w = jnp.exp2(z - within - run)
            if masked:
                w = jnp.where(causal, w, 0.0)
            ws[c] = w.astype(BF16)
            run = run + jnp.sum(sp, axis=1, keepdims=True)
        acc_ref[r] += _dot(jnp.concatenate(ws, axis=1), v_ref[pl.ds(base, ntiles * tk), :])
        run_ref[r] = run

    subs = []
    for r in range(nsub):
        i = step * nsub + r
        q = q_ref[r * tq:(r + 1) * tq, :]
        qf = q.astype(F32)
        zmax = jnp.sqrt(jnp.sum(qf * qf, axis=1, keepdims=True)) * kmax_ref[...] * SCORE_BOUND_SLACK
        before = jnp.maximum(i - (group - 1), 0)
        key_tiles(r, i, q, before, group, True)
        subs.append((i, q, zmax, before))

    for r, (i, q, zmax, before) in enumerate(subs):
        def any_weight_left(r=r, zmax=zmax):
            return jnp.max(zmax - run_ref[r]) > -UNDERFLOW_LOG2

        def sweep(count, first_of, ntiles, alive, r=r, i=i, q=q, any_weight_left=any_weight_left):
            def cond(carry):
                n, live = carry
                return jnp.logical_and(n < count, live)

            def body(carry):
                n, _ = carry
                key_tiles(r, i, q, first_of(n), ntiles, False)
                return n + 1, any_weight_left()

            return lax.while_loop(cond, body, (jnp.int32(0), alive))[1]

        alive = sweep(before // group, lambda n, before=before: before - group * (n + 1), group,
                      any_weight_left())
        sweep(before % group, lambda n, before=before: before % group - 1 - n, 1, alive)
        o_ref[r * tq:(r + 1) * tq, :] = acc_ref[r].astype(BF16)


def _stick_breaking(q, k, v, nh):
    B, S, _ = q.shape
    tq = _tile(S, 256)
    tk = tq
    group = _tile(S // tk, 2)
    nsub = _tile(S // tq, 2)
    rows = tq * nsub
    return pl.pallas_call(
        functools.partial(_sb_kernel, tq=tq, tk=tk, group=group, nsub=nsub),
        grid=(B, nh, S // rows),
        in_specs=[
            pl.BlockSpec((None, rows, HEAD_DIM), lambda b, h, i: (b, i, h)),
            pl.BlockSpec((None, S, HEAD_DIM), lambda b, h, i: (b, 0, h)),
            pl.BlockSpec((None, S, HEAD_DIM), lambda b, h, i: (b, 0, h)),
        ],
        out_specs=pl.BlockSpec((None, rows, HEAD_DIM), lambda b, h, i: (b, i, h)),
        out_shape=jax.ShapeDtypeStruct(q.shape, BF16),
        scratch_shapes=[pltpu.VMEM((nsub, tq, HEAD_DIM), F32), pltpu.VMEM((nsub, tq, 1), F32),
                        pltpu.VMEM((1, 1), F32)],
        compiler_params=_params(("arbitrary", "arbitrary", "arbitrary")),
        name="stick_breaking",
    )(q, k, v)


def _split2(x):
    a = x.astype(BF16)
    return a, (x - a.astype(F32)).astype(BF16)


def _gla_kernel(q_ref, k_ref, la_ref, v_ref, r_ref, gn_ref, o_ref, s_ref, *, chunk, npair):
    n = pl.program_id(1)

    @pl.when(n == 0)
    def _():
        s_ref[...] = jnp.zeros_like(s_ref)

    C = chunk
    row = lax.broadcasted_iota(jnp.int32, (C, C), 0)
    col = lax.broadcasted_iota(jnp.int32, (C, C), 1)
    rcol = lax.broadcasted_iota(jnp.int32, (C, 1), 0)
    lane = lax.broadcasted_iota(jnp.int32, (C, LANES), 1)
    first = lane < GLA_DK
    sq_first = lax.broadcasted_iota(jnp.int32, (HEAD_DIM, LANES), 1) < GLA_DK
    zero = jnp.zeros((), BF16)

    def chunk_body(fine):
        q = q_ref[...]
        k = k_ref[...]
        la = la_ref[...]
        g_hi, g_lo = _split2(la)

        def rowsum(mask):
            m = mask.astype(BF16)
            return _dot(m, g_hi) + _dot(m, g_lo)

        def block_row(x, blk, at):
            x3 = x.reshape(C // blk, blk, x.shape[1])
            return jnp.broadcast_to(x3[:, at:at + 1, :], x3.shape).reshape(x.shape)

        b = rowsum(col <= row)
        b_last = b[C - 1:C, :]

        if fine:
            levels = [(q.astype(BF16), k.astype(BF16), row == col)]
            m = 1
        else:
            ref = block_row(b, DIAG_BLOCK, DIAG_BLOCK // 2 - 1)
            levels = [((q * jnp.exp(b - ref)).astype(BF16), (k * jnp.exp(ref - b)).astype(BF16),
                       jnp.logical_and(row // DIAG_BLOCK == col // DIAG_BLOCK, col <= row))]
            m = DIAG_BLOCK
        while m < C:
            if fine:
                ref = rowsum(col <= (row // (2 * m)) * (2 * m) + (m - 1))
            else:
                ref = block_row(b, 2 * m, m - 1)
            odd = (rcol // m) % 2 == 1
            qs = jnp.where(odd, q * jnp.exp(jnp.where(odd, b - ref, 0.0)), 0.0)
            ks = jnp.where(odd, 0.0, k * jnp.exp(jnp.where(odd, 0.0, ref - b)))
            levels.append((qs.astype(BF16), ks.astype(BF16), (row // (2 * m)) == (col // (2 * m))))
            m *= 2

        q_inter = (q * jnp.exp(b)).astype(BF16)
        k_state = (k * jnp.exp(b_last - b)).astype(BF16)

        for p in range(npair):
            ksl = slice(p * LANES, (p + 1) * LANES)
            state = s_ref[p]
            state_b = state.astype(BF16)
            scores = jnp.zeros((2 * C, C), F32)
            for qs, ks, mask in levels:
                qp = qs[:, ksl]
                q2 = jnp.concatenate([jnp.where(first, qp, zero), jnp.where(first, zero, qp)], axis=0)
                mask2 = jnp.concatenate([mask, mask], axis=0)
                scores += jnp.where(mask2, _dot_nt(q2, ks[:, ksl]), 0.0)
            qi = q_inter[:, ksl]
            inter = _dot_nt(jnp.concatenate([jnp.where(first, qi, zero), jnp.where(first, zero, qi)],
                                            axis=0), state_b)
            kv = []
            for e in range(2):
                hd = 2 * p + e
                vsl = slice(hd * HEAD_DIM, (hd + 1) * HEAD_DIM)
                rows = slice(e * C, (e + 1) * C)
                v = v_ref[:, vsl]
                o = _dot(scores[rows].astype(BF16), v) + inter[rows]
                o = _rms(o, gn_ref[...]) * r_ref[:, vsl].astype(F32)
                o_ref[:, vsl] = o.astype(BF16)
                kv.append(_dot_tn(v, k_state[:, ksl]))
            s_ref[p] = jnp.exp(b_last[:, ksl]) * state + jnp.where(sq_first, kv[0], kv[1])

    moderate = jnp.min(la_ref[...]) > -GLA_MODERATE_GATE

    @pl.when(moderate)
    def _():
        chunk_body(fine=False)

    @pl.when(jnp.logical_not(moderate))
    def _():
        chunk_body(fine=True)


def _gla(q, k, la, v, r, gn):
    B, S, kw = q.shape
    vw = v.shape[-1]
    chunk = _tile(S, 128)
    npair = kw // LANES
    return pl.pallas_call(
        functools.partial(_gla_kernel, chunk=chunk, npair=npair),
        grid=(B, S // chunk),
        in_specs=[
            pl.BlockSpec((None, chunk, kw), lambda b, n: (b, n, 0)),
            pl.BlockSpec((None, chunk, kw), lambda b, n: (b, n, 0)),
            pl.BlockSpec((None, chunk, kw), lambda b, n: (b, n, 0)),
            pl.BlockSpec((None, chunk, vw), lambda b, n: (b, n, 0)),
            pl.BlockSpec((None, chunk, vw), lambda b, n: (b, n, 0)),
            pl.BlockSpec((1, HEAD_DIM), lambda b, n: (0, 0)),
        ],
        out_specs=pl.BlockSpec((None, chunk, vw), lambda b, n: (b, n, 0)),
        out_shape=jax.ShapeDtypeStruct(v.shape, BF16),
        scratch_shapes=[pltpu.VMEM((npair, HEAD_DIM, LANES), F32)],
        compiler_params=_params(("arbitrary", "arbitrary")),
        name="gla",
    )(q, k, la, v, r, gn)


def _outproj_kernel(x_ref, a_ref, b_ref, wa_ref, wb_ref, o_ref):
    o_ref[...] = x_ref[...] + _dot(a_ref[...], wa_ref[...]) + _dot(b_ref[...], wb_ref[...])


def _out_proj(x2, a, b, w_out):
    T, D = x2.shape
    wa, wb = a.shape[1], b.shape[1]
    tm = _tile(T, 512)
    return pl.pallas_call(
        _outproj_kernel,
        grid=(T // tm,),
        in_specs=[
            _row_spec(tm, D), _row_spec(tm, wa), _row_spec(tm, wb),
            pl.BlockSpec((wa, D), lambda i: (0, 0)),
            pl.BlockSpec((wb, D), lambda i: (0, 0)),
        ],
        out_specs=_row_spec(tm, D),
        out_shape=jax.ShapeDtypeStruct((T, D), F32),
        compiler_params=_params(("parallel",)),
        name="mix_out_proj",
    )(x2, a, b, w_out[:wa], w_out[wa:])


def _memkv_kernel(m_ref, g_ref, w_ref, kg_ref, k_ref, v_ref, *, nh):
    h = _rms(m_ref[...], g_ref[...]).astype(BF16)
    acc = _dot(h, w_ref[...])
    w = nh * HEAD_DIM
    for hd in range(nh):
        sl = slice(hd * HEAD_DIM, (hd + 1) * HEAD_DIM)
        k_ref[:, sl] = _rms(acc[:, sl], kg_ref[...]).astype(BF16)
    v_ref[...] = acc[:, w:].astype(BF16)


def _mem_kv(mem2, g, w_kv, kg):
    R, D = mem2.shape
    w = w_kv.shape[1] // 2
    return pl.pallas_call(
        functools.partial(_memkv_kernel, nh=w // HEAD_DIM),
        grid=(1,),
        in_specs=[_full_spec(mem2.shape), _full_spec(g.shape), _full_spec(w_kv.shape),
                  _full_spec(kg.shape)],
        out_specs=[_full_spec((R, w)), _full_spec((R, w))],
        out_shape=[jax.ShapeDtypeStruct((R, w), BF16)] * 2,
        compiler_params=_params(("arbitrary",)),
        name="mem_kv",
    )(mem2, g, w_kv, kg)


def _xa_kernel(x_ref, g_ref, wq_ref, qg_ref, k_ref, v_ref, wo_ref, o_ref, *, nh):
    x = x_ref[...]
    h = _rms(x, g_ref[...]).astype(BF16)
    qf = _dot(h, wq_ref[...])
    scale = HEAD_DIM ** -0.5
    outs = []
    for hd in range(nh):
        sl = slice(hd * HEAD_DIM, (hd + 1) * HEAD_DIM)
        q = (_rms(qf[:, sl], qg_ref[...]) * scale).astype(BF16)
        s = _dot_nt(q, k_ref[:, sl])
        s = s - jnp.max(s, axis=-1, keepdims=True)
        e = jnp.exp(s)
        p = e / jnp.sum(e, axis=-1, keepdims=True)
        outs.append(_dot(p.astype(BF16), v_ref[:, sl]).astype(BF16))
    o = jnp.concatenate(outs, axis=-1)
    o_ref[...] = x + _dot(o, wo_ref[...])


def _cross_attention(x, g, w_q, qg, k, v, w_o):
    B, S, D = x.shape
    M, W = k.shape[1], k.shape[2]
    tm = _tile(S, 1024)
    return pl.pallas_call(
        functools.partial(_xa_kernel, nh=W // HEAD_DIM),
        grid=(B, S // tm),
        in_specs=[
            pl.BlockSpec((None, tm, D), lambda b, i: (b, i, 0)),
            pl.BlockSpec((1, D), lambda b, i: (0, 0)),
            pl.BlockSpec((D, W), lambda b, i: (0, 0)),
            pl.BlockSpec((1, HEAD_DIM), lambda b, i: (0, 0)),
            pl.BlockSpec((None, M, W), lambda b, i: (b, 0, 0)),
            pl.BlockSpec((None, M, W), lambda b, i: (b, 0, 0)),
            pl.BlockSpec((W, D), lambda b, i: (0, 0)),
        ],
        out_specs=pl.BlockSpec((None, tm, D), lambda b, i: (b, i, 0)),
        out_shape=jax.ShapeDtypeStruct((B, S, D), F32),
        compiler_params=_params(("parallel", "parallel")),
        name="mem_cross_attention",
    )(x, g, w_q, qg, k, v, w_o)


def _row(v):
    return v.reshape(1, -1).astype(F32)


def _sb_gla_mixer(x, h2, w_in3, layer, sb_qg, sb_kg, w_gate, b_gate, gla_og, w_out3):
    B, S, D = x.shape
    T = B * S
    x2 = x.reshape(T, D)
    n_heads = D // HEAD_DIM
    n_sb = n_heads // 4
    n_gla = n_heads - n_sb
    sbw = n_sb * HEAD_DIM
    kw = n_gla * GLA_DK
    vw = n_gla * HEAD_DIM
    seg = 3 * sbw
    assert 2 * kw == seg and vw == seg
    w_lr = w_in3[layer][:, 4 * seg:4 * seg + GLA_GATE_RANK]
    w_lr_p = jnp.pad(w_lr, ((0, 0), (0, LANES - GLA_GATE_RANK))).astype(BF16)
    w_gate_p = jnp.pad(w_gate, ((0, LANES - GLA_GATE_RANK), (0, 0))).astype(BF16)

    q_sb, k_sb, v_sb = _proj_call(
        functools.partial(_proj_sb_kernel, nh=n_sb), h2,
        [_weight_bf16(w_in3, layer, 0, seg), _row(sb_qg), _row(sb_kg)],
        [sbw, sbw, sbw], [BF16] * 3, "proj_sb")
    q_g, k_g, log_a = _proj_call(
        functools.partial(_proj_gla_kernel, kw=kw), h2,
        [_weight_bf16(w_in3, layer, 1, seg), w_lr_p, w_gate_p, _row(b_gate)],
        [kw, kw, kw], [F32] * 3, "proj_gla_qk")
    (v_g,) = _proj_call(functools.partial(_proj_plain_kernel, silu=False), h2,
                        [_weight_bf16(w_in3, layer, 2, seg)], [vw], [BF16], "proj_gla_v")
    (r_g,) = _proj_call(functools.partial(_proj_plain_kernel, silu=True), h2,
                        [_weight_bf16(w_in3, layer, 3, seg)], [vw], [BF16], "proj_gla_r")

    o_sb = _stick_breaking(q_sb.reshape(B, S, sbw), k_sb.reshape(B, S, sbw),
                           v_sb.reshape(B, S, sbw), n_sb)
    o_g = _gla(q_g.reshape(B, S, kw), k_g.reshape(B, S, kw), log_a.reshape(B, S, kw),
               v_g.reshape(B, S, vw), r_g.reshape(B, S, vw), _row(gla_og))
    out = _out_proj(x2, o_sb.reshape(T, sbw), o_g.reshape(T, vw), _weight_bf16(w_out3, layer))
    return out.reshape(B, S, D)


def kernel(x, mem, ffn1_norm, ffn1_w_gu, ffn1_w_down, mix_norm, ab_w_in, sb_q_norm, sb_k_norm,
           gla_w_gate, gla_b_gate, gla_o_norm, ab_w_out, conv_w_in, conv_w, conv_w_out, xa_norm,
           mem_norm, xa_w_q, xa_w_kv, xa_q_norm, xa_k_norm, xa_w_o, ffn2_norm, ffn2_w_gu,
           ffn2_w_down):
    B, S, D = x.shape
    T = B * S
    depth = ffn1_norm.shape[0]
    mem2 = mem.reshape(-1, D)
    for layer in range(depth):
        i = layer // 2
        even = layer % 2 == 0
        ffn1 = _swiglu_ffn(x.reshape(T, D), _row(ffn1_norm[layer]), _weight_bf16(ffn1_w_gu, layer),
                           _weight_bf16(ffn1_w_down, layer), scale=0.5,
                           next_gain=_row(mix_norm[layer]) if even else None)
        if even:
            x = _sb_gla_mixer(ffn1[0].reshape(B, S, D), ffn1[1], ab_w_in, i, sb_q_norm[i],
                              sb_k_norm[i], gla_w_gate[i], gla_b_gate[i], gla_o_norm[i], ab_w_out)
        else:
            x = _conv_mixer(ffn1.reshape(B, S, D), _row(mix_norm[layer]), _weight_bf16(conv_w_in, i),
                            conv_w[i].astype(F32), _weight_bf16(conv_w_out, i))
        k_m, v_m = _mem_kv(mem2, _row(mem_norm[layer]), _weight_bf16(xa_w_kv, layer),
                           _row(xa_k_norm[layer]))
        xw = k_m.shape[1]
        x = _cross_attention(x, _row(xa_norm[layer]), _weight_bf16(xa_w_q, layer),
                             _row(xa_q_norm[layer]), k_m.reshape(B, -1, xw),
                             v_m.reshape(B, -1, xw), _weight_bf16(xa_w_o, layer))
        x = _swiglu_ffn(x.reshape(T, D), _row(ffn2_norm[layer]), _weight_bf16(ffn2_w_gu, layer),
                        _weight_bf16(ffn2_w_down, layer), scale=0.5).reshape(B, S, D)
    return x
```

```python
import functools

import jax
import jax.numpy as jnp
from jax import lax
from jax.experimental import pallas as pl
from jax.experimental.pallas import tpu as pltpu

F32 = jnp.float32
BF16 = jnp.bfloat16

EPS = 1e-6
HEAD_DIM = 128
GLA_DK = 64
GLA_GATE_RANK = 16
GLA_GATE_TAU = 16.0
LANES = 128
LOG2E = 1.4426950408889634
UNDERFLOW_LOG2 = 160.0
SCORE_BOUND_SLACK = 1.001
CONV_SPLIT = 2
DIAG_BLOCK = 8
GLA_MODERATE_GATE = 10.0
CAST_BLOCK_BYTES = 8 * 1024 * 1024
VMEM_LIMIT = 56 * 1024 * 1024

_NT = (((1,), (1,)), ((), ()))
_TN = (((0,), (0,)), ((), ()))


def _params(sem):
    return pltpu.CompilerParams(dimension_semantics=sem, vmem_limit_bytes=VMEM_LIMIT)


def _dot(a, b):
    return jnp.dot(a, b, preferred_element_type=F32)


def _dot_nt(a, b):
    return lax.dot_general(a, b, _NT, preferred_element_type=F32)


def _dot_tn(a, b):
    return lax.dot_general(a, b, _TN, preferred_element_type=F32)


def _rms(xf, g):
    ms = jnp.mean(xf * xf, axis=-1, keepdims=True)
    return xf * lax.rsqrt(ms + EPS) * g


def _sigmoid(x):
    return 1.0 / (1.0 + jnp.exp(-x))


def _softplus(x):
    return jnp.maximum(x, 0.0) + jnp.log(1.0 + jnp.exp(-jnp.abs(x)))


def _tile(n, pref):
    t = min(pref, n)
    while n % t:
        t //= 2
    return t


def _cast_kernel(w_ref, o_ref):
    o_ref[...] = w_ref[...].astype(BF16)


def _weight_bf16(w3, layer, col_block=0, ncols=None):
    _, K, N = w3.shape
    ncols = N if ncols is None else ncols
    limit = max(8, CAST_BLOCK_BYTES // (4 * ncols))
    rb = _tile(K, 1 << (limit.bit_length() - 1))
    return pl.pallas_call(
        _cast_kernel,
        grid=(K // rb,),
        in_specs=[pl.BlockSpec((None, rb, ncols), lambda r: (layer, r, col_block))],
        out_specs=pl.BlockSpec((rb, ncols), lambda r: (r, 0)),
        out_shape=jax.ShapeDtypeStruct((K, ncols), BF16),
        compiler_params=_params(("parallel",)),
        name="weight_to_bf16",
    )(w3)


def _glu_kernel(x_ref, g_ref, wa_ref, wb_ref, wd_ref, *rest, scale, nf, next_norm):
    if next_norm:
        g2_ref, o_ref, hn_ref, h_ref = rest
    else:
        o_ref, h_ref = rest
    i = pl.program_id(0)
    f = pl.program_id(1)
    slot = i % 2

    @pl.when(jnp.logical_and(i == 0, f == 0))
    def _():
        h_ref[0] = _rms(x_ref[...], g_ref[...]).astype(BF16)

    def down():
        h = h_ref[slot]
        a = _dot(h, wa_ref[...])
        b = _dot(h, wb_ref[...])
        return _dot((a * _sigmoid(a) * b).astype(BF16), wd_ref[...])

    @pl.when(f == 0)
    def _():
        o_ref[...] = x_ref[...] * (1.0 / scale) + down()

    @pl.when(jnp.logical_and(f > 0, f < nf - 1))
    def _():
        o_ref[...] += down()

    @pl.when(f == nf - 1)
    def _():
        out = scale * (o_ref[...] + down())
        o_ref[...] = out
        if next_norm:
            hn_ref[...] = _rms(out, g2_ref[...]).astype(BF16)
        h_ref[1 - slot] = _rms(x_ref[...], g_ref[...]).astype(BF16)


def _swiglu_ffn(x2, g, w_gu, w_d, *, scale, next_gain=None):
    T, D = x2.shape
    F = w_d.shape[0]
    tm = _tile(T, 512)
    tf = _tile(F, 1024)
    nf = F // tf
    nt = T // tm
    assert nf >= 2
    next_norm = next_gain is not None
    row = pl.BlockSpec((tm, D), lambda i, f: (i, 0))
    vec = pl.BlockSpec((1, D), lambda i, f: (0, 0))
    in_specs = [
        pl.BlockSpec((tm, D), lambda i, f: (jnp.minimum(i + jnp.minimum(f, 1), nt - 1), 0)),
        vec,
        pl.BlockSpec((D, tf), lambda i, f: (0, f)),
        pl.BlockSpec((D, tf), lambda i, f: (0, f + nf)),
        pl.BlockSpec((tf, D), lambda i, f: (f, 0)),
    ]
    args = [x2, g, w_gu, w_gu, w_d]
    out_specs = row
    out_shape = jax.ShapeDtypeStruct((T, D), F32)
    if next_norm:
        in_specs.append(vec)
        args.append(next_gain)
        out_specs = [row, row]
        out_shape = [out_shape, jax.ShapeDtypeStruct((T, D), BF16)]
    return pl.pallas_call(
        functools.partial(_glu_kernel, scale=scale, nf=nf, next_norm=next_norm),
        grid=(nt, nf),
        in_specs=in_specs,
        out_specs=out_specs,
        out_shape=out_shape,
        scratch_shapes=[pltpu.VMEM((2, tm, D), BF16)],
        compiler_params=_params(("arbitrary", "arbitrary")),
        name="swiglu_ffn",
    )(*args)


def _conv_kernel(x_ref, g_ref, wb_ref, wc_ref, wu_ref, cw_ref, wo_ref, o_ref, h_ref, halo_ref,
                 *, nc):
    i = pl.program_id(1)
    c = pl.program_id(2)
    t = pl.program_id(0) * pl.num_programs(1) + i
    slot = t % 2

    @pl.when(jnp.logical_and(t == 0, c == 0))
    def _():
        h_ref[0] = _rms(x_ref[...], g_ref[...]).astype(BF16)

    @pl.when(i == 0)
    def _():
        halo_ref[c] = jnp.zeros(halo_ref.shape[1:], F32)

    def mix():
        h = h_ref[slot]
        tm, tc = h.shape[0], wb_ref.shape[1]
        prev = halo_ref[c]
        row = lax.broadcasted_iota(jnp.int32, (tm, tc // CONV_SPLIT), 0)
        acc = None
        for s in range(CONV_SPLIT):
            sl = slice(s * (tc // CONV_SPLIT), (s + 1) * (tc // CONV_SPLIT))
            bg = _dot(h, wb_ref[:, sl])
            cu = _dot(h, wc_ref[:, sl]) * _dot(h, wu_ref[:, sl])
            p1 = prev[7:8, sl]
            p2 = prev[6:7, sl]
            r1 = jnp.where(row == 0, p1, pltpu.roll(cu, 1, 0))
            r2 = jnp.where(row == 0, p2, jnp.where(row == 1, p1, pltpu.roll(cu, 2, 0)))
            taps = cw_ref[:, pl.ds(pl.multiple_of(c * tc + sl.start, LANES), sl.stop - sl.start)]
            y = taps[0:1] * r2 + taps[1:2] * r1 + taps[2:3] * cu
            halo_ref[c, :, sl] = cu[tm - 8:, :]
            d = _dot((bg * y).astype(BF16), wo_ref[sl, :])
            acc = d if acc is None else acc + d
        return acc

    @pl.when(c == 0)
    def _():
        o_ref[...] = x_ref[...] + mix()

    @pl.when(jnp.logical_and(c > 0, c < nc - 1))
    def _():
        o_ref[...] += mix()

    @pl.when(c == nc - 1)
    def _():
        o_ref[...] += mix()
        h_ref[1 - slot] = _rms(x_ref[...], g_ref[...]).astype(BF16)


def _conv_mixer(x, g, w_in, conv_w, w_out):
    B, S, D = x.shape
    tm = _tile(S, 512)
    tc = _tile(D, 512)
    nc = D // tc
    ni = S // tm
    assert nc >= 2

    def x_map(b, i, c):
        t = jnp.minimum(b * ni + i + jnp.minimum(c, 1), B * ni - 1)
        return (t // ni, t % ni, 0)

    return pl.pallas_call(
        functools.partial(_conv_kernel, nc=nc),
        grid=(B, ni, nc),
        in_specs=[
            pl.BlockSpec((None, tm, D), x_map),
            pl.BlockSpec((1, D), lambda b, i, c: (0, 0)),
            pl.BlockSpec((D, tc), lambda b, i, c: (0, c)),
            pl.BlockSpec((D, tc), lambda b, i, c: (0, c + nc)),
            pl.BlockSpec((D, tc), lambda b, i, c: (0, c + 2 * nc)),
            pl.BlockSpec((3, D), lambda b, i, c: (0, 0)),
            pl.BlockSpec((tc, D), lambda b, i, c: (c, 0)),
        ],
        out_specs=pl.BlockSpec((None, tm, D), lambda b, i, c: (b, i, 0)),
        out_shape=jax.ShapeDtypeStruct((B, S, D), F32),
        scratch_shapes=[pltpu.VMEM((2, tm, D), BF16), pltpu.VMEM((nc, 8, tc), F32)],
        compiler_params=_params(("arbitrary", "arbitrary", "arbitrary")),
        name="conv_mixer",
    )(x, g, w_in, w_in, w_in, conv_w, w_out)


def _proj_sb_kernel(h_ref, w_ref, qg_ref, kg_ref, q_ref, k_ref, v_ref, *, nh):
    acc = _dot(h_ref[...], w_ref[...])
    w = nh * HEAD_DIM
    qscale = LOG2E * HEAD_DIM ** -0.5
    for hd in range(nh):
        lo = hd * HEAD_DIM
        q = acc[:, lo:lo + HEAD_DIM]
        k = acc[:, w + lo:w + lo + HEAD_DIM]
        q_ref[:, lo:lo + HEAD_DIM] = (_rms(q, qg_ref[...]) * qscale).astype(BF16)
        k_ref[:, lo:lo + HEAD_DIM] = _rms(k, kg_ref[...]).astype(BF16)
    v_ref[...] = acc[:, 2 * w:].astype(BF16)


def _proj_gla_kernel(h_ref, w_ref, wlr_ref, wg_ref, bg_ref, q_ref, k_ref, la_ref, *, kw):
    h = h_ref[...]
    acc = _dot(h, w_ref[...])
    q_ref[...] = acc[:, :kw] * (GLA_DK ** -0.5)
    k_ref[...] = acc[:, kw:]
    low_rank = _dot(h, wlr_ref[...])
    logit = _dot(low_rank.astype(BF16), wg_ref[...]) + bg_ref[...]
    la_ref[...] = -_softplus(-logit) * (1.0 / GLA_GATE_TAU)


def _proj_plain_kernel(h_ref, w_ref, o_ref, *, silu):
    acc = _dot(h_ref[...], w_ref[...])
    if silu:
        acc = acc * _sigmoid(acc)
    o_ref[...] = acc.astype(BF16)


def _row_spec(tm, n):
    return pl.BlockSpec((tm, n), lambda i: (i, 0))


def _full_spec(shape):
    return pl.BlockSpec(shape, lambda i: (0,) * len(shape))


def _proj_call(kernel, h2, extra_in, out_widths, out_dtypes, name):
    T, D = h2.shape
    tm = _tile(T, 512)
    in_specs = [_row_spec(tm, D)] + [_full_spec(a.shape) for a in extra_in]
    return pl.pallas_call(
        kernel,
        grid=(T // tm,),
        in_specs=in_specs,
        out_specs=[_row_spec(tm, n) for n in out_widths],
        out_shape=[jax.ShapeDtypeStruct((T, n), dt) for n, dt in zip(out_widths, out_dtypes)],
        compiler_params=_params(("parallel",)),
        name=name,
    )(h2, *extra_in)


def _sb_kernel(q_ref, k_ref, v_ref, o_ref, acc_ref, run_ref, kmax_ref, *, tq, tk, group, nsub):
    step = pl.program_id(2)
    span = tk * group
    n_keys = k_ref.shape[0]

    @pl.when(step == 0)
    def _():
        def body(n, m):
            kk = k_ref[pl.ds(pl.multiple_of(n * span, span), span), :].astype(F32)
            return jnp.maximum(m, jnp.max(jnp.sum(kk * kk, axis=1, keepdims=True), axis=0, keepdims=True))
        m = lax.fori_loop(0, n_keys // span, body, jnp.zeros((1, 1), F32))
        kmax_ref[...] = jnp.sqrt(m)

    acc_ref[...] = jnp.zeros_like(acc_ref)
    run_ref[...] = jnp.zeros_like(run_ref)
    row = lax.broadcasted_iota(jnp.int32, (tq, tk), 0)
    col = lax.broadcasted_iota(jnp.int32, (tq, tk), 1)
    krow = lax.broadcasted_iota(jnp.int32, (tk, tk), 0)
    kcol = lax.broadcasted_iota(jnp.int32, (tk, tk), 1)
    suffix = (krow >= kcol).astype(BF16)

    def key_tiles(r, i, q, first_tile, ntiles, masked):
        base = pl.multiple_of(first_tile * tk, tk)
        run = run_ref[r]
        ws = [None] * ntiles
        for c in reversed(range(ntiles)):
            start = pl.multiple_of(base + c * tk, tk)
            z = _dot_nt(q, k_ref[pl.ds(start, tk), :])
            sp = jnp.maximum(z, 0.0) + jnp.log(1.0 + jnp.exp2(-jnp.abs(z))) * LOG2E
            if masked:
                causal = (col + start) < (row + i * tq)
                sp = jnp.where(causal, sp, 0.0)
            within = _dot(sp.astype(BF16), suffix)
            w = jnp.exp2(z - within - run)
            if masked:
                w = jnp.where(causal, w, 0.0)
            ws[c] = w.astype(BF16)
            run = run + jnp.sum(sp, axis=1, keepdims=True)
        acc_ref[r] += _dot(jnp.concatenate(ws, axis=1), v_ref[pl.ds(base, ntiles * tk), :])
        run_ref[r] = run

    subs = []
    for r in range(nsub):
        i = step * nsub + r
        q = q_ref[r * tq:(r + 1) * tq, :]
        qf = q.astype(F32)
        zmax = jnp.sqrt(jnp.sum(qf * qf, axis=1, keepdims=True)) * kmax_ref[...] * SCORE_BOUND_SLACK
        before = jnp.maximum(i - (group - 1), 0)
        key_tiles(r, i, q, before, group, True)
        subs.append((i, q, zmax, before))

    for r, (i, q, zmax, before) in enumerate(subs):
        def any_weight_left(r=r, zmax=zmax):
            return jnp.max(zmax - run_ref[r]) > -UNDERFLOW_LOG2

        def sweep(count, first_of, ntiles, alive, r=r, i=i, q=q, any_weight_left=any_weight_left):
            def cond(carry):
                n, live = carry
                return jnp.logical_and(n < count, live)

            def body(carry):
                n, _ = carry
                key_tiles(r, i, q, first_of(n), ntiles, False)
                return n + 1, any_weight_left()

            return lax.while_loop(cond, body, (jnp.int32(0), alive))[1]

        alive = sweep(before // group, lambda n, before=before: before - group * (n + 1), group,
                      any_weight_left())
        sweep(before % group, lambda n, before=before: before % group - 1 - n, 1, alive)
        o_ref[r * tq:(r + 1) * tq, :] = acc_ref[r].astype(BF16)


def _stick_breaking(q, k, v, nh):
    B, S, _ = q.shape
    tq = _tile(S, 256)
    tk = tq
    group = _tile(S // tk, 2)
    nsub = _tile(S // tq, 2)
    rows = tq * nsub
    return pl.pallas_call(
        functools.partial(_sb_kernel, tq=tq, tk=tk, group=group, nsub=nsub),
        grid=(B, nh, S // rows),
        in_specs=[
            pl.BlockSpec((None, rows, HEAD_DIM), lambda b, h, i: (b, i, h)),
            pl.BlockSpec((None, S, HEAD_DIM), lambda b, h, i: (b, 0, h)),
            pl.BlockSpec((None, S, HEAD_DIM), lambda b, h, i: (b, 0, h)),
        ],
        out_specs=pl.BlockSpec((None, rows, HEAD_DIM), lambda b, h, i: (b, i, h)),
        out_shape=jax.ShapeDtypeStruct(q.shape, BF16),
        scratch_shapes=[pltpu.VMEM((nsub, tq, HEAD_DIM), F32), pltpu.VMEM((nsub, tq, 1), F32),
                        pltpu.VMEM((1, 1), F32)],
        compiler_params=_params(("arbitrary", "arbitrary", "arbitrary")),
        name="stick_breaking",
    )(q, k, v)


def _split2(x):
    a = x.astype(BF16)
    return a, (x - a.astype(F32)).astype(BF16)


def _gla_kernel(q_ref, k_ref, la_ref, v_ref, r_ref, gn_ref, o_ref, s_ref, *, chunk, npair):
    n = pl.program_id(1)

    @pl.when(n == 0)
    def _():
        s_ref[...] = jnp.zeros_like(s_ref)

    C = chunk
    row = lax.broadcasted_iota(jnp.int32, (C, C), 0)
    col = lax.broadcasted_iota(jnp.int32, (C, C), 1)
    rcol = lax.broadcasted_iota(jnp.int32, (C, 1), 0)
    lane = lax.broadcasted_iota(jnp.int32, (C, LANES), 1)
    first = lane < GLA_DK
    sq_first = lax.broadcasted_iota(jnp.int32, (HEAD_DIM, LANES), 1) < GLA_DK
    zero = jnp.zeros((), BF16)

    def chunk_body(fine):
        q = q_ref[...]
        k = k_ref[...]
        la = la_ref[...]
        g_hi, g_lo = _split2(la)

        def rowsum(mask):
            m = mask.astype(BF16)
            return _dot(m, g_hi) + _dot(m, g_lo)

        def block_row(x, blk, at):
            x3 = x.reshape(C // blk, blk, x.shape[1])
            return jnp.broadcast_to(x3[:, at:at + 1, :], x3.shape).reshape(x.shape)

        b = rowsum(col <= row)
        b_last = b[C - 1:C, :]

        if fine:
            levels = [(q.astype(BF16), k.astype(BF16), row == col)]
            m = 1
        else:
            ref = block_row(b, DIAG_BLOCK, DIAG_BLOCK // 2 - 1)
            levels = [((q * jnp.exp(b - ref)).astype(BF16), (k * jnp.exp(ref - b)).astype(BF16),
                       jnp.logical_and(row // DIAG_BLOCK == col // DIAG_BLOCK, col <= row))]
            m = DIAG_BLOCK
        while m < C:
            if fine:
                ref = rowsum(col <= (row // (2 * m)) * (2 * m) + (m - 1))
            else:
                ref = block_row(b, 2 * m, m - 1)
            odd = (rcol // m) % 2 == 1
            qs = jnp.where(odd, q * jnp.exp(jnp.where(odd, b - ref, 0.0)), 0.0)
            ks = jnp.where(odd, 0.0, k * jnp.exp(jnp.where(odd, 0.0, ref - b)))
            levels.append((qs.astype(BF16), ks.astype(BF16), (row // (2 * m)) == (col // (2 * m))))
            m *= 2

        q_inter = (q * jnp.exp(b)).astype(BF16)
        k_state = (k * jnp.exp(b_last - b)).astype(BF16)

        for p in range(npair):
            ksl = slice(p * LANES, (p + 1) * LANES)
            state = s_ref[p]
            state_b = state.astype(BF16)
            scores = jnp.zeros((2 * C, C), F32)
            for qs, ks, mask in levels:
                qp = qs[:, ksl]
                q2 = jnp.concatenate([jnp.where(first, qp, zero), jnp.where(first, zero, qp)], axis=0)
                mask2 = jnp.concatenate([mask, mask], axis=0)
                scores += jnp.where(mask2, _dot_nt(q2, ks[:, ksl]), 0.0)
            qi = q_inter[:, ksl]
            inter = _dot_nt(jnp.concatenate([jnp.where(first, qi, zero), jnp.where(first, zero, qi)],
                                            axis=0), state_b)
            kv = []
            for e in range(2):
                hd = 2 * p + e
                vsl = slice(hd * HEAD_DIM, (hd + 1) * HEAD_DIM)
                rows = slice(e * C, (e + 1) * C)
                v = v_ref[:, vsl]
                o = _dot(scores[rows].astype(BF16), v) + inter[rows]
                o = _rms(o, gn_ref[...]) * r_ref[:, vsl].astype(F32)
                o_ref[:, vsl] = o.astype(BF16)
                kv.append(_dot_tn(v, k_state[:, ksl]))
            s_ref[p] = jnp.exp(b_last[:, ksl]) * state + jnp.where(sq_first, kv[0], kv[1])

    moderate = jnp.min(la_ref[...]) > -GLA_MODERATE_GATE

    @pl.when(moderate)
    def _():
        chunk_body(fine=False)

    @pl.when(jnp.logical_not(moderate))
    def _():
        chunk_body(fine=True)


def _gla(q, k, la, v, r, gn):
    B, S, kw = q.shape
    vw = v.shape[-1]
    chunk = _tile(S, 128)
    npair = kw // LANES
    return pl.pallas_call(
        functools.partial(_gla_kernel, chunk=chunk, npair=npair),
        grid=(B, S // chunk),
        in_specs=[
            pl.BlockSpec((None, chunk, kw), lambda b, n: (b, n, 0)),
            pl.BlockSpec((None, chunk, kw), lambda b, n: (b, n, 0)),
            pl.BlockSpec((None, chunk, kw), lambda b, n: (b, n, 0)),
            pl.BlockSpec((None, chunk, vw), lambda b, n: (b, n, 0)),
            pl.BlockSpec((None, chunk, vw), lambda b, n: (b, n, 0)),
            pl.BlockSpec((1, HEAD_DIM), lambda b, n: (0, 0)),
        ],
        out_specs=pl.BlockSpec((None, chunk, vw), lambda b, n: (b, n, 0)),
        out_shape=jax.ShapeDtypeStruct(v.shape, BF16),
        scratch_shapes=[pltpu.VMEM((npair, HEAD_DIM, LANES), F32)],
        compiler_params=_params(("arbitrary", "arbitrary")),
        name="gla",
    )(q, k, la, v, r, gn)


def _outproj_kernel(x_ref, a_ref, b_ref, wa_ref, wb_ref, o_ref):
    o_ref[...] = x_ref[...] + _dot(a_ref[...], wa_ref[...]) + _dot(b_ref[...], wb_ref[...])


def _out_proj(x2, a, b, w_out):
    T, D = x2.shape
    wa, wb = a.shape[1], b.shape[1]
    tm = _tile(T, 512)
    return pl.pallas_call(
        _outproj_kernel,
        grid=(T // tm,),
        in_specs=[
            _row_spec(tm, D), _row_spec(tm, wa), _row_spec(tm, wb),
            pl.BlockSpec((wa, D), lambda i: (0, 0)),
            pl.BlockSpec((wb, D), lambda i: (0, 0)),
        ],
        out_specs=_row_spec(tm, D),
        out_shape=jax.ShapeDtypeStruct((T, D), F32),
        compiler_params=_params(("parallel",)),
        name="mix_out_proj",
    )(x2, a, b, w_out[:wa], w_out[wa:])


def _memkv_kernel(m_ref, g_ref, w_ref, kg_ref, k_ref, v_ref, *, nh):
    h = _rms(m_ref[...], g_ref[...]).astype(BF16)
    acc = _dot(h, w_ref[...])
    w = nh * HEAD_DIM
    for hd in range(nh):
        sl = slice(hd * HEAD_DIM, (hd + 1) * HEAD_DIM)
        k_ref[:, sl] = _rms(acc[:, sl], kg_ref[...]).astype(BF16)
    v_ref[...] = acc[:, w:].astype(BF16)


def _mem_kv(mem2, g, w_kv, kg):
    R, D = mem2.shape
    w = w_kv.shape[1] // 2
    return pl.pallas_call(
        functools.partial(_memkv_kernel, nh=w // HEAD_DIM),
        grid=(1,),
        in_specs=[_full_spec(mem2.shape), _full_spec(g.shape), _full_spec(w_kv.shape),
                  _full_spec(kg.shape)],
        out_specs=[_full_spec((R, w)), _full_spec((R, w))],
        out_shape=[jax.ShapeDtypeStruct((R, w), BF16)] * 2,
        compiler_params=_params(("arbitrary",)),
        name="mem_kv",
    )(mem2, g, w_kv, kg)


def _xa_kernel(x_ref, g_ref, wq_ref, qg_ref, k_ref, v_ref, wo_ref, o_ref, *, nh):
    x = x_ref[...]
    h = _rms(x, g_ref[...]).astype(BF16)
    qf = _dot(h, wq_ref[...])
    scale = HEAD_DIM ** -0.5
    outs = []
    for hd in range(nh):
        sl = slice(hd * HEAD_DIM, (hd + 1) * HEAD_DIM)
        q = (_rms(qf[:, sl], qg_ref[...]) * scale).astype(BF16)
        s = _dot_nt(q, k_ref[:, sl])
        s = s - jnp.max(s, axis=-1, keepdims=True)
        e = jnp.exp(s)
        p = e / jnp.sum(e, axis=-1, keepdims=True)
        outs.append(_dot(p.astype(BF16), v_ref[:, sl]).astype(BF16))
    o = jnp.concatenate(outs, axis=-1)
    o_ref[...] = x + _dot(o, wo_ref[...])


def _cross_attention(x, g, w_q, qg, k, v, w_o):
    B, S, D = x.shape
    M, W = k.shape[1], k.shape[2]
    tm = _tile(S, 1024)
    return pl.pallas_call(
        functools.partial(_xa_kernel, nh=W // HEAD_DIM),
        grid=(B, S // tm),
        in_specs=[
            pl.BlockSpec((None, tm, D), lambda b, i: (b, i, 0)),
            pl.BlockSpec((1, D), lambda b, i: (0, 0)),
            pl.BlockSpec((D, W), lambda b, i: (0, 0)),
            pl.BlockSpec((1, HEAD_DIM), lambda b, i: (0, 0)),
            pl.BlockSpec((None, M, W), lambda b, i: (b, 0, 0)),
            pl.BlockSpec((None, M, W), lambda b, i: (b, 0, 0)),
            pl.BlockSpec((W, D), lambda b, i: (0, 0)),
        ],
        out_specs=pl.BlockSpec((None, tm, D), lambda b, i: (b, i, 0)),
        out_shape=jax.ShapeDtypeStruct((B, S, D), F32),
        compiler_params=_params(("parallel", "parallel")),
        name="mem_cross_attention",
    )(x, g, w_q, qg, k, v, w_o)


def _row(v):
    return v.reshape(1, -1).astype(F32)


def _sb_gla_mixer(x, h2, w_in3, layer, sb_qg, sb_kg, w_gate, b_gate, gla_og, w_out3):
    B, S, D = x.shape
    T = B * S
    x2 = x.reshape(T, D)
    n_heads = D // HEAD_DIM
    n_sb = n_heads // 4
    n_gla = n_heads - n_sb
    sbw = n_sb * HEAD_DIM
    kw = n_gla * GLA_DK
    vw = n_gla * HEAD_DIM
    seg = 3 * sbw
    assert 2 * kw == seg and vw == seg
    w_lr = w_in3[layer][:, 4 * seg:4 * seg + GLA_GATE_RANK]
    w_lr_p = jnp.pad(w_lr, ((0, 0), (0, LANES - GLA_GATE_RANK))).astype(BF16)
    w_gate_p = jnp.pad(w_gate, ((0, LANES - GLA_GATE_RANK), (0, 0))).astype(BF16)

    q_sb, k_sb, v_sb = _proj_call(
        functools.partial(_proj_sb_kernel, nh=n_sb), h2,
        [_weight_bf16(w_in3, layer, 0, seg), _row(sb_qg), _row(sb_kg)],
        [sbw, sbw, sbw], [BF16] * 3, "proj_sb")
    q_g, k_g, log_a = _proj_call(
        functools.partial(_proj_gla_kernel, kw=kw), h2,
        [_weight_bf16(w_in3, layer, 1, seg), w_lr_p, w_gate_p, _row(b_gate)],
        [kw, kw, kw], [F32] * 3, "proj_gla_qk")
    (v_g,) = _proj_call(functools.partial(_proj_plain_kernel, silu=False), h2,
                        [_weight_bf16(w_in3, layer, 2, seg)], [vw], [BF16], "proj_gla_v")
    (r_g,) = _proj_call(functools.partial(_proj_plain_kernel, silu=True), h2,
                        [_weight_bf16(w_in3, layer, 3, seg)], [vw], [BF16], "proj_gla_r")

    o_sb = _stick_breaking(q_sb.reshape(B, S, sbw), k_sb.reshape(B, S, sbw),
                           v_sb.reshape(B, S, sbw), n_sb)
    o_g = _gla(q_g.reshape(B, S, kw), k_g.reshape(B, S, kw), log_a.reshape(B, S, kw),
               v_g.reshape(B, S, vw), r_g.reshape(B, S, vw), _row(gla_og))
    out = _out_proj(x2, o_sb.reshape(T, sbw), o_g.reshape(T, vw), _weight_bf16(w_out3, layer))
    return out.reshape(B, S, D)


def kernel(x, mem, ffn1_norm, ffn1_w_gu, ffn1_w_down, mix_norm, ab_w_in, sb_q_norm, sb_k_norm,
           gla_w_gate, gla_b_gate, gla_o_norm, ab_w_out, conv_w_in, conv_w, conv_w_out, xa_norm,
           mem_norm, xa_w_q, xa_w_kv, xa_q_norm, xa_k_norm, xa_w_o, ffn2_norm, ffn2_w_gu,
           ffn2_w_down):
    B, S, D = x.shape
    T = B * S
    depth = ffn1_norm.shape[0]
    mem2 = mem.reshape(-1, D)
    for layer in range(depth):
        i = layer // 2
        even = layer % 2 == 0
        ffn1 = _swiglu_ffn(x.reshape(T, D), _row(ffn1_norm[layer]), _weight_bf16(ffn1_w_gu, layer),
                           _weight_bf16(ffn1_w_down, layer), scale=0.5,
                           next_gain=_row(mix_norm[layer]) if even else None)
        if even:
            x = _sb_gla_mixer(ffn1[0].reshape(B, S, D), ffn1[1], ab_w_in, i, sb_q_norm[i],
                              sb_k_norm[i], gla_w_gate[i], gla_b_gate[i], gla_o_norm[i], ab_w_out)
        else:
            x = _conv_mixer(ffn1.reshape(B, S, D), _row(mix_norm[layer]), _weight_bf16(conv_w_in, i),
                            conv_w[i].astype(F32), _weight_bf16(conv_w_out, i))
        k_m, v_m = _mem_kv(mem2, _row(mem_norm[layer]), _weight_bf16(xa_w_kv, layer),
                           _row(xa_k_norm[layer]))
        xw = k_m.shape[1]
        x = _cross_attention(x, _row(xa_norm[layer]), _weight_bf16(xa_w_q, layer),
                             _row(xa_q_norm[layer]), k_m.reshape(B, -1, xw),
                             v_m.reshape(B, -1, xw), _weight_bf16(xa_w_o, layer))
        x = _swiglu_ffn(x.reshape(T, D), _row(ffn2_norm[layer]), _weight_bf16(ffn2_w_gu, layer),
                        _weight_bf16(ffn2_w_down, layer), scale=0.5).reshape(B, S, D)
    return x
```

```python
import functools

import jax
import jax.numpy as jnp
from jax import lax
from jax.experimental import pallas as pl
from jax.experimental.pallas import tpu as pltpu

F32 = jnp.float32
BF16 = jnp.bfloat16

EPS = 1e-6
HEAD_DIM = 128
GLA_DK = 64
GLA_GATE_RANK = 16
GLA_GATE_TAU = 16.0
LANES = 128
LOG2E = 1.4426950408889634
UNDERFLOW_LOG2 = 160.0
SCORE_BOUND_SLACK = 1.001
CONV_SPLIT = 2
DIAG_BLOCK = 8
GLA_MODERATE_GATE = 10.0
CAST_BLOCK_BYTES = 8 * 1024 * 1024
TOKEN_TILE = 512
FFN_HIDDEN_TILE = 1024
CONV_CHANNEL_TILE = 512
XA_TOKEN_TILE = 1024
SB_TILE = 256
SB_SPAN_TILES = 2
SB_TILES_PER_STEP = 2
GLA_CHUNK = 128
GLA_CHUNKS_PER_STEP = 2
VMEM_LIMIT = 56 * 1024 * 1024

_NT = (((1,), (1,)), ((), ()))
_TN = (((0,), (0,)), ((), ()))


def _params(sem):
    return pltpu.CompilerParams(dimension_semantics=sem, vmem_limit_bytes=VMEM_LIMIT)


def _dot(a, b):
    return jnp.dot(a, b, preferred_element_type=F32)


def _dot_nt(a, b):
    return lax.dot_general(a, b, _NT, preferred_element_type=F32)


def _dot_tn(a, b):
    return lax.dot_general(a, b, _TN, preferred_element_type=F32)


def _rms(xf, g):
    ms = jnp.mean(xf * xf, axis=-1, keepdims=True)
    return xf * lax.rsqrt(ms + EPS) * g


def _sigmoid(x):
    return 1.0 / (1.0 + jnp.exp(-x))


def _softplus(x):
    return jnp.maximum(x, 0.0) + jnp.log(1.0 + jnp.exp(-jnp.abs(x)))


def _tile(n, pref):
    t = min(pref, n)
    while n % t:
        t //= 2
    return t


def _cast_kernel(w_ref, o_ref):
    o_ref[...] = w_ref[...].astype(BF16)


def _weight_bf16(w3, layer, col_block=0, ncols=None):
    _, K, N = w3.shape
    ncols = N if ncols is None else ncols
    limit = max(8, CAST_BLOCK_BYTES // (4 * ncols))
    rb = _tile(K, 1 << (limit.bit_length() - 1))
    return pl.pallas_call(
        _cast_kernel,
        grid=(K // rb,),
        in_specs=[pl.BlockSpec((None, rb, ncols), lambda r: (layer, r, col_block))],
        out_specs=pl.BlockSpec((rb, ncols), lambda r: (r, 0)),
        out_shape=jax.ShapeDtypeStruct((K, ncols), BF16),
        compiler_params=_params(("parallel",)),
        name="weight_to_bf16",
    )(w3)


def _glu_kernel(x_ref, g_ref, wa_ref, wb_ref, wd_ref, *rest, scale, nf, next_norm):
    if next_norm:
        g2_ref, o_ref, hn_ref, h_ref = rest
    else:
        o_ref, h_ref = rest
    i = pl.program_id(0)
    f = pl.program_id(1)
    slot = i % 2

    @pl.when(jnp.logical_and(i == 0, f == 0))
    def _():
        h_ref[0] = _rms(x_ref[...], g_ref[...]).astype(BF16)

    def down():
        h = h_ref[slot]
        a = _dot(h, wa_ref[...])
        b = _dot(h, wb_ref[...])
        return _dot((a * _sigmoid(a) * b).astype(BF16), wd_ref[...])

    @pl.when(f == 0)
    def _():
        o_ref[...] = x_ref[...] * (1.0 / scale) + down()

    @pl.when(jnp.logical_and(f > 0, f < nf - 1))
    def _():
        o_ref[...] += down()

    @pl.when(f == nf - 1)
    def _():
        out = scale * (o_ref[...] + down())
        o_ref[...] = out
        if next_norm:
            hn_ref[...] = _rms(out, g2_ref[...]).astype(BF16)
        h_ref[1 - slot] = _rms(x_ref[...], g_ref[...]).astype(BF16)


def _swiglu_ffn(x2, g, w_gu, w_d, *, scale, next_gain=None):
    T, D = x2.shape
    F = w_d.shape[0]
    tm = _tile(T, TOKEN_TILE)
    tf = _tile(F, FFN_HIDDEN_TILE)
    nf = F // tf
    nt = T // tm
    assert nf >= 2
    next_norm = next_gain is not None
    row = pl.BlockSpec((tm, D), lambda i, f: (i, 0))
    vec = pl.BlockSpec((1, D), lambda i, f: (0, 0))
    in_specs = [
        pl.BlockSpec((tm, D), lambda i, f: (jnp.minimum(i + jnp.minimum(f, 1), nt - 1), 0)),
        vec,
        pl.BlockSpec((D, tf), lambda i, f: (0, f)),
        pl.BlockSpec((D, tf), lambda i, f: (0, f + nf)),
        pl.BlockSpec((tf, D), lambda i, f: (f, 0)),
    ]
    args = [x2, g, w_gu, w_gu, w_d]
    out_specs = row
    out_shape = jax.ShapeDtypeStruct((T, D), F32)
    if next_norm:
        in_specs.append(vec)
        args.append(next_gain)
        out_specs = [row, row]
        out_shape = [out_shape, jax.ShapeDtypeStruct((T, D), BF16)]
    return pl.pallas_call(
        functools.partial(_glu_kernel, scale=scale, nf=nf, next_norm=next_norm),
        grid=(nt, nf),
        in_specs=in_specs,
        out_specs=out_specs,
        out_shape=out_shape,
        scratch_shapes=[pltpu.VMEM((2, tm, D), BF16)],
        compiler_params=_params(("arbitrary", "arbitrary")),
        name="swiglu_ffn",
    )(*args)


def _conv_kernel(x_ref, g_ref, wb_ref, wc_ref, wu_ref, cw_ref, wo_ref, o_ref, h_ref, halo_ref,
                 *, nc):
    i = pl.program_id(1)
    c = pl.program_id(2)
    t = pl.program_id(0) * pl.num_programs(1) + i
    slot = t % 2

    @pl.when(jnp.logical_and(t == 0, c == 0))
    def _():
        h_ref[0] = _rms(x_ref[...], g_ref[...]).astype(BF16)

    @pl.when(i == 0)
    def _():
        halo_ref[c] = jnp.zeros(halo_ref.shape[1:], F32)

    def mix():
        h = h_ref[slot]
        tm, tc = h.shape[0], wb_ref.shape[1]
        prev = halo_ref[c]
        row = lax.broadcasted_iota(jnp.int32, (tm, tc // CONV_SPLIT), 0)
        acc = None
        for s in range(CONV_SPLIT):
            sl = slice(s * (tc // CONV_SPLIT), (s + 1) * (tc // CONV_SPLIT))
            bg = _dot(h, wb_ref[:, sl])
            cu = _dot(h, wc_ref[:, sl]) * _dot(h, wu_ref[:, sl])
            p1 = prev[7:8, sl]
            p2 = prev[6:7, sl]
            r1 = jnp.where(row == 0, p1, pltpu.roll(cu, 1, 0))
            r2 = jnp.where(row == 0, p2, jnp.where(row == 1, p1, pltpu.roll(cu, 2, 0)))
            taps = cw_ref[:, pl.ds(pl.multiple_of(c * tc + sl.start, LANES), sl.stop - sl.start)]
            y = taps[0:1] * r2 + taps[1:2] * r1 + taps[2:3] * cu
            halo_ref[c, :, sl] = cu[tm - 8:, :]
            d = _dot((bg * y).astype(BF16), wo_ref[sl, :])
            acc = d if acc is None else acc + d
        return acc

    @pl.when(c == 0)
    def _():
        o_ref[...] = x_ref[...] + mix()

    @pl.when(jnp.logical_and(c > 0, c < nc - 1))
    def _():
        o_ref[...] += mix()

    @pl.when(c == nc - 1)
    def _():
        o_ref[...] += mix()
        h_ref[1 - slot] = _rms(x_ref[...], g_ref[...]).astype(BF16)


def _conv_mixer(x, g, w_in, conv_w, w_out):
    B, S, D = x.shape
    tm = _tile(S, TOKEN_TILE)
    tc = _tile(D, CONV_CHANNEL_TILE)
    nc = D // tc
    ni = S // tm
    assert nc >= 2

    def x_map(b, i, c):
        t = jnp.minimum(b * ni + i + jnp.minimum(c, 1), B * ni - 1)
        return (t // ni, t % ni, 0)

    return pl.pallas_call(
        functools.partial(_conv_kernel, nc=nc),
        grid=(B, ni, nc),
        in_specs=[
            pl.BlockSpec((None, tm, D), x_map),
            pl.BlockSpec((1, D), lambda b, i, c: (0, 0)),
            pl.BlockSpec((D, tc), lambda b, i, c: (0, c)),
            pl.BlockSpec((D, tc), lambda b, i, c: (0, c + nc)),
            pl.BlockSpec((D, tc), lambda b, i, c: (0, c + 2 * nc)),
            pl.BlockSpec((3, D), lambda b, i, c: (0, 0)),
            pl.BlockSpec((tc, D), lambda b, i, c: (c, 0)),
        ],
        out_specs=pl.BlockSpec((None, tm, D), lambda b, i, c: (b, i, 0)),
        out_shape=jax.ShapeDtypeStruct((B, S, D), F32),
        scratch_shapes=[pltpu.VMEM((2, tm, D), BF16), pltpu.VMEM((nc, 8, tc), F32)],
        compiler_params=_params(("arbitrary", "arbitrary", "arbitrary")),
        name="conv_mixer",
    )(x, g, w_in, w_in, w_in, conv_w, w_out)


def _proj_sb_kernel(h_ref, w_ref, qg_ref, kg_ref, q_ref, k_ref, v_ref, *, nh):
    acc = _dot(h_ref[...], w_ref[...])
    w = nh * HEAD_DIM
    qscale = LOG2E * HEAD_DIM ** -0.5
    for hd in range(nh):
        lo = hd * HEAD_DIM
        q = acc[:, lo:lo + HEAD_DIM]
        k = acc[:, w + lo:w + lo + HEAD_DIM]
        q_ref[:, lo:lo + HEAD_DIM] = (_rms(q, qg_ref[...]) * qscale).astype(BF16)
        k_ref[:, lo:lo + HEAD_DIM] = _rms(k, kg_ref[...]).astype(BF16)
    v_ref[...] = acc[:, 2 * w:].astype(BF16)


def _proj_gla_kernel(h_ref, w_ref, wlr_ref, wg_ref, bg_ref, q_ref, k_ref, la_ref, *, kw):
    h = h_ref[...]
    acc = _dot(h, w_ref[...])
    q_ref[...] = acc[:, :kw] * (GLA_DK ** -0.5)
    k_ref[...] = acc[:, kw:]
    low_rank = _dot(h, wlr_ref[...])
    logit = _dot(low_rank.astype(BF16), wg_ref[...]) + bg_ref[...]
    la_ref[...] = -_softplus(-logit) * (1.0 / GLA_GATE_TAU)


def _proj_plain_kernel(h_ref, w_ref, o_ref, *, silu):
    acc = _dot(h_ref[...], w_ref[...])
    if silu:
        acc = acc * _sigmoid(acc)
    o_ref[...] = acc.astype(BF16)


def _row_spec(tm, n):
    return pl.BlockSpec((tm, n), lambda i: (i, 0))


def _full_spec(shape):
    return pl.BlockSpec(shape, lambda i: (0,) * len(shape))


def _proj_call(kernel, h2, extra_in, out_widths, out_dtypes, name):
    T, D = h2.shape
    tm = _tile(T, TOKEN_TILE)
    in_specs = [_row_spec(tm, D)] + [_full_spec(a.shape) for a in extra_in]
    return pl.pallas_call(
        kernel,
        grid=(T // tm,),
        in_specs=in_specs,
        out_specs=[_row_spec(tm, n) for n in out_widths],
        out_shape=[jax.ShapeDtypeStruct((T, n), dt) for n, dt in zip(out_widths, out_dtypes)],
        compiler_params=_params(("parallel",)),
        name=name,
    )(h2, *extra_in)


def _sb_kernel(q_ref, k_ref, v_ref, o_ref, acc_ref, run_ref, kmax_ref, *, tq, tk, group, nsub):
    step = pl.program_id(2)
    span = tk * group
    n_keys = k_ref.shape[0]

    @pl.when(step == 0)
    def _():
        def body(n, m):
            kk = k_ref[pl.ds(pl.multiple_of(n * span, span), span), :].astype(F32)
            return jnp.maximum(m, jnp.max(jnp.sum(kk * kk, axis=1, keepdims=True), axis=0, keepdims=True))
        m = lax.fori_loop(0, n_keys // span, body, jnp.zeros((1, 1), F32))
        kmax_ref[...] = jnp.sqrt(m)

    acc_ref[...] = jnp.zeros_like(acc_ref)
    run_ref[...] = jnp.zeros_like(run_ref)
    row = lax.broadcasted_iota(jnp.int32, (tq, tk), 0)
    col = lax.broadcasted_iota(jnp.int32, (tq, tk), 1)
    krow = lax.broadcasted_iota(jnp.int32, (tk, tk), 0)
    kcol = lax.broadcasted_iota(jnp.int32, (tk, tk), 1)
    suffix = (krow >= kcol).astype(BF16)

    def key_tiles(r, i, q, first_tile, ntiles, masked):
        base = pl.multiple_of(first_tile * tk, tk)
        run = run_ref[r]
        ws = [None] * ntiles
        for c in reversed(range(ntiles)):
            start = pl.multiple_of(base + c * tk, tk)
            z = _dot_nt(q, k_ref[pl.ds(start, tk), :])
            sp = jnp.maximum(z, 0.0) + jnp.log(1.0 + jnp.exp2(-jnp.abs(z))) * LOG2E
            if masked:
                causal = (col + start) < (row + i * tq)
                sp = jnp.where(causal, sp, 0.0)
            within = _dot(sp.astype(BF16), suffix)
            w = jnp.exp2(z - within - run)
            if masked:
                w = jnp.where(causal, w, 0.0)
            ws[c] = w.astype(BF16)
            run = run + jnp.sum(sp, axis=1, keepdims=True)
        acc_ref[r] += _dot(jnp.concatenate(ws, axis=1), v_ref[pl.ds(base, ntiles * tk), :])
        run_ref[r] = run

    subs = []
    for r in range(nsub):
        i = step * nsub + r
        q = q_ref[r * tq:(r + 1) * tq, :]
        qf = q.astype(F32)
        zmax = jnp.sqrt(jnp.sum(qf * qf, axis=1, keepdims=True)) * kmax_ref[...] * SCORE_BOUND_SLACK
        before = jnp.maximum(i - (group - 1), 0)
        key_tiles(r, i, q, before, group, True)
        subs.append((i, q, zmax, before))

    for r, (i, q, zmax, before) in enumerate(subs):
        def any_weight_left(r=r, zmax=zmax):
            return jnp.max(zmax - run_ref[r]) > -UNDERFLOW_LOG2

        def sweep(count, first_of, ntiles, alive, r=r, i=i, q=q, any_weight_left=any_weight_left):
            def cond(carry):
                n, live = carry
                return jnp.logical_and(n < count, live)

            def body(carry):
                n, _ = carry
                key_tiles(r, i, q, first_of(n), ntiles, False)
                return n + 1, any_weight_left()

            return lax.while_loop(cond, body, (jnp.int32(0), alive))[1]

        alive = sweep(before // group, lambda n, before=before: before - group * (n + 1), group,
                      any_weight_left())
        sweep(before % group, lambda n, before=before: before % group - 1 - n, 1, alive)
        o_ref[r * tq:(r + 1) * tq, :] = acc_ref[r].astype(BF16)


def _stick_breaking(q, k, v, nh):
    B, S, _ = q.shape
    tq = _tile(S, SB_TILE)
    tk = tq
    group = _tile(S // tk, SB_SPAN_TILES)
    nsub = _tile(S // tq, SB_TILES_PER_STEP)
    rows = tq * nsub
    return pl.pallas_call(
        functools.partial(_sb_kernel, tq=tq, tk=tk, group=group, nsub=nsub),
        grid=(B, nh, S // rows),
        in_specs=[
            pl.BlockSpec((None, rows, HEAD_DIM), lambda b, h, i: (b, i, h)),
            pl.BlockSpec((None, S, HEAD_DIM), lambda b, h, i: (b, 0, h)),
            pl.BlockSpec((None, S, HEAD_DIM), lambda b, h, i: (b, 0, h)),
        ],
        out_specs=pl.BlockSpec((None, rows, HEAD_DIM), lambda b, h, i: (b, i, h)),
        out_shape=jax.ShapeDtypeStruct(q.shape, BF16),
        scratch_shapes=[pltpu.VMEM((nsub, tq, HEAD_DIM), F32), pltpu.VMEM((nsub, tq, 1), F32),
                        pltpu.VMEM((1, 1), F32)],
        compiler_params=_params(("arbitrary", "arbitrary", "arbitrary")),
        name="stick_breaking",
    )(q, k, v)


def _split2(x):
    a = x.astype(BF16)
    return a, (x - a.astype(F32)).astype(BF16)


def _gla_kernel(q_ref, k_ref, la_ref, v_ref, r_ref, gn_ref, o_ref, s_ref, *, chunk, nchunk, npair):
    n = pl.program_id(1)

    @pl.when(n == 0)
    def _():
        s_ref[...] = jnp.zeros_like(s_ref)

    C = chunk
    row = lax.broadcasted_iota(jnp.int32, (C, C), 0)
    col = lax.broadcasted_iota(jnp.int32, (C, C), 1)
    rcol = lax.broadcasted_iota(jnp.int32, (C, 1), 0)
    lane = lax.broadcasted_iota(jnp.int32, (C, LANES), 1)
    first = lane < GLA_DK
    sq_first = lax.broadcasted_iota(jnp.int32, (HEAD_DIM, LANES), 1) < GLA_DK
    zero = jnp.zeros((), BF16)

    def chunk_body(rows, fine):
        q = q_ref[rows, :]
        k = k_ref[rows, :]
        la = la_ref[rows, :]
        g_hi, g_lo = _split2(la)

        def rowsum(mask):
            m = mask.astype(BF16)
            return _dot(m, g_hi) + _dot(m, g_lo)

        def block_row(x, blk, at):
            x3 = x.reshape(C // blk, blk, x.shape[1])
            return jnp.broadcast_to(x3[:, at:at + 1, :], x3.shape).reshape(x.shape)

        b = rowsum(col <= row)
        b_last = b[C - 1:C, :]

        if fine:
            levels = [(q.astype(BF16), k.astype(BF16), row == col)]
            m = 1
        else:
            ref = block_row(b, DIAG_BLOCK, DIAG_BLOCK // 2 - 1)
            levels = [((q * jnp.exp(b - ref)).astype(BF16), (k * jnp.exp(ref - b)).astype(BF16),
                       jnp.logical_and(row // DIAG_BLOCK == col // DIAG_BLOCK, col <= row))]
            m = DIAG_BLOCK
        while m < C:
            if fine:
                ref = rowsum(col <= (row // (2 * m)) * (2 * m) + (m - 1))
            else:
                ref = block_row(b, 2 * m, m - 1)
            odd = (rcol // m) % 2 == 1
            qs = jnp.where(odd, q * jnp.exp(jnp.where(odd, b - ref, 0.0)), 0.0)
            ks = jnp.where(odd, 0.0, k * jnp.exp(jnp.where(odd, 0.0, ref - b)))
            levels.append((qs.astype(BF16), ks.astype(BF16), (row // (2 * m)) == (col // (2 * m))))
            m *= 2

        q_inter = (q * jnp.exp(b)).astype(BF16)
        k_state = (k * jnp.exp(b_last - b)).astype(BF16)

        for p in range(npair):
            ksl = slice(p * LANES, (p + 1) * LANES)
            state = s_ref[p]
            state_b = state.astype(BF16)
            scores = jnp.zeros((2 * C, C), F32)
            for qs, ks, mask in levels:
                qp = qs[:, ksl]
                q2 = jnp.concatenate([jnp.where(first, qp, zero), jnp.where(first, zero, qp)], axis=0)
                mask2 = jnp.concatenate([mask, mask], axis=0)
                scores += jnp.where(mask2, _dot_nt(q2, ks[:, ksl]), 0.0)
            qi = q_inter[:, ksl]
            inter = _dot_nt(jnp.concatenate([jnp.where(first, qi, zero), jnp.where(first, zero, qi)],
                                            axis=0), state_b)
            kv = []
            for e in range(2):
                hd = 2 * p + e
                vsl = slice(hd * HEAD_DIM, (hd + 1) * HEAD_DIM)
                half = slice(e * C, (e + 1) * C)
                v = v_ref[rows, vsl]
                o = _dot(scores[half].astype(BF16), v) + inter[half]
                o = _rms(o, gn_ref[...]) * r_ref[rows, vsl].astype(F32)
                o_ref[rows, vsl] = o.astype(BF16)
                kv.append(_dot_tn(v, k_state[:, ksl]))
            s_ref[p] = jnp.exp(b_last[:, ksl]) * state + jnp.where(sq_first, kv[0], kv[1])

    for j in range(nchunk):
        rows = slice(j * C, (j + 1) * C)
        moderate = jnp.min(la_ref[rows, :]) > -GLA_MODERATE_GATE

        @pl.when(moderate)
        def _():
            chunk_body(rows, fine=False)

        @pl.when(jnp.logical_not(moderate))
        def _():
            chunk_body(rows, fine=True)


def _gla(q, k, la, v, r, gn):
    B, S, kw = q.shape
    vw = v.shape[-1]
    chunk = _tile(S, GLA_CHUNK)
    nchunk = _tile(S // chunk, GLA_CHUNKS_PER_STEP)
    rows = chunk * nchunk
    npair = kw // LANES
    blk = lambda w: pl.BlockSpec((None, rows, w), lambda b, n: (b, n, 0))
    return pl.pallas_call(
        functools.partial(_gla_kernel, chunk=chunk, nchunk=nchunk, npair=npair),
        grid=(B, S // rows),
        in_specs=[blk(kw), blk(kw), blk(kw), blk(vw), blk(vw),
                  pl.BlockSpec((1, HEAD_DIM), lambda b, n: (0, 0))],
        out_specs=blk(vw),
        out_shape=jax.ShapeDtypeStruct(v.shape, BF16),
        scratch_shapes=[pltpu.VMEM((npair, HEAD_DIM, LANES), F32)],
        compiler_params=_params(("arbitrary", "arbitrary")),
        name="gla",
    )(q, k, la, v, r, gn)


def _outproj_kernel(x_ref, a_ref, b_ref, wa_ref, wb_ref, o_ref):
    o_ref[...] = x_ref[...] + _dot(a_ref[...], wa_ref[...]) + _dot(b_ref[...], wb_ref[...])


def _out_proj(x2, a, b, w_out):
    T, D = x2.shape
    wa, wb = a.shape[1], b.shape[1]
    tm = _tile(T, TOKEN_TILE)
    return pl.pallas_call(
        _outproj_kernel,
        grid=(T // tm,),
        in_specs=[
            _row_spec(tm, D), _row_spec(tm, wa), _row_spec(tm, wb),
            pl.BlockSpec((wa, D), lambda i: (0, 0)),
            pl.BlockSpec((wb, D), lambda i: (0, 0)),
        ],
        out_specs=_row_spec(tm, D),
        out_shape=jax.ShapeDtypeStruct((T, D), F32),
        compiler_params=_params(("parallel",)),
        name="mix_out_proj",
    )(x2, a, b, w_out[:wa], w_out[wa:])


def _memkv_kernel(m_ref, g_ref, w_ref, kg_ref, k_ref, v_ref, *, nh):
    h = _rms(m_ref[...], g_ref[...]).astype(BF16)
    acc = _dot(h, w_ref[...])
    w = nh * HEAD_DIM
    for hd in range(nh):
        sl = slice(hd * HEAD_DIM, (hd + 1) * HEAD_DIM)
        k_ref[:, sl] = _rms(acc[:, sl], kg_ref[...]).astype(BF16)
    v_ref[...] = acc[:, w:].astype(BF16)


def _mem_kv(mem2, g, w_kv, kg):
    R, D = mem2.shape
    w = w_kv.shape[1] // 2
    return pl.pallas_call(
        functools.partial(_memkv_kernel, nh=w // HEAD_DIM),
        grid=(1,),
        in_specs=[_full_spec(mem2.shape), _full_spec(g.shape), _full_spec(w_kv.shape),
                  _full_spec(kg.shape)],
        out_specs=[_full_spec((R, w)), _full_spec((R, w))],
        out_shape=[jax.ShapeDtypeStruct((R, w), BF16)] * 2,
        compiler_params=_params(("arbitrary",)),
        name="mem_kv",
    )(mem2, g, w_kv, kg)


def _xa_kernel(x_ref, g_ref, wq_ref, qg_ref, k_ref, v_ref, wo_ref, o_ref, *, nh):
    x = x_ref[...]
    h = _rms(x, g_ref[...]).astype(BF16)
    qf = _dot(h, wq_ref[...])
    scale = HEAD_DIM ** -0.5
    outs = []
    for hd in range(nh):
        sl = slice(hd * HEAD_DIM, (hd + 1) * HEAD_DIM)
        q = (_rms(qf[:, sl], qg_ref[...]) * scale).astype(BF16)
        s = _dot_nt(q, k_ref[:, sl])
        s = s - jnp.max(s, axis=-1, keepdims=True)
        e = jnp.exp(s)
        p = e / jnp.sum(e, axis=-1, keepdims=True)
        outs.append(_dot(p.astype(BF16), v_ref[:, sl]).astype(BF16))
    o = jnp.concatenate(outs, axis=-1)
    o_ref[...] = x + _dot(o, wo_ref[...])


def _cross_attention(x, g, w_q, qg, k, v, w_o):
    B, S, D = x.shape
    M, W = k.shape[1], k.shape[2]
    tm = _tile(S, XA_TOKEN_TILE)
    return pl.pallas_call(
        functools.partial(_xa_kernel, nh=W // HEAD_DIM),
        grid=(B, S // tm),
        in_specs=[
            pl.BlockSpec((None, tm, D), lambda b, i: (b, i, 0)),
            pl.BlockSpec((1, D), lambda b, i: (0, 0)),
            pl.BlockSpec((D, W), lambda b, i: (0, 0)),
            pl.BlockSpec((1, HEAD_DIM), lambda b, i: (0, 0)),
            pl.BlockSpec((None, M, W), lambda b, i: (b, 0, 0)),
            pl.BlockSpec((None, M, W), lambda b, i: (b, 0, 0)),
            pl.BlockSpec((W, D), lambda b, i: (0, 0)),
        ],
        out_specs=pl.BlockSpec((None, tm, D), lambda b, i: (b, i, 0)),
        out_shape=jax.ShapeDtypeStruct((B, S, D), F32),
        compiler_params=_params(("parallel", "parallel")),
        name="mem_cross_attention",
    )(x, g, w_q, qg, k, v, w_o)


def _row(v):
    return v.reshape(1, -1).astype(F32)


def _sb_gla_mixer(x, h2, w_in3, layer, sb_qg, sb_kg, w_gate, b_gate, gla_og, w_out3):
    B, S, D = x.shape
    T = B * S
    x2 = x.reshape(T, D)
    n_heads = D // HEAD_DIM
    n_sb = n_heads // 4
    n_gla = n_heads - n_sb
    sbw = n_sb * HEAD_DIM
    kw = n_gla * GLA_DK
    vw = n_gla * HEAD_DIM
    seg = 3 * sbw
    assert 2 * kw == seg and vw == seg
    w_lr = w_in3[layer][:, 4 * seg:4 * seg + GLA_GATE_RANK]
    w_lr_p = jnp.pad(w_lr, ((0, 0), (0, LANES - GLA_GATE_RANK))).astype(BF16)
    w_gate_p = jnp.pad(w_gate, ((0, LANES - GLA_GATE_RANK), (0, 0))).astype(BF16)

    q_sb, k_sb, v_sb = _proj_call(
        functools.partial(_proj_sb_kernel, nh=n_sb), h2,
        [_weight_bf16(w_in3, layer, 0, seg), _row(sb_qg), _row(sb_kg)],
        [sbw, sbw, sbw], [BF16] * 3, "proj_sb")
    q_g, k_g, log_a = _proj_call(
        functools.partial(_proj_gla_kernel, kw=kw), h2,
        [_weight_bf16(w_in3, layer, 1, seg), w_lr_p, w_gate_p, _row(b_gate)],
        [kw, kw, kw], [F32] * 3, "proj_gla_qk")
    (v_g,) = _proj_call(functools.partial(_proj_plain_kernel, silu=False), h2,
                        [_weight_bf16(w_in3, layer, 2, seg)], [vw], [BF16], "proj_gla_v")
    (r_g,) = _proj_call(functools.partial(_proj_plain_kernel, silu=True), h2,
                        [_weight_bf16(w_in3, layer, 3, seg)], [vw], [BF16], "proj_gla_r")

    o_sb = _stick_breaking(q_sb.reshape(B, S, sbw), k_sb.reshape(B, S, sbw),
                           v_sb.reshape(B, S, sbw), n_sb)
    o_g = _gla(q_g.reshape(B, S, kw), k_g.reshape(B, S, kw), log_a.reshape(B, S, kw),
               v_g.reshape(B, S, vw), r_g.reshape(B, S, vw), _row(gla_og))
    out = _out_proj(x2, o_sb.reshape(T, sbw), o_g.reshape(T, vw), _weight_bf16(w_out3, layer))
    return out.reshape(B, S, D)


def kernel(x, mem, ffn1_norm, ffn1_w_gu, ffn1_w_down, mix_norm, ab_w_in, sb_q_norm, sb_k_norm,
           gla_w_gate, gla_b_gate, gla_o_norm, ab_w_out, conv_w_in, conv_w, conv_w_out, xa_norm,
           mem_norm, xa_w_q, xa_w_kv, xa_q_norm, xa_k_norm, xa_w_o, ffn2_norm, ffn2_w_gu,
           ffn2_w_down):
    B, S, D = x.shape
    T = B * S
    depth = ffn1_norm.shape[0]
    mem2 = mem.reshape(-1, D)
    for layer in range(depth):
        i = layer // 2
        even = layer % 2 == 0
        ffn1 = _swiglu_ffn(x.reshape(T, D), _row(ffn1_norm[layer]), _weight_bf16(ffn1_w_gu, layer),
                           _weight_bf16(ffn1_w_down, layer), scale=0.5,
                           next_gain=_row(mix_norm[layer]) if even else None)
        if even:
            x = _sb_gla_mixer(ffn1[0].reshape(B, S, D), ffn1[1], ab_w_in, i, sb_q_norm[i],
                              sb_k_norm[i], gla_w_gate[i], gla_b_gate[i], gla_o_norm[i], ab_w_out)
        else:
            x = _conv_mixer(ffn1.reshape(B, S, D), _row(mix_norm[layer]), _weight_bf16(conv_w_in, i),
                            conv_w[i].astype(F32), _weight_bf16(conv_w_out, i))
        k_m, v_m = _mem_kv(mem2, _row(mem_norm[layer]), _weight_bf16(xa_w_kv, layer),
                           _row(xa_k_norm[layer]))
        xw = k_m.shape[1]
        x = _cross_attention(x, _row(xa_norm[layer]), _weight_bf16(xa_w_q, layer),
                             _row(xa_q_norm[layer]), k_m.reshape(B, -1, xw),
                             v_m.reshape(B, -1, xw), _weight_bf16(xa_w_o, layer))
        x = _swiglu_ffn(x.reshape(T, D), _row(ffn2_norm[layer]), _weight_bf16(ffn2_w_gu, layer),
                        _weight_bf16(ffn2_w_down, layer), scale=0.5).reshape(B, S, D)
    return x
```

```python
import functools

import jax
import jax.numpy as jnp
from jax import lax
from jax.experimental import pallas as pl
from jax.experimental.pallas import tpu as pltpu

F32 = jnp.float32
BF16 = jnp.bfloat16

EPS = 1e-6
HEAD_DIM = 128
GLA_DK = 64
GLA_GATE_RANK = 16
GLA_GATE_TAU = 16.0
LANES = 128
LOG2E = 1.4426950408889634
UNDERFLOW_LOG2 = 160.0
SCORE_BOUND_SLACK = 1.001
CONV_SPLIT = 2
DIAG_BLOCK = 8
GLA_MODERATE_GATE = 10.0
CAST_BLOCK_BYTES = 8 * 1024 * 1024
TOKEN_TILE = 512
PROJ_TOKEN_TILE = 1024
FFN_HIDDEN_TILE = 1024
CONV_CHANNEL_TILE = 512
XA_TOKEN_TILE = 1024
SB_TILE = 256
SB_SPAN_TILES = 2
SB_TILES_PER_STEP = 4
GLA_CHUNK = 128
GLA_CHUNKS_PER_STEP = 1
VMEM_LIMIT = 56 * 1024 * 1024

_NT = (((1,), (1,)), ((), ()))
_TN = (((0,), (0,)), ((), ()))


def _params(sem):
    return pltpu.CompilerParams(dimension_semantics=sem, vmem_limit_bytes=VMEM_LIMIT)


def _dot(a, b):
    return jnp.dot(a, b, preferred_element_type=F32)


def _dot_nt(a, b):
    return lax.dot_general(a, b, _NT, preferred_element_type=F32)


def _dot_tn(a, b):
    return lax.dot_general(a, b, _TN, preferred_element_type=F32)


def _rms(xf, g):
    ms = jnp.mean(xf * xf, axis=-1, keepdims=True)
    return xf * lax.rsqrt(ms + EPS) * g


def _sigmoid(x):
    return 1.0 / (1.0 + jnp.exp(-x))


def _softplus(x):
    return jnp.maximum(x, 0.0) + jnp.log(1.0 + jnp.exp(-jnp.abs(x)))


def _tile(n, pref):
    t = min(pref, n)
    while n % t:
        t //= 2
    return t


def _cast_kernel(w_ref, o_ref):
    o_ref[...] = w_ref[...].astype(BF16)


def _weight_bf16(w3, layer, col_block=0, ncols=None):
    _, K, N = w3.shape
    ncols = N if ncols is None else ncols
    limit = max(8, CAST_BLOCK_BYTES // (4 * ncols))
    rb = _tile(K, 1 << (limit.bit_length() - 1))
    return pl.pallas_call(
        _cast_kernel,
        grid=(K // rb,),
        in_specs=[pl.BlockSpec((None, rb, ncols), lambda r: (layer, r, col_block))],
        out_specs=pl.BlockSpec((rb, ncols), lambda r: (r, 0)),
        out_shape=jax.ShapeDtypeStruct((K, ncols), BF16),
        compiler_params=_params(("parallel",)),
        name="weight_to_bf16",
    )(w3)


def _glu_kernel(x_ref, g_ref, wa_ref, wb_ref, wd_ref, *rest, scale, nf, next_norm):
    if next_norm:
        g2_ref, o_ref, hn_ref, h_ref = rest
    else:
        o_ref, h_ref = rest
    i = pl.program_id(0)
    f = pl.program_id(1)
    slot = i % 2

    @pl.when(jnp.logical_and(i == 0, f == 0))
    def _():
        h_ref[0] = _rms(x_ref[...], g_ref[...]).astype(BF16)

    def down():
        h = h_ref[slot]
        a = _dot(h, wa_ref[...])
        b = _dot(h, wb_ref[...])
        return _dot((a * _sigmoid(a) * b).astype(BF16), wd_ref[...])

    @pl.when(f == 0)
    def _():
        o_ref[...] = x_ref[...] * (1.0 / scale) + down()

    @pl.when(jnp.logical_and(f > 0, f < nf - 1))
    def _():
        o_ref[...] += down()

    @pl.when(f == nf - 1)
    def _():
        out = scale * (o_ref[...] + down())
        o_ref[...] = out
        if next_norm:
            hn_ref[...] = _rms(out, g2_ref[...]).astype(BF16)
        h_ref[1 - slot] = _rms(x_ref[...], g_ref[...]).astype(BF16)


def _swiglu_ffn(x2, g, w_gu, w_d, *, scale, next_gain=None):
    T, D = x2.shape
    F = w_d.shape[0]
    tm = _tile(T, TOKEN_TILE)
    tf = _tile(F, FFN_HIDDEN_TILE)
    nf = F // tf
    nt = T // tm
    assert nf >= 2
    next_norm = next_gain is not None
    row = pl.BlockSpec((tm, D), lambda i, f: (i, 0))
    vec = pl.BlockSpec((1, D), lambda i, f: (0, 0))
    in_specs = [
        pl.BlockSpec((tm, D), lambda i, f: (jnp.minimum(i + jnp.minimum(f, 1), nt - 1), 0)),
        vec,
        pl.BlockSpec((D, tf), lambda i, f: (0, f)),
        pl.BlockSpec((D, tf), lambda i, f: (0, f + nf)),
        pl.BlockSpec((tf, D), lambda i, f: (f, 0)),
    ]
    args = [x2, g, w_gu, w_gu, w_d]
    out_specs = row
    out_shape = jax.ShapeDtypeStruct((T, D), F32)
    if next_norm:
        in_specs.append(vec)
        args.append(next_gain)
        out_specs = [row, row]
        out_shape = [out_shape, jax.ShapeDtypeStruct((T, D), BF16)]
    return pl.pallas_call(
        functools.partial(_glu_kernel, scale=scale, nf=nf, next_norm=next_norm),
        grid=(nt, nf),
        in_specs=in_specs,
        out_specs=out_specs,
        out_shape=out_shape,
        scratch_shapes=[pltpu.VMEM((2, tm, D), BF16)],
        compiler_params=_params(("arbitrary", "arbitrary")),
        name="swiglu_ffn",
    )(*args)


def _conv_kernel(x_ref, g_ref, wb_ref, wc_ref, wu_ref, cw_ref, wo_ref, o_ref, h_ref, halo_ref,
                 *, nc):
    i = pl.program_id(1)
    c = pl.program_id(2)
    t = pl.program_id(0) * pl.num_programs(1) + i
    slot = t % 2

    @pl.when(jnp.logical_and(t == 0, c == 0))
    def _():
        h_ref[0] = _rms(x_ref[...], g_ref[...]).astype(BF16)

    @pl.when(i == 0)
    def _():
        halo_ref[c] = jnp.zeros(halo_ref.shape[1:], F32)

    def mix():
        h = h_ref[slot]
        tm, tc = h.shape[0], wb_ref.shape[1]
        prev = halo_ref[c]
        row = lax.broadcasted_iota(jnp.int32, (tm, tc // CONV_SPLIT), 0)
        acc = None
        for s in range(CONV_SPLIT):
            sl = slice(s * (tc // CONV_SPLIT), (s + 1) * (tc // CONV_SPLIT))
            bg = _dot(h, wb_ref[:, sl])
            cu = _dot(h, wc_ref[:, sl]) * _dot(h, wu_ref[:, sl])
            p1 = prev[7:8, sl]
            p2 = prev[6:7, sl]
            r1 = jnp.where(row == 0, p1, pltpu.roll(cu, 1, 0))
            r2 = jnp.where(row == 0, p2, jnp.where(row == 1, p1, pltpu.roll(cu, 2, 0)))
            taps = cw_ref[:, pl.ds(pl.multiple_of(c * tc + sl.start, LANES), sl.stop - sl.start)]
            y = taps[0:1] * r2 + taps[1:2] * r1 + taps[2:3] * cu
            halo_ref[c, :, sl] = cu[tm - 8:, :]
            d = _dot((bg * y).astype(BF16), wo_ref[sl, :])
            acc = d if acc is None else acc + d
        return acc

    @pl.when(c == 0)
    def _():
        o_ref[...] = x_ref[...] + mix()

    @pl.when(jnp.logical_and(c > 0, c < nc - 1))
    def _():
        o_ref[...] += mix()

    @pl.when(c == nc - 1)
    def _():
        o_ref[...] += mix()
        h_ref[1 - slot] = _rms(x_ref[...], g_ref[...]).astype(BF16)


def _conv_mixer(x, g, w_in, conv_w, w_out):
    B, S, D = x.shape
    tm = _tile(S, TOKEN_TILE)
    tc = _tile(D, CONV_CHANNEL_TILE)
    nc = D // tc
    ni = S // tm
    assert nc >= 2

    def x_map(b, i, c):
        t = jnp.minimum(b * ni + i + jnp.minimum(c, 1), B * ni - 1)
        return (t // ni, t % ni, 0)

    return pl.pallas_call(
        functools.partial(_conv_kernel, nc=nc),
        grid=(B, ni, nc),
        in_specs=[
            pl.BlockSpec((None, tm, D), x_map),
            pl.BlockSpec((1, D), lambda b, i, c: (0, 0)),
            pl.BlockSpec((D, tc), lambda b, i, c: (0, c)),
            pl.BlockSpec((D, tc), lambda b, i, c: (0, c + nc)),
            pl.BlockSpec((D, tc), lambda b, i, c: (0, c + 2 * nc)),
            pl.BlockSpec((3, D), lambda b, i, c: (0, 0)),
            pl.BlockSpec((tc, D), lambda b, i, c: (c, 0)),
        ],
        out_specs=pl.BlockSpec((None, tm, D), lambda b, i, c: (b, i, 0)),
        out_shape=jax.ShapeDtypeStruct((B, S, D), F32),
        scratch_shapes=[pltpu.VMEM((2, tm, D), BF16), pltpu.VMEM((nc, 8, tc), F32)],
        compiler_params=_params(("arbitrary", "arbitrary", "arbitrary")),
        name="conv_mixer",
    )(x, g, w_in, w_in, w_in, conv_w, w_out)


def _proj_sb_kernel(h_ref, w_ref, qg_ref, kg_ref, q_ref, k_ref, v_ref, *, nh):
    acc = _dot(h_ref[...], w_ref[...])
    w = nh * HEAD_DIM
    qscale = LOG2E * HEAD_DIM ** -0.5
    for hd in range(nh):
        lo = hd * HEAD_DIM
        q = acc[:, lo:lo + HEAD_DIM]
        k = acc[:, w + lo:w + lo + HEAD_DIM]
        q_ref[:, lo:lo + HEAD_DIM] = (_rms(q, qg_ref[...]) * qscale).astype(BF16)
        k_ref[:, lo:lo + HEAD_DIM] = _rms(k, kg_ref[...]).astype(BF16)
    v_ref[...] = acc[:, 2 * w:].astype(BF16)


def _proj_gla_kernel(h_ref, w_ref, wlr_ref, wg_ref, bg_ref, q_ref, k_ref, la_ref, *, kw):
    h = h_ref[...]
    acc = _dot(h, w_ref[...])
    q_ref[...] = acc[:, :kw] * (GLA_DK ** -0.5)
    k_ref[...] = acc[:, kw:]
    low_rank = _dot(h, wlr_ref[...])
    logit = _dot(low_rank.astype(BF16), wg_ref[...]) + bg_ref[...]
    la_ref[...] = -_softplus(-logit) * (1.0 / GLA_GATE_TAU)


def _proj_plain_kernel(h_ref, w_ref, o_ref, *, silu):
    acc = _dot(h_ref[...], w_ref[...])
    if silu:
        acc = acc * _sigmoid(acc)
    o_ref[...] = acc.astype(BF16)


def _row_spec(tm, n):
    return pl.BlockSpec((tm, n), lambda i: (i, 0))


def _full_spec(shape):
    return pl.BlockSpec(shape, lambda i: (0,) * len(shape))


def _proj_call(kernel, h2, extra_in, out_widths, out_dtypes, name):
    T, D = h2.shape
    tm = _tile(T, PROJ_TOKEN_TILE)
    in_specs = [_row_spec(tm, D)] + [_full_spec(a.shape) for a in extra_in]
    return pl.pallas_call(
        kernel,
        grid=(T // tm,),
        in_specs=in_specs,
        out_specs=[_row_spec(tm, n) for n in out_widths],
        out_shape=[jax.ShapeDtypeStruct((T, n), dt) for n, dt in zip(out_widths, out_dtypes)],
        compiler_params=_params(("parallel",)),
        name=name,
    )(h2, *extra_in)


def _sb_kernel(q_ref, k_ref, v_ref, o_ref, acc_ref, run_ref, kmax_ref, *, tq, tk, group, nsub):
    step = pl.program_id(2)
    span = tk * group
    n_keys = k_ref.shape[0]

    @pl.when(step == 0)
    def _():
        def body(n, m):
            kk = k_ref[pl.ds(pl.multiple_of(n * span, span), span), :].astype(F32)
            return jnp.maximum(m, jnp.max(jnp.sum(kk * kk, axis=1, keepdims=True), axis=0, keepdims=True))
        m = lax.fori_loop(0, n_keys // span, body, jnp.zeros((1, 1), F32))
        kmax_ref[...] = jnp.sqrt(m)

    acc_ref[...] = jnp.zeros_like(acc_ref)
    run_ref[...] = jnp.zeros_like(run_ref)
    row = lax.broadcasted_iota(jnp.int32, (tq, tk), 0)
    col = lax.broadcasted_iota(jnp.int32, (tq, tk), 1)
    krow = lax.broadcasted_iota(jnp.int32, (tk, tk), 0)
    kcol = lax.broadcasted_iota(jnp.int32, (tk, tk), 1)
    suffix = (krow >= kcol).astype(BF16)

    def key_tiles(r, i, q, first_tile, ntiles, masked):
        base = pl.multiple_of(first_tile * tk, tk)
        run = run_ref[r]
        ws = [None] * ntiles
        for c in reversed(range(ntiles)):
            start = pl.multiple_of(base + c * tk, tk)
            z = _dot_nt(q, k_ref[pl.ds(start, tk), :])
            sp = jnp.maximum(z, 0.0) + jnp.log(1.0 + jnp.exp2(-jnp.abs(z))) * LOG2E
            if masked:
                causal = (col + start) < (row + i * tq)
                sp = jnp.where(causal, sp, 0.0)
            within = _dot(sp.astype(BF16), suffix)
            w = jnp.exp2(z - within - run)
            if masked:
                w = jnp.where(causal, w, 0.0)
            ws[c] = w.astype(BF16)
            run = run + jnp.sum(sp, axis=1, keepdims=True)
        acc_ref[r] += _dot(jnp.concatenate(ws, axis=1), v_ref[pl.ds(base, ntiles * tk), :])
        run_ref[r] = run

    subs = []
    for r in range(nsub):
        i = step * nsub + r
        q = q_ref[r * tq:(r + 1) * tq, :]
        qf = q.astype(F32)
        zmax = jnp.sqrt(jnp.sum(qf * qf, axis=1, keepdims=True)) * kmax_ref[...] * SCORE_BOUND_SLACK
        before = jnp.maximum(i - (group - 1), 0)
        key_tiles(r, i, q, before, group, True)
        subs.append((i, q, zmax, before))

    for r, (i, q, zmax, before) in enumerate(subs):
        def any_weight_left(r=r, zmax=zmax):
            return jnp.max(zmax - run_ref[r]) > -UNDERFLOW_LOG2

        def sweep(count, first_of, ntiles, alive, r=r, i=i, q=q, any_weight_left=any_weight_left):
            def cond(carry):
                n, live = carry
                return jnp.logical_and(n < count, live)

            def body(carry):
                n, _ = carry
                key_tiles(r, i, q, first_of(n), ntiles, False)
                return n + 1, any_weight_left()

            return lax.while_loop(cond, body, (jnp.int32(0), alive))[1]

        alive = sweep(before // group, lambda n, before=before: before - group * (n + 1), group,
                      any_weight_left())
        sweep(before % group, lambda n, before=before: before % group - 1 - n, 1, alive)
        o_ref[r * tq:(r + 1) * tq, :] = acc_ref[r].astype(BF16)


def _stick_breaking(q, k, v, nh):
    B, S, _ = q.shape
    tq = _tile(S, SB_TILE)
    tk = tq
    group = _tile(S // tk, SB_SPAN_TILES)
    nsub = _tile(S // tq, SB_TILES_PER_STEP)
    rows = tq * nsub
    return pl.pallas_call(
        functools.partial(_sb_kernel, tq=tq, tk=tk, group=group, nsub=nsub),
        grid=(B, nh, S // rows),
        in_specs=[
            pl.BlockSpec((None, rows, HEAD_DIM), lambda b, h, i: (b, i, h)),
            pl.BlockSpec((None, S, HEAD_DIM), lambda b, h, i: (b, 0, h)),
            pl.BlockSpec((None, S, HEAD_DIM), lambda b, h, i: (b, 0, h)),
        ],
        out_specs=pl.BlockSpec((None, rows, HEAD_DIM), lambda b, h, i: (b, i, h)),
        out_shape=jax.ShapeDtypeStruct(q.shape, BF16),
        scratch_shapes=[pltpu.VMEM((nsub, tq, HEAD_DIM), F32), pltpu.VMEM((nsub, tq, 1), F32),
                        pltpu.VMEM((1, 1), F32)],
        compiler_params=_params(("arbitrary", "arbitrary", "arbitrary")),
        name="stick_breaking",
    )(q, k, v)


def _split2(x):
    a = x.astype(BF16)
    return a, (x - a.astype(F32)).astype(BF16)


def _gla_kernel(q_ref, k_ref, la_ref, v_ref, r_ref, gn_ref, o_ref, s_ref, *, chunk, nchunk, npair):
    n = pl.program_id(1)

    @pl.when(n == 0)
    def _():
        s_ref[...] = jnp.zeros_like(s_ref)

    C = chunk
    row = lax.broadcasted_iota(jnp.int32, (C, C), 0)
    col = lax.broadcasted_iota(jnp.int32, (C, C), 1)
    rcol = lax.broadcasted_iota(jnp.int32, (C, 1), 0)
    lane = lax.broadcasted_iota(jnp.int32, (C, LANES), 1)
    first = lane < GLA_DK
    sq_first = lax.broadcasted_iota(jnp.int32, (HEAD_DIM, LANES), 1) < GLA_DK
    zero = jnp.zeros((), BF16)

    def chunk_body(rows, fine):
        q = q_ref[rows, :]
        k = k_ref[rows, :]
        la = la_ref[rows, :]
        g_hi, g_lo = _split2(la)

        def rowsum(mask):
            m = mask.astype(BF16)
            return _dot(m, g_hi) + _dot(m, g_lo)

        def block_row(x, blk, at):
            x3 = x.reshape(C // blk, blk, x.shape[1])
            return jnp.broadcast_to(x3[:, at:at + 1, :], x3.shape).reshape(x.shape)

        b = rowsum(col <= row)
        b_last = b[C - 1:C, :]

        if fine:
            levels = [(q.astype(BF16), k.astype(BF16), row == col)]
            m = 1
        else:
            ref = block_row(b, DIAG_BLOCK, DIAG_BLOCK // 2 - 1)
            levels = [((q * jnp.exp(b - ref)).astype(BF16), (k * jnp.exp(ref - b)).astype(BF16),
                       jnp.logical_and(row // DIAG_BLOCK == col // DIAG_BLOCK, col <= row))]
            m = DIAG_BLOCK
        while m < C:
            if fine:
                ref = rowsum(col <= (row // (2 * m)) * (2 * m) + (m - 1))
            else:
                ref = block_row(b, 2 * m, m - 1)
            odd = (rcol // m) % 2 == 1
            qs = jnp.where(odd, q * jnp.exp(jnp.where(odd, b - ref, 0.0)), 0.0)
            ks = jnp.where(odd, 0.0, k * jnp.exp(jnp.where(odd, 0.0, ref - b)))
            levels.append((qs.astype(BF16), ks.astype(BF16), (row // (2 * m)) == (col // (2 * m))))
            m *= 2

        q_inter = (q * jnp.exp(b)).astype(BF16)
        k_state = (k * jnp.exp(b_last - b)).astype(BF16)

        for p in range(npair):
            ksl = slice(p * LANES, (p + 1) * LANES)
            state = s_ref[p]
            state_b = state.astype(BF16)
            scores = jnp.zeros((2 * C, C), F32)
            for qs, ks, mask in levels:
                qp = qs[:, ksl]
                q2 = jnp.concatenate([jnp.where(first, qp, zero), jnp.where(first, zero, qp)], axis=0)
                mask2 = jnp.concatenate([mask, mask], axis=0)
                scores += jnp.where(mask2, _dot_nt(q2, ks[:, ksl]), 0.0)
            qi = q_inter[:, ksl]
            inter = _dot_nt(jnp.concatenate([jnp.where(first, qi, zero), jnp.where(first, zero, qi)],
                                            axis=0), state_b)
            kv = []
            for e in range(2):
                hd = 2 * p + e
                vsl = slice(hd * HEAD_DIM, (hd + 1) * HEAD_DIM)
                half = slice(e * C, (e + 1) * C)
                v = v_ref[rows, vsl]
                o = _dot(scores[half].astype(BF16), v) + inter[half]
                o = _rms(o, gn_ref[...]) * r_ref[rows, vsl].astype(F32)
                o_ref[rows, vsl] = o.astype(BF16)
                kv.append(_dot_tn(v, k_state[:, ksl]))
            s_ref[p] = jnp.exp(b_last[:, ksl]) * state + jnp.where(sq_first, kv[0], kv[1])

    for j in range(nchunk):
        rows = slice(j * C, (j + 1) * C)
        moderate = jnp.min(la_ref[rows, :]) > -GLA_MODERATE_GATE

        @pl.when(moderate)
        def _():
            chunk_body(rows, fine=False)

        @pl.when(jnp.logical_not(moderate))
        def _():
            chunk_body(rows, fine=True)


def _gla(q, k, la, v, r, gn):
    B, S, kw = q.shape
    vw = v.shape[-1]
    chunk = _tile(S, GLA_CHUNK)
    nchunk = _tile(S // chunk, GLA_CHUNKS_PER_STEP)
    rows = chunk * nchunk
    npair = kw // LANES
    blk = lambda w: pl.BlockSpec((None, rows, w), lambda b, n: (b, n, 0))
    return pl.pallas_call(
        functools.partial(_gla_kernel, chunk=chunk, nchunk=nchunk, npair=npair),
        grid=(B, S // rows),
        in_specs=[blk(kw), blk(kw), blk(kw), blk(vw), blk(vw),
                  pl.BlockSpec((1, HEAD_DIM), lambda b, n: (0, 0))],
        out_specs=blk(vw),
        out_shape=jax.ShapeDtypeStruct(v.shape, BF16),
        scratch_shapes=[pltpu.VMEM((npair, HEAD_DIM, LANES), F32)],
        compiler_params=_params(("arbitrary", "arbitrary")),
        name="gla",
    )(q, k, la, v, r, gn)


def _outproj_kernel(x_ref, a_ref, b_ref, wa_ref, wb_ref, o_ref):
    o_ref[...] = x_ref[...] + _dot(a_ref[...], wa_ref[...]) + _dot(b_ref[...], wb_ref[...])


def _out_proj(x2, a, b, w_out):
    T, D = x2.shape
    wa, wb = a.shape[1], b.shape[1]
    tm = _tile(T, TOKEN_TILE)
    return pl.pallas_call(
        _outproj_kernel,
        grid=(T // tm,),
        in_specs=[
            _row_spec(tm, D), _row_spec(tm, wa), _row_spec(tm, wb),
            pl.BlockSpec((wa, D), lambda i: (0, 0)),
            pl.BlockSpec((wb, D), lambda i: (0, 0)),
        ],
        out_specs=_row_spec(tm, D),
        out_shape=jax.ShapeDtypeStruct((T, D), F32),
        compiler_params=_params(("parallel",)),
        name="mix_out_proj",
    )(x2, a, b, w_out[:wa], w_out[wa:])


def _memkv_kernel(m_ref, g_ref, w_ref, kg_ref, k_ref, v_ref, *, nh):
    h = _rms(m_ref[...], g_ref[...]).astype(BF16)
    acc = _dot(h, w_ref[...])
    w = nh * HEAD_DIM
    for hd in range(nh):
        sl = slice(hd * HEAD_DIM, (hd + 1) * HEAD_DIM)
        k_ref[:, sl] = _rms(acc[:, sl], kg_ref[...]).astype(BF16)
    v_ref[...] = acc[:, w:].astype(BF16)


def _mem_kv(mem2, g, w_kv, kg):
    R, D = mem2.shape
    w = w_kv.shape[1] // 2
    return pl.pallas_call(
        functools.partial(_memkv_kernel, nh=w // HEAD_DIM),
        grid=(1,),
        in_specs=[_full_spec(mem2.shape), _full_spec(g.shape), _full_spec(w_kv.shape),
                  _full_spec(kg.shape)],
        out_specs=[_full_spec((R, w)), _full_spec((R, w))],
        out_shape=[jax.ShapeDtypeStruct((R, w), BF16)] * 2,
        compiler_params=_params(("arbitrary",)),
        name="mem_kv",
    )(mem2, g, w_kv, kg)


def _xa_kernel(x_ref, g_ref, wq_ref, qg_ref, k_ref, v_ref, wo_ref, o_ref, *, nh):
    x = x_ref[...]
    h = _rms(x, g_ref[...]).astype(BF16)
    qf = _dot(h, wq_ref[...])
    scale = HEAD_DIM ** -0.5
    outs = []
    for hd in range(nh):
        sl = slice(hd * HEAD_DIM, (hd + 1) * HEAD_DIM)
        q = (_rms(qf[:, sl], qg_ref[...]) * scale).astype(BF16)
        s = _dot_nt(q, k_ref[:, sl])
        s = s - jnp.max(s, axis=-1, keepdims=True)
        e = jnp.exp(s)
        p = e / jnp.sum(e, axis=-1, keepdims=True)
        outs.append(_dot(p.astype(BF16), v_ref[:, sl]).astype(BF16))
    o = jnp.concatenate(outs, axis=-1)
    o_ref[...] = x + _dot(o, wo_ref[...])


def _cross_attention(x, g, w_q, qg, k, v, w_o):
    B, S, D = x.shape
    M, W = k.shape[1], k.shape[2]
    tm = _tile(S, XA_TOKEN_TILE)
    return pl.pallas_call(
        functools.partial(_xa_kernel, nh=W // HEAD_DIM),
        grid=(B, S // tm),
        in_specs=[
            pl.BlockSpec((None, tm, D), lambda b, i: (b, i, 0)),
            pl.BlockSpec((1, D), lambda b, i: (0, 0)),
            pl.BlockSpec((D, W), lambda b, i: (0, 0)),
            pl.BlockSpec((1, HEAD_DIM), lambda b, i: (0, 0)),
            pl.BlockSpec((None, M, W), lambda b, i: (b, 0, 0)),
            pl.BlockSpec((None, M, W), lambda b, i: (b, 0, 0)),
            pl.BlockSpec((W, D), lambda b, i: (0, 0)),
        ],
        out_specs=pl.BlockSpec((None, tm, D), lambda b, i: (b, i, 0)),
        out_shape=jax.ShapeDtypeStruct((B, S, D), F32),
        compiler_params=_params(("parallel", "parallel")),
        name="mem_cross_attention",
    )(x, g, w_q, qg, k, v, w_o)


def _row(v):
    return v.reshape(1, -1).astype(F32)


def _sb_gla_mixer(x, h2, w_in3, layer, sb_qg, sb_kg, w_gate, b_gate, gla_og, w_out3):
    B, S, D = x.shape
    T = B * S
    x2 = x.reshape(T, D)
    n_heads = D // HEAD_DIM
    n_sb = n_heads // 4
    n_gla = n_heads - n_sb
    sbw = n_sb * HEAD_DIM
    kw = n_gla * GLA_DK
    vw = n_gla * HEAD_DIM
    seg = 3 * sbw
    assert 2 * kw == seg and vw == seg
    w_lr = w_in3[layer][:, 4 * seg:4 * seg + GLA_GATE_RANK]
    w_lr_p = jnp.pad(w_lr, ((0, 0), (0, LANES - GLA_GATE_RANK))).astype(BF16)
    w_gate_p = jnp.pad(w_gate, ((0, LANES - GLA_GATE_RANK), (0, 0))).astype(BF16)

    q_sb, k_sb, v_sb = _proj_call(
        functools.partial(_proj_sb_kernel, nh=n_sb), h2,
        [_weight_bf16(w_in3, layer, 0, seg), _row(sb_qg), _row(sb_kg)],
        [sbw, sbw, sbw], [BF16] * 3, "proj_sb")
    q_g, k_g, log_a = _proj_call(
        functools.partial(_proj_gla_kernel, kw=kw), h2,
        [_weight_bf16(w_in3, layer, 1, seg), w_lr_p, w_gate_p, _row(b_gate)],
        [kw, kw, kw], [F32] * 3, "proj_gla_qk")
    (v_g,) = _proj_call(functools.partial(_proj_plain_kernel, silu=False), h2,
                        [_weight_bf16(w_in3, layer, 2, seg)], [vw], [BF16], "proj_gla_v")
    (r_g,) = _proj_call(functools.partial(_proj_plain_kernel, silu=True), h2,
                        [_weight_bf16(w_in3, layer, 3, seg)], [vw], [BF16], "proj_gla_r")

    o_sb = _stick_breaking(q_sb.reshape(B, S, sbw), k_sb.reshape(B, S, sbw),
                           v_sb.reshape(B, S, sbw), n_sb)
    o_g = _gla(q_g.reshape(B, S, kw), k_g.reshape(B, S, kw), log_a.reshape(B, S, kw),
               v_g.reshape(B, S, vw), r_g.reshape(B, S, vw), _row(gla_og))
    out = _out_proj(x2, o_sb.reshape(T, sbw), o_g.reshape(T, vw), _weight_bf16(w_out3, layer))
    return out.reshape(B, S, D)


def kernel(x, mem, ffn1_norm, ffn1_w_gu, ffn1_w_down, mix_norm, ab_w_in, sb_q_norm, sb_k_norm,
           gla_w_gate, gla_b_gate, gla_o_norm, ab_w_out, conv_w_in, conv_w, conv_w_out, xa_norm,
           mem_norm, xa_w_q, xa_w_kv, xa_q_norm, xa_k_norm, xa_w_o, ffn2_norm, ffn2_w_gu,
           ffn2_w_down):
    B, S, D = x.shape
    T = B * S
    depth = ffn1_norm.shape[0]
    mem2 = mem.reshape(-1, D)
    for layer in range(depth):
        i = layer // 2
        even = layer % 2 == 0
        ffn1 = _swiglu_ffn(x.reshape(T, D), _row(ffn1_norm[layer]), _weight_bf16(ffn1_w_gu, layer),
                           _weight_bf16(ffn1_w_down, layer), scale=0.5,
                           next_gain=_row(mix_norm[layer]) if even else None)
        if even:
            x = _sb_gla_mixer(ffn1[0].reshape(B, S, D), ffn1[1], ab_w_in, i, sb_q_norm[i],
                              sb_k_norm[i], gla_w_gate[i], gla_b_gate[i], gla_o_norm[i], ab_w_out)
        else:
            x = _conv_mixer(ffn1.reshape(B, S, D), _row(mix_norm[layer]), _weight_bf16(conv_w_in, i),
                            conv_w[i].astype(F32), _weight_bf16(conv_w_out, i))
        k_m, v_m = _mem_kv(mem2, _row(mem_norm[layer]), _weight_bf16(xa_w_kv, layer),
                           _row(xa_k_norm[layer]))
        xw = k_m.shape[1]
        x = _cross_attention(x, _row(xa_norm[layer]), _weight_bf16(xa_w_q, layer),
                             _row(xa_q_norm[layer]), k_m.reshape(B, -1, xw),
                             v_m.reshape(B, -1, xw), _weight_bf16(xa_w_o, layer))
        x = _swiglu_ffn(x.reshape(T, D), _row(ffn2_norm[layer]), _weight_bf16(ffn2_w_gu, layer),
                        _weight_bf16(ffn2_w_down, layer), scale=0.5).reshape(B, S, D)
    return x
```

```python
import functools

import jax
import jax.numpy as jnp
from jax import lax
from jax.experimental import pallas as pl
from jax.experimental.pallas import tpu as pltpu

F32 = jnp.float32
BF16 = jnp.bfloat16

EPS = 1e-6
HEAD_DIM = 128
GLA_DK = 64
GLA_GATE_RANK = 16
GLA_GATE_TAU = 16.0
LANES = 128
LOG2E = 1.4426950408889634
UNDERFLOW_LOG2 = 160.0
SCORE_BOUND_SLACK = 1.001
CONV_SPLIT = 2
DIAG_BLOCK = 8
GLA_MODERATE_GATE = 10.0
CAST_BLOCK_BYTES = 8 * 1024 * 1024
TOKEN_TILE = 512
PROJ_TOKEN_TILE = 1024
FFN_HIDDEN_TILE = 1024
CONV_CHANNEL_TILE = 512
XA_TOKEN_TILE = 1024
SB_TILE = 256
SB_SPAN_TILES = 2
SB_TILES_PER_STEP = 4
GLA_CHUNK = 128
GLA_CHUNKS_PER_STEP = 1
VMEM_LIMIT = 60 * 1024 * 1024

_NT = (((1,), (1,)), ((), ()))
_TN = (((0,), (0,)), ((), ()))


def _params(sem):
    return pltpu.CompilerParams(dimension_semantics=sem, vmem_limit_bytes=VMEM_LIMIT)


def _dot(a, b):
    return jnp.dot(a, b, preferred_element_type=F32)


def _dot_nt(a, b):
    return lax.dot_general(a, b, _NT, preferred_element_type=F32)


def _dot_tn(a, b):
    return lax.dot_general(a, b, _TN, preferred_element_type=F32)


def _rms(xf, g):
    ms = jnp.mean(xf * xf, axis=-1, keepdims=True)
    return xf * lax.rsqrt(ms + EPS) * g


def _sigmoid(x):
    return 1.0 / (1.0 + jnp.exp(-x))


def _softplus(x):
    return jnp.maximum(x, 0.0) + jnp.log(1.0 + jnp.exp(-jnp.abs(x)))


def _tile(n, pref):
    t = min(pref, n)
    while n % t:
        t //= 2
    return t


def _cast_kernel(w_ref, o_ref):
    o_ref[...] = w_ref[...].astype(BF16)


def _weight_bf16(w3, layer, col_block=0, ncols=None):
    _, K, N = w3.shape
    ncols = N if ncols is None else ncols
    limit = max(8, CAST_BLOCK_BYTES // (4 * ncols))
    rb = _tile(K, 1 << (limit.bit_length() - 1))
    return pl.pallas_call(
        _cast_kernel,
        grid=(K // rb,),
        in_specs=[pl.BlockSpec((None, rb, ncols), lambda r: (layer, r, col_block))],
        out_specs=pl.BlockSpec((rb, ncols), lambda r: (r, 0)),
        out_shape=jax.ShapeDtypeStruct((K, ncols), BF16),
        compiler_params=_params(("parallel",)),
        name="weight_to_bf16",
    )(w3)


def _glu_kernel(x_ref, g_ref, wa_ref, wb_ref, wd_ref, *rest, scale, nf, next_norm):
    if next_norm:
        g2_ref, o_ref, hn_ref, h_ref = rest
    else:
        o_ref, h_ref = rest
    i = pl.program_id(0)
    f = pl.program_id(1)
    slot = i % 2

    @pl.when(jnp.logical_and(i == 0, f == 0))
    def _():
        h_ref[0] = _rms(x_ref[...], g_ref[...]).astype(BF16)

    def down():
        h = h_ref[slot]
        a = _dot(h, wa_ref[...])
        b = _dot(h, wb_ref[...])
        return _dot((a * _sigmoid(a) * b).astype(BF16), wd_ref[...])

    @pl.when(f == 0)
    def _():
        o_ref[...] = x_ref[...] * (1.0 / scale) + down()

    @pl.when(jnp.logical_and(f > 0, f < nf - 1))
    def _():
        o_ref[...] += down()

    @pl.when(f == nf - 1)
    def _():
        out = scale * (o_ref[...] + down())
        o_ref[...] = out
        if next_norm:
            hn_ref[...] = _rms(out, g2_ref[...]).astype(BF16)
        h_ref[1 - slot] = _rms(x_ref[...], g_ref[...]).astype(BF16)


def _swiglu_ffn(x2, g, w_gu, w_d, *, scale, next_gain=None):
    T, D = x2.shape
    F = w_d.shape[0]
    next_norm = next_gain is not None
    tm = _tile(T, TOKEN_TILE if next_norm else 2 * TOKEN_TILE)
    tf = _tile(F, FFN_HIDDEN_TILE if next_norm else FFN_HIDDEN_TILE // 2)
    nf = F // tf
    nt = T // tm
    assert nf >= 2
    row = pl.BlockSpec((tm, D), lambda i, f: (i, 0))
    vec = pl.BlockSpec((1, D), lambda i, f: (0, 0))
    in_specs = [
        pl.BlockSpec((tm, D), lambda i, f: (jnp.minimum(i + jnp.minimum(f, 1), nt - 1), 0)),
        vec,
        pl.BlockSpec((D, tf), lambda i, f: (0, f)),
        pl.BlockSpec((D, tf), lambda i, f: (0, f + nf)),
        pl.BlockSpec((tf, D), lambda i, f: (f, 0)),
    ]
    args = [x2, g, w_gu, w_gu, w_d]
    out_specs = row
    out_shape = jax.ShapeDtypeStruct((T, D), F32)
    if next_norm:
        in_specs.append(vec)
        args.append(next_gain)
        out_specs = [row, row]
        out_shape = [out_shape, jax.ShapeDtypeStruct((T, D), BF16)]
    return pl.pallas_call(
        functools.partial(_glu_kernel, scale=scale, nf=nf, next_norm=next_norm),
        grid=(nt, nf),
        in_specs=in_specs,
        out_specs=out_specs,
        out_shape=out_shape,
        scratch_shapes=[pltpu.VMEM((2, tm, D), BF16)],
        compiler_params=_params(("arbitrary", "arbitrary")),
        name="swiglu_ffn",
    )(*args)


def _conv_kernel(x_ref, g_ref, wb_ref, wc_ref, wu_ref, cw_ref, wo_ref, o_ref, h_ref, halo_ref,
                 *, nc):
    i = pl.program_id(1)
    c = pl.program_id(2)
    t = pl.program_id(0) * pl.num_programs(1) + i
    slot = t % 2

    @pl.when(jnp.logical_and(t == 0, c == 0))
    def _():
        h_ref[0] = _rms(x_ref[...], g_ref[...]).astype(BF16)

    @pl.when(i == 0)
    def _():
        halo_ref[c] = jnp.zeros(halo_ref.shape[1:], F32)

    def mix():
        h = h_ref[slot]
        tm, tc = h.shape[0], wb_ref.shape[1]
        prev = halo_ref[c]
        row = lax.broadcasted_iota(jnp.int32, (tm, tc // CONV_SPLIT), 0)
        acc = None
        for s in range(CONV_SPLIT):
            sl = slice(s * (tc // CONV_SPLIT), (s + 1) * (tc // CONV_SPLIT))
            bg = _dot(h, wb_ref[:, sl])
            cu = _dot(h, wc_ref[:, sl]) * _dot(h, wu_ref[:, sl])
            p1 = prev[7:8, sl]
            p2 = prev[6:7, sl]
            r1 = jnp.where(row == 0, p1, pltpu.roll(cu, 1, 0))
            r2 = jnp.where(row == 0, p2, jnp.where(row == 1, p1, pltpu.roll(cu, 2, 0)))
            taps = cw_ref[:, pl.ds(pl.multiple_of(c * tc + sl.start, LANES), sl.stop - sl.start)]
            y = taps[0:1] * r2 + taps[1:2] * r1 + taps[2:3] * cu
            halo_ref[c, :, sl] = cu[tm - 8:, :]
            d = _dot((bg * y).astype(BF16), wo_ref[sl, :])
            acc = d if acc is None else acc + d
        return acc

    @pl.when(c == 0)
    def _():
        o_ref[...] = x_ref[...] + mix()

    @pl.when(jnp.logical_and(c > 0, c < nc - 1))
    def _():
        o_ref[...] += mix()

    @pl.when(c == nc - 1)
    def _():
        o_ref[...] += mix()
        h_ref[1 - slot] = _rms(x_ref[...], g_ref[...]).astype(BF16)


def _conv_mixer(x, g, w_in, conv_w, w_out):
    B, S, D = x.shape
    tm = _tile(S, TOKEN_TILE)
    tc = _tile(D, CONV_CHANNEL_TILE)
    nc = D // tc
    ni = S // tm
    assert nc >= 2

    def x_map(b, i, c):
        t = jnp.minimum(b * ni + i + jnp.minimum(c, 1), B * ni - 1)
        return (t // ni, t % ni, 0)

    return pl.pallas_call(
        functools.partial(_conv_kernel, nc=nc),
        grid=(B, ni, nc),
        in_specs=[
            pl.BlockSpec((None, tm, D), x_map),
            pl.BlockSpec((1, D), lambda b, i, c: (0, 0)),
            pl.BlockSpec((D, tc), lambda b, i, c: (0, c)),
            pl.BlockSpec((D, tc), lambda b, i, c: (0, c + nc)),
            pl.BlockSpec((D, tc), lambda b, i, c: (0, c + 2 * nc)),
            pl.BlockSpec((3, D), lambda b, i, c: (0, 0)),
            pl.BlockSpec((tc, D), lambda b, i, c: (c, 0)),
        ],
        out_specs=pl.BlockSpec((None, tm, D), lambda b, i, c: (b, i, 0)),
        out_shape=jax.ShapeDtypeStruct((B, S, D), F32),
        scratch_shapes=[pltpu.VMEM((2, tm, D), BF16), pltpu.VMEM((nc, 8, tc), F32)],
        compiler_params=_params(("arbitrary", "arbitrary", "arbitrary")),
        name="conv_mixer",
    )(x, g, w_in, w_in, w_in, conv_w, w_out)


def _proj_sb_kernel(h_ref, w_ref, qg_ref, kg_ref, q_ref, k_ref, v_ref, *, nh):
    acc = _dot(h_ref[...], w_ref[...])
    w = nh * HEAD_DIM
    qscale = LOG2E * HEAD_DIM ** -0.5
    for hd in range(nh):
        lo = hd * HEAD_DIM
        q = acc[:, lo:lo + HEAD_DIM]
        k = acc[:, w + lo:w + lo + HEAD_DIM]
        q_ref[:, lo:lo + HEAD_DIM] = (_rms(q, qg_ref[...]) * qscale).astype(BF16)
        k_ref[:, lo:lo + HEAD_DIM] = _rms(k, kg_ref[...]).astype(BF16)
    v_ref[...] = acc[:, 2 * w:].astype(BF16)


def _proj_gla_kernel(h_ref, w_ref, wlr_ref, wg_ref, bg_ref, q_ref, k_ref, la_ref, *, kw):
    h = h_ref[...]
    acc = _dot(h, w_ref[...])
    q_ref[...] = acc[:, :kw] * (GLA_DK ** -0.5)
    k_ref[...] = acc[:, kw:]
    low_rank = _dot(h, wlr_ref[...])
    logit = _dot(low_rank.astype(BF16), wg_ref[...]) + bg_ref[...]
    la_ref[...] = -_softplus(-logit) * (1.0 / GLA_GATE_TAU)


def _proj_plain_kernel(h_ref, w_ref, o_ref, *, silu):
    acc = _dot(h_ref[...], w_ref[...])
    if silu:
        acc = acc * _sigmoid(acc)
    o_ref[...] = acc.astype(BF16)


def _row_spec(tm, n):
    return pl.BlockSpec((tm, n), lambda i: (i, 0))


def _full_spec(shape):
    return pl.BlockSpec(shape, lambda i: (0,) * len(shape))


def _proj_call(kernel, h2, extra_in, out_widths, out_dtypes, name):
    T, D = h2.shape
    tm = _tile(T, PROJ_TOKEN_TILE)
    in_specs = [_row_spec(tm, D)] + [_full_spec(a.shape) for a in extra_in]
    return pl.pallas_call(
        kernel,
        grid=(T // tm,),
        in_specs=in_specs,
        out_specs=[_row_spec(tm, n) for n in out_widths],
        out_shape=[jax.ShapeDtypeStruct((T, n), dt) for n, dt in zip(out_widths, out_dtypes)],
        compiler_params=_params(("parallel",)),
        name=name,
    )(h2, *extra_in)


def _sb_kernel(q_ref, k_ref, v_ref, o_ref, acc_ref, run_ref, kmax_ref, *, tq, tk, group, nsub):
    step = pl.program_id(2)
    span = tk * group
    n_keys = k_ref.shape[0]

    @pl.when(step == 0)
    def _():
        def body(n, m):
            kk = k_ref[pl.ds(pl.multiple_of(n * span, span), span), :].astype(F32)
            return jnp.maximum(m, jnp.max(jnp.sum(kk * kk, axis=1, keepdims=True), axis=0, keepdims=True))
        m = lax.fori_loop(0, n_keys // span, body, jnp.zeros((1, 1), F32))
        kmax_ref[...] = jnp.sqrt(m)

    acc_ref[...] = jnp.zeros_like(acc_ref)
    run_ref[...] = jnp.zeros_like(run_ref)
    row = lax.broadcasted_iota(jnp.int32, (tq, tk), 0)
    col = lax.broadcasted_iota(jnp.int32, (tq, tk), 1)
    krow = lax.broadcasted_iota(jnp.int32, (tk, tk), 0)
    kcol = lax.broadcasted_iota(jnp.int32, (tk, tk), 1)
    suffix = (krow >= kcol).astype(BF16)

    def key_tiles(r, i, q, first_tile, ntiles, masked):
        base = pl.multiple_of(first_tile * tk, tk)
        run = run_ref[r]
        ws = [None] * ntiles
        for c in reversed(range(ntiles)):
            start = pl.multiple_of(base + c * tk, tk)
            z = _dot_nt(q, k_ref[pl.ds(start, tk), :])
            sp = jnp.maximum(z, 0.0) + jnp.log(1.0 + jnp.exp2(-jnp.abs(z))) * LOG2E
            if masked:
                causal = (col + start) < (row + i * tq)
                sp = jnp.where(causal, sp, 0.0)
            within = _dot(sp.astype(BF16), suffix)
            w = jnp.exp2(z - within - run)
            if masked:
                w = jnp.where(causal, w, 0.0)
            ws[c] = w.astype(BF16)
            run = run + jnp.sum(sp, axis=1, keepdims=True)
        acc_ref[r] += _dot(jnp.concatenate(ws, axis=1), v_ref[pl.ds(base, ntiles * tk), :])
        run_ref[r] = run

    subs = []
    for r in range(nsub):
        i = step * nsub + r
        q = q_ref[r * tq:(r + 1) * tq, :]
        qf = q.astype(F32)
        zmax = jnp.sqrt(jnp.sum(qf * qf, axis=1, keepdims=True)) * kmax_ref[...] * SCORE_BOUND_SLACK
        before = jnp.maximum(i - (group - 1), 0)
        key_tiles(r, i, q, before, group, True)
        subs.append((i, q, zmax, before))

    for r, (i, q, zmax, before) in enumerate(subs):
        def any_weight_left(r=r, zmax=zmax):
            return jnp.max(zmax - run_ref[r]) > -UNDERFLOW_LOG2

        def sweep(count, first_of, ntiles, alive, r=r, i=i, q=q, any_weight_left=any_weight_left):
            def cond(carry):
                n, live = carry
                return jnp.logical_and(n < count, live)

            def body(carry):
                n, _ = carry
                key_tiles(r, i, q, first_of(n), ntiles, False)
                return n + 1, any_weight_left()

            return lax.while_loop(cond, body, (jnp.int32(0), alive))[1]

        alive = sweep(before // group, lambda n, before=before: before - group * (n + 1), group,
                      any_weight_left())
        sweep(before % group, lambda n, before=before: before % group - 1 - n, 1, alive)
        o_ref[r * tq:(r + 1) * tq, :] = acc_ref[r].astype(BF16)


def _stick_breaking(q, k, v, nh):
    B, S, _ = q.shape
    tq = _tile(S, SB_TILE)
    tk = tq
    group = _tile(S // tk, SB_SPAN_TILES)
    nsub = _tile(S // tq, SB_TILES_PER_STEP)
    rows = tq * nsub
    return pl.pallas_call(
        functools.partial(_sb_kernel, tq=tq, tk=tk, group=group, nsub=nsub),
        grid=(B, nh, S // rows),
        in_specs=[
            pl.BlockSpec((None, rows, HEAD_DIM), lambda b, h, i: (b, i, h)),
            pl.BlockSpec((None, S, HEAD_DIM), lambda b, h, i: (b, 0, h)),
            pl.BlockSpec((None, S, HEAD_DIM), lambda b, h, i: (b, 0, h)),
        ],
        out_specs=pl.BlockSpec((None, rows, HEAD_DIM), lambda b, h, i: (b, i, h)),
        out_shape=jax.ShapeDtypeStruct(q.shape, BF16),
        scratch_shapes=[pltpu.VMEM((nsub, tq, HEAD_DIM), F32), pltpu.VMEM((nsub, tq, 1), F32),
                        pltpu.VMEM((1, 1), F32)],
        compiler_params=_params(("arbitrary", "arbitrary", "arbitrary")),
        name="stick_breaking",
    )(q, k, v)


def _split2(x):
    a = x.astype(BF16)
    return a, (x - a.astype(F32)).astype(BF16)


def _gla_kernel(q_ref, k_ref, la_ref, v_ref, r_ref, gn_ref, o_ref, s_ref, *, chunk, nchunk, npair):
    n = pl.program_id(1)

    @pl.when(n == 0)
    def _():
        s_ref[...] = jnp.zeros_like(s_ref)

    C = chunk
    row = lax.broadcasted_iota(jnp.int32, (C, C), 0)
    col = lax.broadcasted_iota(jnp.int32, (C, C), 1)
    rcol = lax.broadcasted_iota(jnp.int32, (C, 1), 0)
    lane = lax.broadcasted_iota(jnp.int32, (C, LANES), 1)
    first = lane < GLA_DK
    sq_first = lax.broadcasted_iota(jnp.int32, (HEAD_DIM, LANES), 1) < GLA_DK
    zero = jnp.zeros((), BF16)

    def chunk_body(rows, fine):
        q = q_ref[rows, :]
        k = k_ref[rows, :]
        la = la_ref[rows, :]
        g_hi, g_lo = _split2(la)

        def rowsum(mask):
            m = mask.astype(BF16)
            return _dot(m, g_hi) + _dot(m, g_lo)

        def block_row(x, blk, at):
            x3 = x.reshape(C // blk, blk, x.shape[1])
            return jnp.broadcast_to(x3[:, at:at + 1, :], x3.shape).reshape(x.shape)

        b = rowsum(col <= row)
        b_last = b[C - 1:C, :]

        if fine:
            levels = [(q.astype(BF16), k.astype(BF16), row == col)]
            m = 1
        else:
            ref = block_row(b, DIAG_BLOCK, DIAG_BLOCK // 2 - 1)
            levels = [((q * jnp.exp(b - ref)).astype(BF16), (k * jnp.exp(ref - b)).astype(BF16),
                       jnp.logical_and(row // DIAG_BLOCK == col // DIAG_BLOCK, col <= row))]
            m = DIAG_BLOCK
        while m < C:
            if fine:
                ref = rowsum(col <= (row // (2 * m)) * (2 * m) + (m - 1))
            else:
                ref = block_row(b, 2 * m, m - 1)
            odd = (rcol // m) % 2 == 1
            qs = jnp.where(odd, q * jnp.exp(jnp.where(odd, b - ref, 0.0)), 0.0)
            ks = jnp.where(odd, 0.0, k * jnp.exp(jnp.where(odd, 0.0, ref - b)))
            levels.append((qs.astype(BF16), ks.astype(BF16), (row // (2 * m)) == (col // (2 * m))))
            m *= 2

        q_inter = (q * jnp.exp(b)).astype(BF16)
        k_state = (k * jnp.exp(b_last - b)).astype(BF16)

        for p in range(npair):
            ksl = slice(p * LANES, (p + 1) * LANES)
            state = s_ref[p]
            state_b = state.astype(BF16)
            scores = jnp.zeros((2 * C, C), F32)
            for qs, ks, mask in levels:
                qp = qs[:, ksl]
                q2 = jnp.concatenate([jnp.where(first, qp, zero), jnp.where(first, zero, qp)], axis=0)
                mask2 = jnp.concatenate([mask, mask], axis=0)
                scores += jnp.where(mask2, _dot_nt(q2, ks[:, ksl]), 0.0)
            qi = q_inter[:, ksl]
            inter = _dot_nt(jnp.concatenate([jnp.where(first, qi, zero), jnp.where(first, zero, qi)],
                                            axis=0), state_b)
            kv = []
            for e in range(2):
                hd = 2 * p + e
                vsl = slice(hd * HEAD_DIM, (hd + 1) * HEAD_DIM)
                half = slice(e * C, (e + 1) * C)
                v = v_ref[rows, vsl]
                o = _dot(scores[half].astype(BF16), v) + inter[half]
                o = _rms(o, gn_ref[...]) * r_ref[rows, vsl].astype(F32)
                o_ref[rows, vsl] = o.astype(BF16)
                kv.append(_dot_tn(v, k_state[:, ksl]))
            s_ref[p] = jnp.exp(b_last[:, ksl]) * state + jnp.where(sq_first, kv[0], kv[1])

    for j in range(nchunk):
        rows = slice(j * C, (j + 1) * C)
        moderate = jnp.min(la_ref[rows, :]) > -GLA_MODERATE_GATE

        @pl.when(moderate)
        def _():
            chunk_body(rows, fine=False)

        @pl.when(jnp.logical_not(moderate))
        def _():
            chunk_body(rows, fine=True)


def _gla(q, k, la, v, r, gn):
    B, S, kw = q.shape
    vw = v.shape[-1]
    chunk = _tile(S, GLA_CHUNK)
    nchunk = _tile(S // chunk, GLA_CHUNKS_PER_STEP)
    rows = chunk * nchunk
    npair = kw // LANES
    blk = lambda w: pl.BlockSpec((None, rows, w), lambda b, n: (b, n, 0))
    return pl.pallas_call(
        functools.partial(_gla_kernel, chunk=chunk, nchunk=nchunk, npair=npair),
        grid=(B, S // rows),
        in_specs=[blk(kw), blk(kw), blk(kw), blk(vw), blk(vw),
                  pl.BlockSpec((1, HEAD_DIM), lambda b, n: (0, 0))],
        out_specs=blk(vw),
        out_shape=jax.ShapeDtypeStruct(v.shape, BF16),
        scratch_shapes=[pltpu.VMEM((npair, HEAD_DIM, LANES), F32)],
        compiler_params=_params(("arbitrary", "arbitrary")),
        name="gla",
    )(q, k, la, v, r, gn)


def _outproj_kernel(x_ref, a_ref, b_ref, wa_ref, wb_ref, o_ref):
    o_ref[...] = x_ref[...] + _dot(a_ref[...], wa_ref[...]) + _dot(b_ref[...], wb_ref[...])


def _out_proj(x2, a, b, w_out):
    T, D = x2.shape
    wa, wb = a.shape[1], b.shape[1]
    tm = _tile(T, TOKEN_TILE)
    return pl.pallas_call(
        _outproj_kernel,
        grid=(T // tm,),
        in_specs=[
            _row_spec(tm, D), _row_spec(tm, wa), _row_spec(tm, wb),
            pl.BlockSpec((wa, D), lambda i: (0, 0)),
            pl.BlockSpec((wb, D), lambda i: (0, 0)),
        ],
        out_specs=_row_spec(tm, D),
        out_shape=jax.ShapeDtypeStruct((T, D), F32),
        compiler_params=_params(("parallel",)),
        name="mix_out_proj",
    )(x2, a, b, w_out[:wa], w_out[wa:])


def _memkv_kernel(m_ref, g_ref, w_ref, kg_ref, k_ref, v_ref, *, nh):
    h = _rms(m_ref[...], g_ref[...]).astype(BF16)
    acc = _dot(h, w_ref[...])
    w = nh * HEAD_DIM
    for hd in range(nh):
        sl = slice(hd * HEAD_DIM, (hd + 1) * HEAD_DIM)
        k_ref[:, sl] = _rms(acc[:, sl], kg_ref[...]).astype(BF16)
    v_ref[...] = acc[:, w:].astype(BF16)


def _mem_kv(mem2, g, w_kv, kg):
    R, D = mem2.shape
    w = w_kv.shape[1] // 2
    return pl.pallas_call(
        functools.partial(_memkv_kernel, nh=w // HEAD_DIM),
        grid=(1,),
        in_specs=[_full_spec(mem2.shape), _full_spec(g.shape), _full_spec(w_kv.shape),
                  _full_spec(kg.shape)],
        out_specs=[_full_spec((R, w)), _full_spec((R, w))],
        out_shape=[jax.ShapeDtypeStruct((R, w), BF16)] * 2,
        compiler_params=_params(("arbitrary",)),
        name="mem_kv",
    )(mem2, g, w_kv, kg)


def _xa_kernel(x_ref, g_ref, wq_ref, qg_ref, k_ref, v_ref, wo_ref, o_ref, *, nh):
    x = x_ref[...]
    h = _rms(x, g_ref[...]).astype(BF16)
    qf = _dot(h, wq_ref[...])
    scale = HEAD_DIM ** -0.5
    outs = []
    for hd in range(nh):
        sl = slice(hd * HEAD_DIM, (hd + 1) * HEAD_DIM)
        q = (_rms(qf[:, sl], qg_ref[...]) * scale).astype(BF16)
        s = _dot_nt(q, k_ref[:, sl])
        s = s - jnp.max(s, axis=-1, keepdims=True)
        e = jnp.exp(s)
        p = e / jnp.sum(e, axis=-1, keepdims=True)
        outs.append(_dot(p.astype(BF16), v_ref[:, sl]).astype(BF16))
    o = jnp.concatenate(outs, axis=-1)
    o_ref[...] = x + _dot(o, wo_ref[...])


def _cross_attention(x, g, w_q, qg, k, v, w_o):
    B, S, D = x.shape
    M, W = k.shape[1], k.shape[2]
    tm = _tile(S, XA_TOKEN_TILE)
    return pl.pallas_call(
        functools.partial(_xa_kernel, nh=W // HEAD_DIM),
        grid=(B, S // tm),
        in_specs=[
            pl.BlockSpec((None, tm, D), lambda b, i: (b, i, 0)),
            pl.BlockSpec((1, D), lambda b, i: (0, 0)),
            pl.BlockSpec((D, W), lambda b, i: (0, 0)),
            pl.BlockSpec((1, HEAD_DIM), lambda b, i: (0, 0)),
            pl.BlockSpec((None, M, W), lambda b, i: (b, 0, 0)),
            pl.BlockSpec((None, M, W), lambda b, i: (b, 0, 0)),
            pl.BlockSpec((W, D), lambda b, i: (0, 0)),
        ],
        out_specs=pl.BlockSpec((None, tm, D), lambda b, i: (b, i, 0)),
        out_shape=jax.ShapeDtypeStruct((B, S, D), F32),
        compiler_params=_params(("parallel", "parallel")),
        name="mem_cross_attention",
    )(x, g, w_q, qg, k, v, w_o)


def _row(v):
    return v.reshape(1, -1).astype(F32)


def _sb_gla_mixer(x, h2, w_in3, layer, sb_qg, sb_kg, w_gate, b_gate, gla_og, w_out3):
    B, S, D = x.shape
    T = B * S
    x2 = x.reshape(T, D)
    n_heads = D // HEAD_DIM
    n_sb = n_heads // 4
    n_gla = n_heads - n_sb
    sbw = n_sb * HEAD_DIM
    kw = n_gla * GLA_DK
    vw = n_gla * HEAD_DIM
    seg = 3 * sbw
    assert 2 * kw == seg and vw == seg
    w_lr = w_in3[layer][:, 4 * seg:4 * seg + GLA_GATE_RANK]
    w_lr_p = jnp.pad(w_lr, ((0, 0), (0, LANES - GLA_GATE_RANK))).astype(BF16)
    w_gate_p = jnp.pad(w_gate, ((0, LANES - GLA_GATE_RANK), (0, 0))).astype(BF16)

    q_sb, k_sb, v_sb = _proj_call(
        functools.partial(_proj_sb_kernel, nh=n_sb), h2,
        [_weight_bf16(w_in3, layer, 0, seg), _row(sb_qg), _row(sb_kg)],
        [sbw, sbw, sbw], [BF16] * 3, "proj_sb")
    q_g, k_g, log_a = _proj_call(
        functools.partial(_proj_gla_kernel, kw=kw), h2,
        [_weight_bf16(w_in3, layer, 1, seg), w_lr_p, w_gate_p, _row(b_gate)],
        [kw, kw, kw], [F32] * 3, "proj_gla_qk")
    (v_g,) = _proj_call(functools.partial(_proj_plain_kernel, silu=False), h2,
                        [_weight_bf16(w_in3, layer, 2, seg)], [vw], [BF16], "proj_gla_v")
    (r_g,) = _proj_call(functools.partial(_proj_plain_kernel, silu=True), h2,
                        [_weight_bf16(w_in3, layer, 3, seg)], [vw], [BF16], "proj_gla_r")

    o_sb = _stick_breaking(q_sb.reshape(B, S, sbw), k_sb.reshape(B, S, sbw),
                           v_sb.reshape(B, S, sbw), n_sb)
    o_g = _gla(q_g.reshape(B, S, kw), k_g.reshape(B, S, kw), log_a.reshape(B, S, kw),
               v_g.reshape(B, S, vw), r_g.reshape(B, S, vw), _row(gla_og))
    out = _out_proj(x2, o_sb.reshape(T, sbw), o_g.reshape(T, vw), _weight_bf16(w_out3, layer))
    return out.reshape(B, S, D)


def kernel(x, mem, ffn1_norm, ffn1_w_gu, ffn1_w_down, mix_norm, ab_w_in, sb_q_norm, sb_k_norm,
           gla_w_gate, gla_b_gate, gla_o_norm, ab_w_out, conv_w_in, conv_w, conv_w_out, xa_norm,
           mem_norm, xa_w_q, xa_w_kv, xa_q_norm, xa_k_norm, xa_w_o, ffn2_norm, ffn2_w_gu,
           ffn2_w_down):
    B, S, D = x.shape
    T = B * S
    depth = ffn1_norm.shape[0]
    mem2 = mem.reshape(-1, D)
    for layer in range(depth):
        i = layer // 2
        even = layer % 2 == 0
        ffn1 = _swiglu_ffn(x.reshape(T, D), _row(ffn1_norm[layer]), _weight_bf16(ffn1_w_gu, layer),
                           _weight_bf16(ffn1_w_down, layer), scale=0.5,
                           next_gain=_row(mix_norm[layer]) if even else None)
        if even:
            x = _sb_gla_mixer(ffn1[0].reshape(B, S, D), ffn1[1], ab_w_in, i, sb_q_norm[i],
                              sb_k_norm[i], gla_w_gate[i], gla_b_gate[i], gla_o_norm[i], ab_w_out)
        else:
            x = _conv_mixer(ffn1.reshape(B, S, D), _row(mix_norm[layer]), _weight_bf16(conv_w_in, i),
                            conv_w[i].astype(F32), _weight_bf16(conv_w_out, i))
        k_m, v_m = _mem_kv(mem2, _row(mem_norm[layer]), _weight_bf16(xa_w_kv, layer),
                           _row(xa_k_norm[layer]))
        xw = k_m.shape[1]
        x = _cross_attention(x, _row(xa_norm[layer]), _weight_bf16(xa_w_q, layer),
                             _row(xa_q_norm[layer]), k_m.reshape(B, -1, xw),
                             v_m.reshape(B, -1, xw), _weight_bf16(xa_w_o, layer))
        x = _swiglu_ffn(x.reshape(T, D), _row(ffn2_norm[layer]), _weight_bf16(ffn2_w_gu, layer),
                        _weight_bf16(ffn2_w_down, layer), scale=0.5).reshape(B, S, D)
    return x
```

```python
import functools

import jax
import jax.numpy as jnp
from jax import lax
from jax.experimental import pallas as pl
from jax.experimental.pallas import tpu as pltpu

F32 = jnp.float32
BF16 = jnp.bfloat16

EPS = 1e-6
HEAD_DIM = 128
GLA_DK = 64
GLA_GATE_RANK = 16
GLA_GATE_TAU = 16.0
LANES = 128
LOG2E = 1.4426950408889634
UNDERFLOW_LOG2 = 160.0
SCORE_BOUND_SLACK = 1.001
CONV_SPLIT = 2
DIAG_BLOCK = 16
GLA_MODERATE_GATE = 5.0
CAST_BLOCK_BYTES = 8 * 1024 * 1024
TOKEN_TILE = 512
PROJ_TOKEN_TILE = 1024
FFN_HIDDEN_TILE = 1024
CONV_CHANNEL_TILE = 512
XA_TOKEN_TILE = 1024
SB_TILE = 256
SB_SPAN_TILES = 2
SB_TILES_PER_STEP = 4
GLA_CHUNK = 256
GLA_CHUNKS_PER_STEP = 1
VMEM_LIMIT = 56 * 1024 * 1024

_NT = (((1,), (1,)), ((), ()))
_TN = (((0,), (0,)), ((), ()))


def _params(sem):
    return pltpu.CompilerParams(dimension_semantics=sem, vmem_limit_bytes=VMEM_LIMIT)


def _dot(a, b):
    return jnp.dot(a, b, preferred_element_type=F32)


def _dot_nt(a, b):
    return lax.dot_general(a, b, _NT, preferred_element_type=F32)


def _dot_tn(a, b):
    return lax.dot_general(a, b, _TN, preferred_element_type=F32)


def _rms(xf, g):
    ms = jnp.mean(xf * xf, axis=-1, keepdims=True)
    return xf * lax.rsqrt(ms + EPS) * g


def _sigmoid(x):
    return 1.0 / (1.0 + jnp.exp(-x))


def _softplus(x):
    return jnp.maximum(x, 0.0) + jnp.log(1.0 + jnp.exp(-jnp.abs(x)))


def _tile(n, pref):
    t = min(pref, n)
    while n % t:
        t //= 2
    return t


def _cast_kernel(w_ref, o_ref):
    o_ref[...] = w_ref[...].astype(BF16)


def _weight_bf16(w3, layer, col_block=0, ncols=None):
    _, K, N = w3.shape
    ncols = N if ncols is None else ncols
    limit = max(8, CAST_BLOCK_BYTES // (4 * ncols))
    rb = _tile(K, 1 << (limit.bit_length() - 1))
    return pl.pallas_call(
        _cast_kernel,
        grid=(K // rb,),
        in_specs=[pl.BlockSpec((None, rb, ncols), lambda r: (layer, r, col_block))],
        out_specs=pl.BlockSpec((rb, ncols), lambda r: (r, 0)),
        out_shape=jax.ShapeDtypeStruct((K, ncols), BF16),
        compiler_params=_params(("parallel",)),
        name="weight_to_bf16",
    )(w3)


def _glu_kernel(x_ref, g_ref, wa_ref, wb_ref, wd_ref, *rest, scale, nf, next_norm):
    if next_norm:
        g2_ref, o_ref, hn_ref, h_ref = rest
    else:
        o_ref, h_ref = rest
    i = pl.program_id(0)
    f = pl.program_id(1)
    slot = i % 2

    @pl.when(jnp.logical_and(i == 0, f == 0))
    def _():
        h_ref[0] = _rms(x_ref[...], g_ref[...]).astype(BF16)

    def down():
        h = h_ref[slot]
        a = _dot(h, wa_ref[...])
        b = _dot(h, wb_ref[...])
        return _dot((a * _sigmoid(a) * b).astype(BF16), wd_ref[...])

    @pl.when(f == 0)
    def _():
        o_ref[...] = x_ref[...] * (1.0 / scale) + down()

    @pl.when(jnp.logical_and(f > 0, f < nf - 1))
    def _():
        o_ref[...] += down()

    @pl.when(f == nf - 1)
    def _():
        out = scale * (o_ref[...] + down())
        o_ref[...] = out
        if next_norm:
            hn_ref[...] = _rms(out, g2_ref[...]).astype(BF16)
        h_ref[1 - slot] = _rms(x_ref[...], g_ref[...]).astype(BF16)


def _swiglu_ffn(x2, g, w_gu, w_d, *, scale, next_gain=None):
    T, D = x2.shape
    F = w_d.shape[0]
    tm = _tile(T, TOKEN_TILE)
    tf = _tile(F, FFN_HIDDEN_TILE)
    nf = F // tf
    nt = T // tm
    assert nf >= 2
    next_norm = next_gain is not None
    row = pl.BlockSpec((tm, D), lambda i, f: (i, 0))
    vec = pl.BlockSpec((1, D), lambda i, f: (0, 0))
    in_specs = [
        pl.BlockSpec((tm, D), lambda i, f: (jnp.minimum(i + jnp.minimum(f, 1), nt - 1), 0)),
        vec,
        pl.BlockSpec((D, tf), lambda i, f: (0, f)),
        pl.BlockSpec((D, tf), lambda i, f: (0, f + nf)),
        pl.BlockSpec((tf, D), lambda i, f: (f, 0)),
    ]
    args = [x2, g, w_gu, w_gu, w_d]
    out_specs = row
    out_shape = jax.ShapeDtypeStruct((T, D), F32)
    if next_norm:
        in_specs.append(vec)
        args.append(next_gain)
        out_specs = [row, row]
        out_shape = [out_shape, jax.ShapeDtypeStruct((T, D), BF16)]
    return pl.pallas_call(
        functools.partial(_glu_kernel, scale=scale, nf=nf, next_norm=next_norm),
        grid=(nt, nf),
        in_specs=in_specs,
        out_specs=out_specs,
        out_shape=out_shape,
        scratch_shapes=[pltpu.VMEM((2, tm, D), BF16)],
        compiler_params=_params(("arbitrary", "arbitrary")),
        name="swiglu_ffn",
    )(*args)


def _conv_kernel(x_ref, g_ref, wb_ref, wc_ref, wu_ref, cw_ref, wo_ref, o_ref, h_ref, halo_ref,
                 *, nc):
    i = pl.program_id(1)
    c = pl.program_id(2)
    t = pl.program_id(0) * pl.num_programs(1) + i
    slot = t % 2

    @pl.when(jnp.logical_and(t == 0, c == 0))
    def _():
        h_ref[0] = _rms(x_ref[...], g_ref[...]).astype(BF16)

    @pl.when(i == 0)
    def _():
        halo_ref[c] = jnp.zeros(halo_ref.shape[1:], F32)

    def mix():
        h = h_ref[slot]
        tm, tc = h.shape[0], wb_ref.shape[1]
        prev = halo_ref[c]
        row = lax.broadcasted_iota(jnp.int32, (tm, tc // CONV_SPLIT), 0)
        acc = None
        for s in range(CONV_SPLIT):
            sl = slice(s * (tc // CONV_SPLIT), (s + 1) * (tc // CONV_SPLIT))
            bg = _dot(h, wb_ref[:, sl])
            cu = _dot(h, wc_ref[:, sl]) * _dot(h, wu_ref[:, sl])
            p1 = prev[7:8, sl]
            p2 = prev[6:7, sl]
            r1 = jnp.where(row == 0, p1, pltpu.roll(cu, 1, 0))
            r2 = jnp.where(row == 0, p2, jnp.where(row == 1, p1, pltpu.roll(cu, 2, 0)))
            taps = cw_ref[:, pl.ds(pl.multiple_of(c * tc + sl.start, LANES), sl.stop - sl.start)]
            y = taps[0:1] * r2 + taps[1:2] * r1 + taps[2:3] * cu
            halo_ref[c, :, sl] = cu[tm - 8:, :]
            d = _dot((bg * y).astype(BF16), wo_ref[sl, :])
            acc = d if acc is None else acc + d
        return acc

    @pl.when(c == 0)
    def _():
        o_ref[...] = x_ref[...] + mix()

    @pl.when(jnp.logical_and(c > 0, c < nc - 1))
    def _():
        o_ref[...] += mix()

    @pl.when(c == nc - 1)
    def _():
        o_ref[...] += mix()
        h_ref[1 - slot] = _rms(x_ref[...], g_ref[...]).astype(BF16)


def _conv_mixer(x, g, w_in, conv_w, w_out):
    B, S, D = x.shape
    tm = _tile(S, TOKEN_TILE)
    tc = _tile(D, CONV_CHANNEL_TILE)
    nc = D // tc
    ni = S // tm
    assert nc >= 2

    def x_map(b, i, c):
        t = jnp.minimum(b * ni + i + jnp.minimum(c, 1), B * ni - 1)
        return (t // ni, t % ni, 0)

    return pl.pallas_call(
        functools.partial(_conv_kernel, nc=nc),
        grid=(B, ni, nc),
        in_specs=[
            pl.BlockSpec((None, tm, D), x_map),
            pl.BlockSpec((1, D), lambda b, i, c: (0, 0)),
            pl.BlockSpec((D, tc), lambda b, i, c: (0, c)),
            pl.BlockSpec((D, tc), lambda b, i, c: (0, c + nc)),
            pl.BlockSpec((D, tc), lambda b, i, c: (0, c + 2 * nc)),
            pl.BlockSpec((3, D), lambda b, i, c: (0, 0)),
            pl.BlockSpec((tc, D), lambda b, i, c: (c, 0)),
        ],
        out_specs=pl.BlockSpec((None, tm, D), lambda b, i, c: (b, i, 0)),
        out_shape=jax.ShapeDtypeStruct((B, S, D), F32),
        scratch_shapes=[pltpu.VMEM((2, tm, D), BF16), pltpu.VMEM((nc, 8, tc), F32)],
        compiler_params=_params(("arbitrary", "arbitrary", "arbitrary")),
        name="conv_mixer",
    )(x, g, w_in, w_in, w_in, conv_w, w_out)


def _proj_sb_kernel(h_ref, w_ref, qg_ref, kg_ref, q_ref, k_ref, v_ref, *, nh):
    acc = _dot(h_ref[...], w_ref[...])
    w = nh * HEAD_DIM
    qscale = LOG2E * HEAD_DIM ** -0.5
    for hd in range(nh):
        lo = hd * HEAD_DIM
        q = acc[:, lo:lo + HEAD_DIM]
        k = acc[:, w + lo:w + lo + HEAD_DIM]
        q_ref[:, lo:lo + HEAD_DIM] = (_rms(q, qg_ref[...]) * qscale).astype(BF16)
        k_ref[:, lo:lo + HEAD_DIM] = _rms(k, kg_ref[...]).astype(BF16)
    v_ref[...] = acc[:, 2 * w:].astype(BF16)


def _proj_gla_kernel(h_ref, w_ref, wlr_ref, wg_ref, bg_ref, q_ref, k_ref, la_ref, *, kw):
    h = h_ref[...]
    acc = _dot(h, w_ref[...])
    q_ref[...] = acc[:, :kw] * (GLA_DK ** -0.5)
    k_ref[...] = acc[:, kw:]
    low_rank = _dot(h, wlr_ref[...])
    logit = _dot(low_rank.astype(BF16), wg_ref[...]) + bg_ref[...]
    la_ref[...] = -_softplus(-logit) * (1.0 / GLA_GATE_TAU)


def _proj_plain_kernel(h_ref, w_ref, o_ref, *, silu):
    acc = _dot(h_ref[...], w_ref[...])
    if silu:
        acc = acc * _sigmoid(acc)
    o_ref[...] = acc.astype(BF16)


def _row_spec(tm, n):
    return pl.BlockSpec((tm, n), lambda i: (i, 0))


def _full_spec(shape):
    return pl.BlockSpec(shape, lambda i: (0,) * len(shape))


def _proj_call(kernel, h2, extra_in, out_widths, out_dtypes, name):
    T, D = h2.shape
    tm = _tile(T, PROJ_TOKEN_TILE)
    in_specs = [_row_spec(tm, D)] + [_full_spec(a.shape) for a in extra_in]
    return pl.pallas_call(
        kernel,
        grid=(T // tm,),
        in_specs=in_specs,
        out_specs=[_row_spec(tm, n) for n in out_widths],
        out_shape=[jax.ShapeDtypeStruct((T, n), dt) for n, dt in zip(out_widths, out_dtypes)],
        compiler_params=_params(("parallel",)),
        name=name,
    )(h2, *extra_in)


def _sb_kernel(q_ref, k_ref, v_ref, o_ref, acc_ref, run_ref, kmax_ref, *, tq, tk, group, nsub):
    step = pl.program_id(2)
    span = tk * group
    n_keys = k_ref.shape[0]

    @pl.when(step == 0)
    def _():
        def body(n, m):
            kk = k_ref[pl.ds(pl.multiple_of(n * span, span), span), :].astype(F32)
            return jnp.maximum(m, jnp.max(jnp.sum(kk * kk, axis=1, keepdims=True), axis=0, keepdims=True))
        m = lax.fori_loop(0, n_keys // span, body, jnp.zeros((1, 1), F32))
        kmax_ref[...] = jnp.sqrt(m)

    acc_ref[...] = jnp.zeros_like(acc_ref)
    run_ref[...] = jnp.zeros_like(run_ref)
    row = lax.broadcasted_iota(jnp.int32, (tq, tk), 0)
    col = lax.broadcasted_iota(jnp.int32, (tq, tk), 1)
    krow = lax.broadcasted_iota(jnp.int32, (tk, tk), 0)
    kcol = lax.broadcasted_iota(jnp.int32, (tk, tk), 1)
    suffix = (krow >= kcol).astype(BF16)

    def key_tiles(r, i, q, first_tile, ntiles, masked):
        base = pl.multiple_of(first_tile * tk, tk)
        run = run_ref[r]
        ws = [None] * ntiles
        for c in reversed(range(ntiles)):
            start = pl.multiple_of(base + c * tk, tk)
            z = _dot_nt(q, k_ref[pl.ds(start, tk), :])
            sp = jnp.maximum(z, 0.0) + jnp.log(1.0 + jnp.exp2(-jnp.abs(z))) * LOG2E
            if masked:
                causal = (col + start) < (row + i * tq)
                sp = jnp.where(causal, sp, 0.0)
            within = _dot(sp.astype(BF16), suffix)
            w = jnp.exp2(z - within - run)
            if masked:
                w = jnp.where(causal, w, 0.0)
            ws[c] = w.astype(BF16)
            run = run + jnp.sum(sp, axis=1, keepdims=True)
        acc_ref[r] += _dot(jnp.concatenate(ws, axis=1), v_ref[pl.ds(base, ntiles * tk), :])
        run_ref[r] = run

    subs = []
    for r in range(nsub):
        i = step * nsub + r
        q = q_ref[r * tq:(r + 1) * tq, :]
        qf = q.astype(F32)
        zmax = jnp.sqrt(jnp.sum(qf * qf, axis=1, keepdims=True)) * kmax_ref[...] * SCORE_BOUND_SLACK
        before = jnp.maximum(i - (group - 1), 0)
        key_tiles(r, i, q, before, group, True)
        subs.append((i, q, zmax, before))

    for r, (i, q, zmax, before) in enumerate(subs):
        def any_weight_left(r=r, zmax=zmax):
            return jnp.max(zmax - run_ref[r]) > -UNDERFLOW_LOG2

        def sweep(count, first_of, ntiles, alive, r=r, i=i, q=q, any_weight_left=any_weight_left):
            def cond(carry):
                n, live = carry
                return jnp.logical_and(n < count, live)

            def body(carry):
                n, _ = carry
                key_tiles(r, i, q, first_of(n), ntiles, False)
                return n + 1, any_weight_left()

            return lax.while_loop(cond, body, (jnp.int32(0), alive))[1]

        alive = sweep(before // group, lambda n, before=before: before - group * (n + 1), group,
                      any_weight_left())
        sweep(before % group, lambda n, before=before: before % group - 1 - n, 1, alive)
        o_ref[r * tq:(r + 1) * tq, :] = acc_ref[r].astype(BF16)


def _stick_breaking(q, k, v, nh):
    B, S, _ = q.shape
    tq = _tile(S, SB_TILE)
    tk = tq
    group = _tile(S // tk, SB_SPAN_TILES)
    nsub = _tile(S // tq, SB_TILES_PER_STEP)
    rows = tq * nsub
    return pl.pallas_call(
        functools.partial(_sb_kernel, tq=tq, tk=tk, group=group, nsub=nsub),
        grid=(B, nh, S // rows),
        in_specs=[
            pl.BlockSpec((None, rows, HEAD_DIM), lambda b, h, i: (b, i, h)),
            pl.BlockSpec((None, S, HEAD_DIM), lambda b, h, i: (b, 0, h)),
            pl.BlockSpec((None, S, HEAD_DIM), lambda b, h, i: (b, 0, h)),
        ],
        out_specs=pl.BlockSpec((None, rows, HEAD_DIM), lambda b, h, i: (b, i, h)),
        out_shape=jax.ShapeDtypeStruct(q.shape, BF16),
        scratch_shapes=[pltpu.VMEM((nsub, tq, HEAD_DIM), F32), pltpu.VMEM((nsub, tq, 1), F32),
                        pltpu.VMEM((1, 1), F32)],
        compiler_params=_params(("arbitrary", "arbitrary", "arbitrary")),
        name="stick_breaking",
    )(q, k, v)


def _split2(x):
    a = x.astype(BF16)
    return a, (x - a.astype(F32)).astype(BF16)


def _gla_kernel(q_ref, k_ref, la_ref, v_ref, r_ref, gn_ref, o_ref, s_ref, *, chunk, nchunk, npair):
    n = pl.program_id(1)

    @pl.when(n == 0)
    def _():
        s_ref[...] = jnp.zeros_like(s_ref)

    C = chunk
    row = lax.broadcasted_iota(jnp.int32, (C, C), 0)
    col = lax.broadcasted_iota(jnp.int32, (C, C), 1)
    rcol = lax.broadcasted_iota(jnp.int32, (C, 1), 0)
    lane = lax.broadcasted_iota(jnp.int32, (C, LANES), 1)
    first = lane < GLA_DK
    sq_first = lax.broadcasted_iota(jnp.int32, (HEAD_DIM, LANES), 1) < GLA_DK
    zero = jnp.zeros((), BF16)

    def chunk_body(rows, fine):
        q = q_ref[rows, :]
        k = k_ref[rows, :]
        la = la_ref[rows, :]
        g_hi, g_lo = _split2(la)

        def rowsum(mask):
            m = mask.astype(BF16)
            return _dot(m, g_hi) + _dot(m, g_lo)

        def block_row(x, blk, at):
            x3 = x.reshape(C // blk, blk, x.shape[1])
            return jnp.broadcast_to(x3[:, at:at + 1, :], x3.shape).reshape(x.shape)

        b = rowsum(col <= row)
        b_last = b[C - 1:C, :]

        if fine:
            levels = [(q.astype(BF16), k.astype(BF16), row == col)]
            m = 1
        else:
            ref = block_row(b, DIAG_BLOCK, DIAG_BLOCK // 2 - 1)
            levels = [((q * jnp.exp(b - ref)).astype(BF16), (k * jnp.exp(ref - b)).astype(BF16),
                       jnp.logical_and(row // DIAG_BLOCK == col // DIAG_BLOCK, col <= row))]
            m = DIAG_BLOCK
        while m < C:
            if fine:
                ref = rowsum(col <= (row // (2 * m)) * (2 * m) + (m - 1))
            else:
                ref = block_row(b, 2 * m, m - 1)
            odd = (rcol // m) % 2 == 1
            qs = jnp.where(odd, q * jnp.exp(jnp.where(odd, b - ref, 0.0)), 0.0)
            ks = jnp.where(odd, 0.0, k * jnp.exp(jnp.where(odd, 0.0, ref - b)))
            levels.append((qs.astype(BF16), ks.astype(BF16), (row // (2 * m)) == (col // (2 * m))))
            m *= 2

        q_inter = (q * jnp.exp(b)).astype(BF16)
        k_state = (k * jnp.exp(b_last - b)).astype(BF16)

        for p in range(npair):
            ksl = slice(p * LANES, (p + 1) * LANES)
            state = s_ref[p]
            state_b = state.astype(BF16)
            scores = jnp.zeros((2 * C, C), F32)
            for qs, ks, mask in levels:
                qp = qs[:, ksl]
                q2 = jnp.concatenate([jnp.where(first, qp, zero), jnp.where(first, zero, qp)], axis=0)
                mask2 = jnp.concatenate([mask, mask], axis=0)
                scores += jnp.where(mask2, _dot_nt(q2, ks[:, ksl]), 0.0)
            qi = q_inter[:, ksl]
            inter = _dot_nt(jnp.concatenate([jnp.where(first, qi, zero), jnp.where(first, zero, qi)],
                                            axis=0), state_b)
            kv = []
            for e in range(2):
                hd = 2 * p + e
                vsl = slice(hd * HEAD_DIM, (hd + 1) * HEAD_DIM)
                half = slice(e * C, (e + 1) * C)
                v = v_ref[rows, vsl]
                o = _dot(scores[half].astype(BF16), v) + inter[half]
                o = _rms(o, gn_ref[...]) * r_ref[rows, vsl].astype(F32)
                o_ref[rows, vsl] = o.astype(BF16)
                kv.append(_dot_tn(v, k_state[:, ksl]))
            s_ref[p] = jnp.exp(b_last[:, ksl]) * state + jnp.where(sq_first, kv[0], kv[1])

    for j in range(nchunk):
        rows = slice(j * C, (j + 1) * C)
        moderate = jnp.min(la_ref[rows, :]) > -GLA_MODERATE_GATE

        @pl.when(moderate)
        def _():
            chunk_body(rows, fine=False)

        @pl.when(jnp.logical_not(moderate))
        def _():
            chunk_body(rows, fine=True)


def _gla(q, k, la, v, r, gn):
    B, S, kw = q.shape
    vw = v.shape[-1]
    chunk = _tile(S, GLA_CHUNK)
    nchunk = _tile(S // chunk, GLA_CHUNKS_PER_STEP)
    rows = chunk * nchunk
    npair = kw // LANES
    blk = lambda w: pl.BlockSpec((None, rows, w), lambda b, n: (b, n, 0))
    return pl.pallas_call(
        functools.partial(_gla_kernel, chunk=chunk, nchunk=nchunk, npair=npair),
        grid=(B, S // rows),
        in_specs=[blk(kw), blk(kw), blk(kw), blk(vw), blk(vw),
                  pl.BlockSpec((1, HEAD_DIM), lambda b, n: (0, 0))],
        out_specs=blk(vw),
        out_shape=jax.ShapeDtypeStruct(v.shape, BF16),
        scratch_shapes=[pltpu.VMEM((npair, HEAD_DIM, LANES), F32)],
        compiler_params=_params(("arbitrary", "arbitrary")),
        name="gla",
    )(q, k, la, v, r, gn)


def _outproj_kernel(x_ref, a_ref, b_ref, wa_ref, wb_ref, o_ref):
    o_ref[...] = x_ref[...] + _dot(a_ref[...], wa_ref[...]) + _dot(b_ref[...], wb_ref[...])


def _out_proj(x2, a, b, w_out):
    T, D = x2.shape
    wa, wb = a.shape[1], b.shape[1]
    tm = _tile(T, TOKEN_TILE)
    return pl.pallas_call(
        _outproj_kernel,
        grid=(T // tm,),
        in_specs=[
            _row_spec(tm, D), _row_spec(tm, wa), _row_spec(tm, wb),
            pl.BlockSpec((wa, D), lambda i: (0, 0)),
            pl.BlockSpec((wb, D), lambda i: (0, 0)),
        ],
        out_specs=_row_spec(tm, D),
        out_shape=jax.ShapeDtypeStruct((T, D), F32),
        compiler_params=_params(("parallel",)),
        name="mix_out_proj",
    )(x2, a, b, w_out[:wa], w_out[wa:])


def _memkv_kernel(m_ref, g_ref, w_ref, kg_ref, k_ref, v_ref, *, nh):
    h = _rms(m_ref[...], g_ref[...]).astype(BF16)
    acc = _dot(h, w_ref[...])
    w = nh * HEAD_DIM
    for hd in range(nh):
        sl = slice(hd * HEAD_DIM, (hd + 1) * HEAD_DIM)
        k_ref[:, sl] = _rms(acc[:, sl], kg_ref[...]).astype(BF16)
    v_ref[...] = acc[:, w:].astype(BF16)


def _mem_kv(mem2, g, w_kv, kg):
    R, D = mem2.shape
    w = w_kv.shape[1] // 2
    return pl.pallas_call(
        functools.partial(_memkv_kernel, nh=w // HEAD_DIM),
        grid=(1,),
        in_specs=[_full_spec(mem2.shape), _full_spec(g.shape), _full_spec(w_kv.shape),
                  _full_spec(kg.shape)],
        out_specs=[_full_spec((R, w)), _full_spec((R, w))],
        out_shape=[jax.ShapeDtypeStruct((R, w), BF16)] * 2,
        compiler_params=_params(("arbitrary",)),
        name="mem_kv",
    )(mem2, g, w_kv, kg)


def _xa_kernel(x_ref, g_ref, wq_ref, qg_ref, k_ref, v_ref, wo_ref, o_ref, *, nh):
    x = x_ref[...]
    h = _rms(x, g_ref[...]).astype(BF16)
    qf = _dot(h, wq_ref[...])
    scale = HEAD_DIM ** -0.5
    outs = []
    for hd in range(nh):
        sl = slice(hd * HEAD_DIM, (hd + 1) * HEAD_DIM)
        q = (_rms(qf[:, sl], qg_ref[...]) * scale).astype(BF16)
        s = _dot_nt(q, k_ref[:, sl])
        s = s - jnp.max(s, axis=-1, keepdims=True)
        e = jnp.exp(s)
        p = e / jnp.sum(e, axis=-1, keepdims=True)
        outs.append(_dot(p.astype(BF16), v_ref[:, sl]).astype(BF16))
    o = jnp.concatenate(outs, axis=-1)
    o_ref[...] = x + _dot(o, wo_ref[...])


def _cross_attention(x, g, w_q, qg, k, v, w_o):
    B, S, D = x.shape
    M, W = k.shape[1], k.shape[2]
    tm = _tile(S, XA_TOKEN_TILE)
    return pl.pallas_call(
        functools.partial(_xa_kernel, nh=W // HEAD_DIM),
        grid=(B, S // tm),
        in_specs=[
            pl.BlockSpec((None, tm, D), lambda b, i: (b, i, 0)),
            pl.BlockSpec((1, D), lambda b, i: (0, 0)),
            pl.BlockSpec((D, W), lambda b, i: (0, 0)),
            pl.BlockSpec((1, HEAD_DIM), lambda b, i: (0, 0)),
            pl.BlockSpec((None, M, W), lambda b, i: (b, 0, 0)),
            pl.BlockSpec((None, M, W), lambda b, i: (b, 0, 0)),
            pl.BlockSpec((W, D), lambda b, i: (0, 0)),
        ],
        out_specs=pl.BlockSpec((None, tm, D), lambda b, i: (b, i, 0)),
        out_shape=jax.ShapeDtypeStruct((B, S, D), F32),
        compiler_params=_params(("parallel", "parallel")),
        name="mem_cross_attention",
    )(x, g, w_q, qg, k, v, w_o)


def _row(v):
    return v.reshape(1, -1).astype(F32)


def _sb_gla_mixer(x, h2, w_in3, layer, sb_qg, sb_kg, w_gate, b_gate, gla_og, w_out3):
    B, S, D = x.shape
    T = B * S
    x2 = x.reshape(T, D)
    n_heads = D // HEAD_DIM
    n_sb = n_heads // 4
    n_gla = n_heads - n_sb
    sbw = n_sb * HEAD_DIM
    kw = n_gla * GLA_DK
    vw = n_gla * HEAD_DIM
    seg = 3 * sbw
    assert 2 * kw == seg and vw == seg
    w_lr = w_in3[layer][:, 4 * seg:4 * seg + GLA_GATE_RANK]
    w_lr_p = jnp.pad(w_lr, ((0, 0), (0, LANES - GLA_GATE_RANK))).astype(BF16)
    w_gate_p = jnp.pad(w_gate, ((0, LANES - GLA_GATE_RANK), (0, 0))).astype(BF16)

    q_sb, k_sb, v_sb = _proj_call(
        functools.partial(_proj_sb_kernel, nh=n_sb), h2,
        [_weight_bf16(w_in3, layer, 0, seg), _row(sb_qg), _row(sb_kg)],
        [sbw, sbw, sbw], [BF16] * 3, "proj_sb")
    q_g, k_g, log_a = _proj_call(
        functools.partial(_proj_gla_kernel, kw=kw), h2,
        [_weight_bf16(w_in3, layer, 1, seg), w_lr_p, w_gate_p, _row(b_gate)],
        [kw, kw, kw], [F32] * 3, "proj_gla_qk")
    (v_g,) = _proj_call(functools.partial(_proj_plain_kernel, silu=False), h2,
                        [_weight_bf16(w_in3, layer, 2, seg)], [vw], [BF16], "proj_gla_v")
    (r_g,) = _proj_call(functools.partial(_proj_plain_kernel, silu=True), h2,
                        [_weight_bf16(w_in3, layer, 3, seg)], [vw], [BF16], "proj_gla_r")

    o_sb = _stick_breaking(q_sb.reshape(B, S, sbw), k_sb.reshape(B, S, sbw),
                           v_sb.reshape(B, S, sbw), n_sb)
    o_g = _gla(q_g.reshape(B, S, kw), k_g.reshape(B, S, kw), log_a.reshape(B, S, kw),
               v_g.reshape(B, S, vw), r_g.reshape(B, S, vw), _row(gla_og))
    out = _out_proj(x2, o_sb.reshape(T, sbw), o_g.reshape(T, vw), _weight_bf16(w_out3, layer))
    return out.reshape(B, S, D)


def kernel(x, mem, ffn1_norm, ffn1_w_gu, ffn1_w_down, mix_norm, ab_w_in, sb_q_norm, sb_k_norm,
           gla_w_gate, gla_b_gate, gla_o_norm, ab_w_out, conv_w_in, conv_w, conv_w_out, xa_norm,
           mem_norm, xa_w_q, xa_w_kv, xa_q_norm, xa_k_norm, xa_w_o, ffn2_norm, ffn2_w_gu,
           ffn2_w_down):
    B, S, D = x.shape
    T = B * S
    depth = ffn1_norm.shape[0]
    mem2 = mem.reshape(-1, D)
    for layer in range(depth):
        i = layer // 2
        even = layer % 2 == 0
        ffn1 = _swiglu_ffn(x.reshape(T, D), _row(ffn1_norm[layer]), _weight_bf16(ffn1_w_gu, layer),
                           _weight_bf16(ffn1_w_down, layer), scale=0.5,
                           next_gain=_row(mix_norm[layer]) if even else None)
        if even:
            x = _sb_gla_mixer(ffn1[0].reshape(B, S, D), ffn1[1], ab_w_in, i, sb_q_norm[i],
                              sb_k_norm[i], gla_w_gate[i], gla_b_gate[i], gla_o_norm[i], ab_w_out)
        else:
            x = _conv_mixer(ffn1.reshape(B, S, D), _row(mix_norm[layer]), _weight_bf16(conv_w_in, i),
                            conv_w[i].astype(F32), _weight_bf16(conv_w_out, i))
        k_m, v_m = _mem_kv(mem2, _row(mem_norm[layer]), _weight_bf16(xa_w_kv, layer),
                           _row(xa_k_norm[layer]))
        xw = k_m.shape[1]
        x = _cross_attention(x, _row(xa_norm[layer]), _weight_bf16(xa_w_q, layer),
                             _row(xa_q_norm[layer]), k_m.reshape(B, -1, xw),
                             v_m.reshape(B, -1, xw), _weight_bf16(xa_w_o, layer))
        x = _swiglu_ffn(x.reshape(T, D), _row(ffn2_norm[layer]), _weight_bf16(ffn2_w_gu, layer),
                        _weight_bf16(ffn2_w_down, layer), scale=0.5).reshape(B, S, D)
    return x
```
